```python
import jax
import jax.numpy as jnp
from jax import lax
import numpy as np


D_MODEL = 4096
BATCH = 4
SEQ = 2048
DEPTH = 1

NSA_HEADS = 16
NSA_KV_GROUPS = 4
NSA_HEAD_DIM = 128
CMP_BLOCK = 32
CMP_STRIDE = 16
SLC_BLOCK = 64
SLC_TOPN = 16
WINDOW = 512
WIN_Q_BLOCK = 128
SLC_Q_BLOCK = 16
ML_HEADS = 8
ML_QK_DIM = 256
ML_V_DIM = 512
ML_CHUNK = 64
PEER_HEADS = 8
PEER_KEY_DIM = 256
PEER_N_KEYS = 128
PEER_N_EXPERTS = PEER_N_KEYS * PEER_N_KEYS
PEER_TOPK = 16
PEER_TOKEN_BLOCK = 64

IN_SPLITS = (NSA_HEADS * NSA_HEAD_DIM, 6 * NSA_KV_GROUPS * NSA_HEAD_DIM, 3 * NSA_HEADS, ML_HEADS * ML_QK_DIM, ML_HEADS * ML_QK_DIM, ML_HEADS * ML_V_DIM, ML_HEADS * ML_V_DIM, 2 * ML_HEADS, 2 * D_MODEL)
IN_WIDTH = sum(IN_SPLITS)

NEG_INF = -1e30
BIG = 1e9
TINY = 1e-30
EPS = 1e-6

kernel_name = 'hybrid_nsa_mlstm_peer_block'


def rms_norm(x, w):
    xf = x.astype(jnp.float32)
    y = xf * lax.rsqrt(jnp.mean(xf * xf, axis=-1, keepdims=True) + EPS)
    return (y * w.astype(jnp.float32)).astype(x.dtype)


def masked_softmax(s, mask):
    s = jnp.where(mask, s.astype(jnp.float32), NEG_INF)
    m = jnp.max(s, axis=-1, keepdims=True)
    p = jnp.where(mask, jnp.exp(s - m), 0.0)
    return p / jnp.maximum(jnp.sum(p, axis=-1, keepdims=True), TINY)


def alibi_slopes(n):
    return jnp.exp2(-8.0 * jnp.arange(1, n + 1, dtype=jnp.float32) / n)


def split_columns(z, sizes):
    out, start = [], 0
    for n in sizes:
        out.append(z[..., start:start + n])
        start += n
    return out


def nsa_attention(q, kv, gates, pos_k, pos_v, w_cmp_k, w_cmp_v):
    B, T, G, R, dk = q.shape
    dt = q.dtype
    slopes = alibi_slopes(G * R).reshape(G, R)
    k_cmp, v_cmp, k_slc, v_slc, k_win, v_win = kv
    t_pos = jnp.arange(T)
    qf = q * (dk ** -0.5)

    n_cmp = (T - CMP_BLOCK) // CMP_STRIDE + 1
    starts = jnp.arange(n_cmp) * CMP_STRIDE
    blk_idx = starts[:, None] + jnp.arange(CMP_BLOCK)[None, :]

    def compress(z, pos, w):
        zb = z[:, blk_idx] + pos[None, None, :, None, :]
        zb = jnp.moveaxis(zb, 3, 2).reshape(B, n_cmp, G, CMP_BLOCK * dk)
        return zb @ w

    kc = compress(k_cmp, pos_k, w_cmp_k)
    vc = compress(v_cmp, pos_v, w_cmp_v)
    s = jnp.einsum('btgrd,bngd->bgrtn', qf, kc).astype(jnp.float32)
    ends = starts + CMP_BLOCK - 1
    dist = (t_pos[:, None] - ends[None, :]).astype(jnp.float32)
    s = s - slopes[:, :, None, None] * dist
    p_cmp = masked_softmax(s, dist >= 0)
    o_cmp = jnp.einsum('bgrtn,bngd->btgrd', p_cmp.astype(dt), vc)

    n_slc = T // SLC_BLOCK
    n_sel = min(SLC_TOPN, n_slc)
    slc_start = jnp.arange(n_slc) * SLC_BLOCK
    overlap = ((starts[:, None] < slc_start[None, :] + SLC_BLOCK) & (starts[:, None] + CMP_BLOCK > slc_start[None, :])).astype(jnp.float32)
    imp = jnp.einsum('bgrtn,nj->bgtj', p_cmp, overlap)
    cur = t_pos // SLC_BLOCK
    j = jnp.arange(n_slc)
    causal = slc_start[None, :] <= t_pos[:, None]
    forced = (j[None, :] == 0) | (j[None, :] == cur[:, None]) | (j[None, :] == cur[:, None] - 1)
    score = jnp.where(causal, jnp.where(forced, BIG, imp), -BIG)
    _, sel = lax.top_k(score, n_sel)

    k_blk = jnp.moveaxis(k_slc.reshape(B, n_slc, SLC_BLOCK, G, dk), 3, 1)
    v_blk = jnp.moveaxis(v_slc.reshape(B, n_slc, SLC_BLOCK, G, dk), 3, 1)
    n_qb = T // SLC_Q_BLOCK

    def to_blocks(z):
        return jnp.moveaxis(z.reshape((B, n_qb, SLC_Q_BLOCK) + z.shape[2:]), 1, 0)

    q_b = to_blocks(qf)
    sel_b = to_blocks(jnp.moveaxis(sel, 1, 2))
    t_b = t_pos.reshape(n_qb, SLC_Q_BLOCK)
    b_ix = jnp.arange(B)[:, None, None, None]
    g_ix = jnp.arange(G)[None, None, :, None]

    def slc_block(args):
        qc, ic, tc = args
        kg = k_blk[b_ix, g_ix, ic]
        vg = v_blk[b_ix, g_ix, ic]
        sc = jnp.einsum('bqgrd,bqgskd->bqgrsk', qc, kg).astype(jnp.float32)
        pos = ic[..., None] * SLC_BLOCK + jnp.arange(SLC_BLOCK)
        dd = (tc[None, :, None, None, None] - pos).astype(jnp.float32)
        sc = sc - slopes[None, None, :, :, None, None] * dd[:, :, :, None]
        mask = jnp.broadcast_to((dd >= 0)[:, :, :, None], sc.shape)
        sh = sc.shape
        p = masked_softmax(sc.reshape(sh[:4] + (-1,)), mask.reshape(sh[:4] + (-1,))).reshape(sh)
        return jnp.einsum('bqgrsk,bqgskd->bqgrd', p.astype(dt), vg)

    o_slc = lax.map(slc_block, (q_b, sel_b, t_b))
    o_slc = jnp.moveaxis(o_slc, 0, 1).reshape(B, T, G, R, dk)

    n_wb = T // WIN_Q_BLOCK
    span = WINDOW + WIN_Q_BLOCK
    slab = (jnp.arange(n_wb) * WIN_Q_BLOCK)[:, None] + jnp.arange(span)[None, :]
    key_pos = slab - WINDOW
    pad = ((0, 0), (WINDOW, 0), (0, 0), (0, 0))
    kw = jnp.pad(k_win, pad)[:, slab]
    vw = jnp.pad(v_win, pad)[:, slab]
    qw = qf.reshape(B, n_wb, WIN_Q_BLOCK, G, R, dk)
    sw = jnp.einsum('bnqgrd,bnkgd->bngrqk', qw, kw).astype(jnp.float32)
    q_pos = t_pos.reshape(n_wb, WIN_Q_BLOCK)
    dw = q_pos[:, :, None] - key_pos[:, None, :]
    wmask = (dw >= 0) & (dw < WINDOW) & (key_pos[:, None, :] >= 0)
    sw = sw - slopes[None, None, :, :, None, None] * dw.astype(jnp.float32)[None, :, None, None]
    pw = masked_softmax(sw, wmask[None, :, None, None])
    o_win = jnp.einsum('bngrqk,bnkgd->bnqgrd', pw.astype(dt), vw).reshape(B, T, G, R, dk)

    g = jax.nn.sigmoid(gates.astype(jnp.float32)).astype(dt)
    o = g[..., 0:1] * o_cmp + g[..., 1:2] * o_slc + g[..., 2:3] * o_win
    return o.reshape(B, T, G * R * dk)


def mlstm_chunkwise(q, k, v, ig, fg):
    B, T, H, dqk = q.shape
    dv = v.shape[-1]
    nc = T // ML_CHUNK

    def chunks(z):
        z = z.astype(jnp.float32).reshape((B, nc, ML_CHUNK) + z.shape[2:])
        return jnp.moveaxis(jnp.moveaxis(z, 1, 0), 3, 2)

    qc = chunks(q)
    kc = chunks(k) * (dqk ** -0.5)
    vc = chunks(v)
    ic = chunks(ig)
    lfc = jax.nn.log_sigmoid(chunks(fg))
    causal = jnp.tril(jnp.ones((ML_CHUNK, ML_CHUNK), dtype=bool))

    def step(carry, xs):
        C, n, m = carry
        qt, kt, vt, it, lft = xs
        b = jnp.cumsum(lft, axis=-1)
        log_d = jnp.where(causal, b[..., :, None] - b[..., None, :] + it[..., None, :], NEG_INF)
        m_inter = b + m[..., None]
        m_t = jnp.maximum(m_inter, jnp.max(log_d, axis=-1))
        d = jnp.exp(log_d - m_t[..., None])
        s = jnp.einsum('bhtd,bhsd->bhts', qt, kt) * d
        w_inter = jnp.exp(m_inter - m_t)
        num = w_inter[..., None] * jnp.einsum('bhtd,bhdv->bhtv', qt, C) + jnp.einsum('bhts,bhsv->bhtv', s, vt)
        den = w_inter * jnp.einsum('bhtd,bhd->bht', qt, n) + jnp.sum(s, axis=-1)
        h = num / jnp.maximum(jnp.abs(den), jnp.exp(-m_t))[..., None]
        g_tot = b[..., -1]
        a = g_tot[..., None] - b + it
        m_new = jnp.maximum(g_tot + m, jnp.max(a, axis=-1))
        decay = jnp.exp(g_tot + m - m_new)
        w_tok = jnp.exp(a - m_new[..., None])
        C_new = decay[..., None, None] * C + jnp.einsum('bhsd,bhsv->bhdv', kt * w_tok[..., None], vt)
        n_new = decay[..., None] * n + jnp.einsum('bhs,bhsd->bhd', w_tok, kt)
        return (C_new, n_new, m_new), h

    init = (jnp.zeros((B, H, dqk, dv), jnp.float32), jnp.zeros((B, H, dqk), jnp.float32), jnp.zeros((B, H), jnp.float32))
    _, h = lax.scan(step, init, (qc, kc, vc, ic, lfc))
    h = jnp.moveaxis(jnp.moveaxis(h, 0, 1), 2, 3)
    return h.reshape(B, T, H, dv)


def token_mixer(u, w_in, cmp_pos, w_cmp_k, w_cmp_v, gate_bias, ml_norm_w, w_up_nsa, w_up_mlstm, w_out):
    B, T, _ = u.shape
    G, R, dk = NSA_KV_GROUPS, NSA_HEADS // NSA_KV_GROUPS, NSA_HEAD_DIM
    z = u @ w_in
    q_n, kv_n, g_n, q_m, k_m, v_m, o_m, if_m, merge = split_columns(z, IN_SPLITS)
    kvr = kv_n.reshape(B, T, 6, G, dk)
    kv = [kvr[:, :, i] for i in range(6)]
    o_nsa = nsa_attention(q_n.reshape(B, T, G, R, dk), kv, g_n.reshape(B, T, G, R, 3), cmp_pos[0], cmp_pos[1], w_cmp_k, w_cmp_v)
    ifr = if_m.reshape(B, T, 2, ML_HEADS) + gate_bias
    h_m = mlstm_chunkwise(q_m.reshape(B, T, ML_HEADS, ML_QK_DIM), k_m.reshape(B, T, ML_HEADS, ML_QK_DIM), v_m.reshape(B, T, ML_HEADS, ML_V_DIM), ifr[:, :, 0], ifr[:, :, 1])
    h_m = rms_norm(h_m.astype(u.dtype), ml_norm_w.reshape(ML_HEADS, ML_V_DIM)) * jax.nn.sigmoid(o_m).reshape(B, T, ML_HEADS, ML_V_DIM)
    y_nsa = o_nsa @ w_up_nsa
    y_ml = h_m.reshape(B, T, ML_HEADS * ML_V_DIM) @ w_up_mlstm
    g_nsa, g_ml = jnp.split(jax.nn.sigmoid(merge), 2, axis=-1)
    return (g_nsa * y_nsa + g_ml * y_ml) @ w_out


def peer_ffn(u, w_q, sub_keys, exp_u, exp_v):
    B, T, D = u.shape
    half = PEER_KEY_DIM // 2
    q = (u @ w_q).reshape(B, T, PEER_HEADS, 2, half).astype(jnp.float32)
    s1 = jnp.einsum('bthd,hnd->bthn', q[..., 0, :], sub_keys[0].astype(jnp.float32))
    s2 = jnp.einsum('bthd,hnd->bthn', q[..., 1, :], sub_keys[1].astype(jnp.float32))
    v1, i1 = lax.top_k(s1, PEER_TOPK)
    v2, i2 = lax.top_k(s2, PEER_TOPK)
    cand_s = (v1[..., :, None] + v2[..., None, :]).reshape(B, T, PEER_HEADS, PEER_TOPK * PEER_TOPK)
    cand_i = (i1[..., :, None] * PEER_N_KEYS + i2[..., None, :]).reshape(B, T, PEER_HEADS, PEER_TOPK * PEER_TOPK)
    top_s, top_pos = lax.top_k(cand_s, PEER_TOPK)
    idx = jnp.take_along_axis(cand_i, top_pos, axis=-1)
    gate = jax.nn.softmax(top_s, axis=-1).astype(u.dtype)
    nb = (B * T) // PEER_TOKEN_BLOCK
    xb = u.reshape(nb, PEER_TOKEN_BLOCK, D)
    ib = idx.reshape(nb, PEER_TOKEN_BLOCK, PEER_HEADS, PEER_TOPK)
    gb = gate.reshape(nb, PEER_TOKEN_BLOCK, PEER_HEADS, PEER_TOPK)

    def block(args):
        xt, it, gt = args
        ue = exp_u[it]
        ve = exp_v[it]
        a = jax.nn.gelu(jnp.einsum('td,thkd->thk', xt, ue), approximate=False) * gt
        return jnp.einsum('thk,thkd->td', a, ve)

    out = lax.map(block, (xb, ib, gb))
    return out.reshape(B, T, D)


def setup_inputs(seed: int = 0) -> dict:
    key = jax.random.key(seed)
    ks = jax.random.split(key, 24)
    f32 = jnp.float32
    L, D = DEPTH, D_MODEL
    nsa_width = NSA_HEADS * NSA_HEAD_DIM
    ml_width = ML_HEADS * ML_V_DIM

    def nrm(k, shape, scale):
        return jax.random.normal(k, shape, f32) * scale

    def gain(k, width):
        return 1.0 + nrm(k, (L, width), 0.02)

    gate_base = jnp.stack([jnp.zeros((ML_HEADS,), f32), jnp.linspace(3.0, 6.0, ML_HEADS, dtype=f32)])
    return {
        'x': nrm(ks[0], (BATCH, SEQ, D), 1.0),
        'c': nrm(ks[1], (BATCH, D), 1.0),
        'w_ada': nrm(ks[2], (L, D, 6 * D), 0.5 * D ** -0.5),
        'b_ada': nrm(ks[3], (L, 6 * D), 0.02),
        'norm_pre_mix': gain(ks[4], D),
        'norm_post_mix': gain(ks[5], D),
        'norm_pre_ffn': gain(ks[6], D),
        'norm_post_ffn': gain(ks[7], D),
        'w_in': nrm(ks[8], (L, D, IN_WIDTH), D ** -0.5),
        'nsa_cmp_pos': nrm(ks[9], (L, 2, CMP_BLOCK, NSA_HEAD_DIM), 0.02),
        'nsa_w_cmp_k': nrm(ks[10], (L, CMP_BLOCK * NSA_HEAD_DIM, NSA_HEAD_DIM), (CMP_BLOCK * NSA_HEAD_DIM) ** -0.5),
        'nsa_w_cmp_v': nrm(ks[11], (L, CMP_BLOCK * NSA_HEAD_DIM, NSA_HEAD_DIM), (CMP_BLOCK * NSA_HEAD_DIM) ** -0.5),
        'mlstm_gate_bias': gate_base[None] + nrm(ks[12], (L, 2, ML_HEADS), 0.1),
        'mlstm_norm_w': gain(ks[13], ml_width),
        'w_up_nsa': nrm(ks[14], (L, nsa_width, D), nsa_width ** -0.5),
        'w_up_mlstm': nrm(ks[15], (L, ml_width, D), ml_width ** -0.5),
        'w_out': nrm(ks[16], (L, D, D), D ** -0.5),
        'peer_w_q': nrm(ks[17], (L, D, PEER_HEADS * PEER_KEY_DIM), D ** -0.5),
        'peer_sub_keys': nrm(ks[18], (L, 2, PEER_HEADS, PEER_N_KEYS, PEER_KEY_DIM // 2), (PEER_KEY_DIM // 2) ** -0.5),
        'peer_u': nrm(ks[19], (L, PEER_N_EXPERTS, D), D ** -0.5),
        'peer_v': nrm(ks[20], (L, PEER_N_EXPERTS, D), PEER_HEADS ** -0.5),
    }


def reference(x, c, w_ada, b_ada, norm_pre_mix, norm_post_mix, norm_pre_ffn, norm_post_ffn, w_in, nsa_cmp_pos, nsa_w_cmp_k, nsa_w_cmp_v, mlstm_gate_bias, mlstm_norm_w, w_up_nsa, w_up_mlstm, w_out, peer_w_q, peer_sub_keys, peer_u, peer_v):
    h = x
    cond = jax.nn.silu(c)
    for l in range(DEPTH):
        mod = (cond @ w_ada[l] + b_ada[l])[:, None, :]
        sh1, sc1, gt1, sh2, sc2, gt2 = jnp.split(mod, 6, axis=-1)
        u = rms_norm(h, norm_pre_mix[l]) * (1.0 + sc1) + sh1
        y = token_mixer(u, w_in[l], nsa_cmp_pos[l], nsa_w_cmp_k[l], nsa_w_cmp_v[l], mlstm_gate_bias[l], mlstm_norm_w[l], w_up_nsa[l], w_up_mlstm[l], w_out[l])
        h = h + gt1 * rms_norm(y, norm_post_mix[l])
        u = rms_norm(h, norm_pre_ffn[l]) * (1.0 + sc2) + sh2
        y = peer_ffn(u, peer_w_q[l], peer_sub_keys[l], peer_u[l], peer_v[l])
        h = h + gt2 * rms_norm(y, norm_post_ffn[l])
    return h.astype(x.dtype)
```

```python
import functools
import math

import jax
import jax.numpy as jnp
from jax import lax
from jax.experimental import pallas as pl
from jax.experimental.pallas import tpu as pltpu

D_MODEL = 4096
BATCH = 4
SEQ = 2048
NSA_HEADS = 16
NSA_KV_GROUPS = 4
NSA_HEAD_DIM = 128
CMP_BLOCK = 32
CMP_STRIDE = 16
SLC_BLOCK = 64
SLC_TOPN = 16
WINDOW = 512
ML_HEADS = 8
ML_QK_DIM = 256
ML_V_DIM = 512
PEER_HEADS = 8
PEER_KEY_DIM = 256
PEER_N_KEYS = 128
PEER_TOPK = 16

NEG_INF = -1e30
BIG = 1e9
TINY = 1e-30
EPS = 1e-6

F32 = jnp.float32
BF16 = jnp.bfloat16

V7X_VMEM_LIMIT_BYTES = 56 * 1024 * 1024
LANES = 128
SUBLANES = 8

ROW_TILE = 256
MM_BM = 1024
MM_BN = 1024
ADA_BN = 512
NSA_TQ = 128
NSA_TK = 256
ML_CHUNK_LEN = 128
PEER_TOPK_TT = 256
PEER_TT = 512
PEER_TE = 256

_NT = (((1,), (1,)), ((), ()))


def _params(*sem):
    return pltpu.CompilerParams(dimension_semantics=sem, vmem_limit_bytes=V7X_VMEM_LIMIT_BYTES)


def _ind(cond, dtype):
    wide = jnp.int32 if jnp.issubdtype(dtype, jnp.integer) else F32
    return jnp.where(cond, jnp.ones((), wide), jnp.zeros((), wide)).astype(dtype)


def _log2(n):
    k = int(math.log2(n))
    assert 1 << k == n
    return k


def _split3(x):
    hi = x.astype(BF16)
    r1 = x - hi.astype(F32)
    mid = r1.astype(BF16)
    lo = (r1 - mid.astype(F32)).astype(BF16)
    return hi, mid, lo


def _adaln_kernel(c_ref, w_ref, b_ref, o_ref):
    c = c_ref[...]
    cond = (c * jax.nn.sigmoid(c)).astype(BF16)
    o_ref[...] = jnp.dot(cond, w_ref[...].astype(BF16), preferred_element_type=F32) + b_ref[...]


def _adaln(c_pad, w_ada, b_ada):
    rows, d = c_pad.shape
    n = w_ada.shape[1]
    bn = min(ADA_BN, n)
    return pl.pallas_call(
        _adaln_kernel,
        grid=(n // bn,),
        in_specs=[
            pl.BlockSpec((rows, d), lambda j: (0, 0)),
            pl.BlockSpec((d, bn), lambda j: (0, j)),
            pl.BlockSpec((1, bn), lambda j: (0, j)),
        ],
        out_specs=pl.BlockSpec((rows, bn), lambda j: (0, j)),
        out_shape=jax.ShapeDtypeStruct((rows, n), F32),
        compiler_params=_params("parallel"),
        name="adaln",
    )(c_pad, w_ada, b_ada.reshape(1, n))


def _rms(x, w):
    return x * lax.rsqrt(jnp.mean(x * x, axis=-1, keepdims=True) + EPS) * w


def _prenorm_kernel(x_ref, w_ref, mod_ref, u_ref):
    y = _rms(x_ref[...], w_ref[...])
    u_ref[...] = (y * (1.0 + mod_ref[1:2, :]) + mod_ref[0:1, :]).astype(u_ref.dtype)


def _prenorm(x2, w, mod3, seq):
    n, d = x2.shape
    tr = min(ROW_TILE, seq)
    return pl.pallas_call(
        _prenorm_kernel,
        grid=(n // tr,),
        in_specs=[
            pl.BlockSpec((tr, d), lambda i: (i, 0)),
            pl.BlockSpec((1, d), lambda i: (0, 0)),
            pl.BlockSpec((None, 6, d), lambda i: ((i * tr) // seq, 0, 0)),
        ],
        out_specs=pl.BlockSpec((tr, d), lambda i: (i, 0)),
        out_shape=jax.ShapeDtypeStruct((n, d), BF16),
        compiler_params=_params("parallel"),
        name="prenorm",
    )(x2, w.reshape(1, d), mod3)


def _midnorm_kernel(x_ref, y_ref, w1_ref, w2_ref, mod_ref, h_ref, u_ref):
    h = x_ref[...] + mod_ref[2:3, :] * _rms(y_ref[...].astype(F32), w1_ref[...])
    h_ref[...] = h
    u = _rms(h, w2_ref[...])
    u_ref[...] = (u * (1.0 + mod_ref[4:5, :]) + mod_ref[3:4, :]).astype(u_ref.dtype)


def _midnorm(x2, y, w_post, w_pre, mod3, seq):
    n, d = x2.shape
    tr = min(ROW_TILE, seq)
    row = pl.BlockSpec((tr, d), lambda i: (i, 0))
    vec = pl.BlockSpec((1, d), lambda i: (0, 0))
    return pl.pallas_call(
        _midnorm_kernel,
        grid=(n // tr,),
        in_specs=[row, row, vec, vec, pl.BlockSpec((None, 6, d), lambda i: ((i * tr) // seq, 0, 0))],
        out_specs=[row, row],
        out_shape=[jax.ShapeDtypeStruct((n, d), F32), jax.ShapeDtypeStruct((n, d), BF16)],
        compiler_params=_params("parallel"),
        name="midnorm",
    )(x2, y, w_post.reshape(1, d), w_pre.reshape(1, d), mod3)


def _finalnorm_kernel(h_ref, y_ref, w_ref, mod_ref, o_ref):
    o_ref[...] = h_ref[...] + mod_ref[5:6, :] * _rms(y_ref[...].astype(F32), w_ref[...])


def _finalnorm(h1, y, w, mod3, seq):
    n, d = h1.shape
    tr = min(ROW_TILE, seq)
    row = pl.BlockSpec((tr, d), lambda i: (i, 0))
    return pl.pallas_call(
        _finalnorm_kernel,
        grid=(n // tr,),
        in_specs=[row, row, pl.BlockSpec((1, d), lambda i: (0, 0)),
                  pl.BlockSpec((None, 6, d), lambda i: ((i * tr) // seq, 0, 0))],
        out_specs=row,
        out_shape=jax.ShapeDtypeStruct((n, d), F32),
        compiler_params=_params("parallel"),
        name="finalnorm",
    )(h1, y, w.reshape(1, d), mod3)


def _mm_kernel(*refs, has_gate, has_add):
    a_ref, b_ref = refs[0], refs[1]
    o_ref = refs[-1]
    acc = jnp.dot(a_ref[...], b_ref[...], preferred_element_type=F32)
    nxt = 2
    if has_gate:
        acc = acc * jax.nn.sigmoid(refs[nxt][...].astype(F32))
        nxt += 1
    if has_add:
        acc = acc + refs[nxt][...].astype(F32)
    o_ref[...] = acc.astype(o_ref.dtype)


def _matmul(a, b, out_dtype, gate=None, gate_col=0, add=None):
    m, k = a.shape
    n = b.shape[1]
    bm = min(MM_BM, m)
    bn = min(MM_BN, n)
    in_specs = [pl.BlockSpec((bm, k), lambda i, j: (i, 0)), pl.BlockSpec((k, bn), lambda i, j: (0, j))]
    args = [a, b]
    if gate is not None:
        goff = gate_col // bn
        assert goff * bn == gate_col
        in_specs.append(pl.BlockSpec((bm, bn), lambda i, j: (i, goff + j)))
        args.append(gate)
    if add is not None:
        in_specs.append(pl.BlockSpec((bm, bn), lambda i, j: (i, j)))
        args.append(add)
    return pl.pallas_call(
        functools.partial(_mm_kernel, has_gate=gate is not None, has_add=add is not None),
        grid=(m // bm, n // bn),
        in_specs=in_specs,
        out_specs=pl.BlockSpec((bm, bn), lambda i, j: (i, j)),
        out_shape=jax.ShapeDtypeStruct((m, n), out_dtype),
        compiler_params=_params("parallel", "arbitrary"),
        name="matmul",
    )(*args)


def _nsa_kernel(slopes_ref, q_ref, xk_ref, xv_ref, ks_ref, vs_ref, kw_ref, vw_ref, gate_ref, wck_ref, wcv_ref,
                pos_ref, o_ref, kc_ref, vc_ref, m_ref, l_ref, acc_ref, *, seq, tq, tk, heads_per_group):
    R = heads_per_group
    dk = NSA_HEAD_DIM
    g = pl.program_id(1)
    qi = pl.program_id(2)
    t0 = qi * tq
    n_cmp = (seq - CMP_BLOCK) // CMP_STRIDE + 1
    n_slc = seq // SLC_BLOCK
    n_sel = min(SLC_TOPN, n_slc)
    half = CMP_STRIDE * dk
    scale = dk ** -0.5
    rows = R * tq

    @pl.when(qi == 0)
    def _():
        for x_ref, w_ref, dst, pi in ((xk_ref, wck_ref, kc_ref, 0), (xv_ref, wcv_ref, vc_ref, 1)):
            x = x_ref[...]
            ya = jnp.dot(x, w_ref[0:half, :], preferred_element_type=F32)
            yb = jnp.dot(x, w_ref[half:2 * half, :], preferred_element_type=F32)
            pos = jnp.broadcast_to(pos_ref[pi], (SUBLANES, 2 * half)).astype(BF16)
            pb = jnp.dot(pos, w_ref[...], preferred_element_type=F32)[0:1, :]
            dst[...] = (ya + pltpu.roll(yb, LANES - 1, 0) + pb).astype(BF16)

    q = q_ref[...]
    qs = jnp.concatenate([q[:, r * dk:(r + 1) * dk] for r in range(R)], axis=0)

    row_id = lax.broadcasted_iota(jnp.int32, (rows, 1), 0)
    t_col = t0 + (row_id & (tq - 1))
    head_of_row = row_id >> _log2(tq)
    slope_col = jnp.zeros((rows, 1), F32)
    for r in range(R):
        slope_col = jnp.where(head_of_row == r, slopes_ref[g, r], slope_col)

    def masked_softmax(s, mask):
        s = jnp.where(mask, s, NEG_INF)
        m = jnp.max(s, axis=-1, keepdims=True)
        p = jnp.where(mask, jnp.exp(s - m), 0.0)
        return p / jnp.maximum(jnp.sum(p, axis=-1, keepdims=True), TINY)

    n_idx = lax.broadcasted_iota(jnp.int32, (rows, LANES), 1)
    dist_i = t_col - (n_idx * CMP_STRIDE + CMP_BLOCK - 1)
    cvalid = (dist_i >= 0) & (n_idx < n_cmp)
    sc = lax.dot_general(qs, kc_ref[...], _NT, preferred_element_type=F32) * scale
    p_cmp = masked_softmax(sc - slope_col * dist_i.astype(F32), cvalid)
    o_cmp = jnp.dot(p_cmp.astype(BF16), vc_ref[...], preferred_element_type=F32)

    psum = p_cmp[0:tq]
    for r in range(1, R):
        psum = psum + p_cmp[r * tq:(r + 1) * tq]
    jj = lax.broadcasted_iota(jnp.int32, (LANES, LANES), 0)
    nn = lax.broadcasted_iota(jnp.int32, (LANES, LANES), 1)
    ovl = _ind((nn * CMP_STRIDE < jj * SLC_BLOCK + SLC_BLOCK) & (nn * CMP_STRIDE + CMP_BLOCK > jj * SLC_BLOCK)
               & (jj < n_slc) & (nn < n_cmp), BF16)
    imp_t = jnp.zeros((LANES, tq), F32)
    for part in _split3(psum):
        imp_t = imp_t + lax.dot_general(ovl, part, _NT, preferred_element_type=F32)
    imp_t = imp_t[0:n_slc]
    j_i = lax.broadcasted_iota(jnp.int32, (n_slc, tq), 0)
    t_i = t0 + lax.broadcasted_iota(jnp.int32, (n_slc, tq), 1)
    cur = t_i >> _log2(SLC_BLOCK)
    causal = j_i * SLC_BLOCK <= t_i
    forced = (j_i == 0) | (j_i == cur) | (j_i == cur - 1)
    score = jnp.where(causal, jnp.where(forced, BIG, imp_t), -BIG)
    rank = jnp.zeros((n_slc, tq), jnp.int32)
    for jp in range(n_slc):
        row = score[jp:jp + 1, :]
        beats = (row > score) | ((row == score) & (j_i > jp))
        rank = rank + _ind(beats, jnp.int32)
    sel_t = _ind(rank < n_sel, F32)
    sel_t = jnp.concatenate([sel_t, jnp.zeros((LANES - n_slc, tq), F32)], axis=0)
    sel = sel_t.T.astype(BF16)

    m_ref[...] = jnp.full((rows, LANES), NEG_INF, F32)
    l_ref[...] = jnp.zeros((rows, LANES), F32)
    acc_ref[...] = jnp.zeros((rows, dk), F32)
    n_kt = (t0 + tq + tk - 1) // tk

    def slc_body(kt, carry):
        k0 = pl.multiple_of(kt * tk, tk)
        kb = ks_ref[pl.ds(k0, tk), :]
        vb = vs_ref[pl.ds(k0, tk), :]
        s = lax.dot_general(qs, kb, _NT, preferred_element_type=F32) * scale
        ej = lax.broadcasted_iota(jnp.int32, (LANES, tk), 0)
        ec = lax.broadcasted_iota(jnp.int32, (LANES, tk), 1)
        expand = _ind(ej == ((k0 + ec) >> _log2(SLC_BLOCK)), BF16)
        selm = jnp.dot(sel, expand, preferred_element_type=F32)
        selm = jnp.concatenate([selm] * R, axis=0)
        pos = k0 + lax.broadcasted_iota(jnp.int32, (rows, tk), 1)
        dd_i = t_col - pos
        mask = (selm > 0.5) & (dd_i >= 0)
        s = jnp.where(mask, s - slope_col * dd_i.astype(F32), NEG_INF)
        m_old = m_ref[...]
        m_new = jnp.maximum(m_old, jnp.max(s, axis=-1, keepdims=True))
        p = jnp.where(mask, jnp.exp(s - m_new[:, 0:1]), 0.0)
        alpha = jnp.exp(m_old - m_new)
        l_ref[...] = alpha * l_ref[...] + jnp.sum(p, axis=-1, keepdims=True)
        acc_ref[...] = alpha * acc_ref[...] + jnp.dot(p.astype(BF16), vb, preferred_element_type=F32)
        m_ref[...] = m_new
        return carry

    lax.fori_loop(0, n_kt, slc_body, 0)
    o_slc = acc_ref[...] / jnp.maximum(l_ref[...], TINY)

    span = WINDOW + tq
    ws = pl.multiple_of(jnp.maximum(t0 - WINDOW, 0), tq)
    kwb = kw_ref[pl.ds(ws, span), :]
    vwb = vw_ref[pl.ds(ws, span), :]
    sw = lax.dot_general(qs, kwb, _NT, preferred_element_type=F32) * scale
    dw_i = t_col - (ws + lax.broadcasted_iota(jnp.int32, (rows, span), 1))
    wmask = (dw_i >= 0) & (dw_i < WINDOW)
    p_win = masked_softmax(sw - slope_col * dw_i.astype(F32), wmask)
    o_win = jnp.dot(p_win.astype(BF16), vwb, preferred_element_type=F32)

    gts = jax.nn.sigmoid(gate_ref[...])
    for r in range(R):
        sl = slice(r * tq, (r + 1) * tq)
        o = (gts[:, 3 * r:3 * r + 1] * o_cmp[sl] + gts[:, 3 * r + 1:3 * r + 2] * o_slc[sl]
             + gts[:, 3 * r + 2:3 * r + 3] * o_win[sl])
        o_ref[:, r * dk:(r + 1) * dk] = o.astype(o_ref.dtype)


def _nsa(slopes, z, x_cmp, gates, wck, wcv, pos, *, batch, seq, kv_col):
    G = NSA_KV_GROUPS
    R = NSA_HEADS // G
    dk = NSA_HEAD_DIM
    tq = NSA_TQ
    tk = NSA_TK
    nq = seq // tq
    assert seq // CMP_STRIDE == LANES and CMP_BLOCK == 2 * CMP_STRIDE and seq >= WINDOW + tq and seq % tk == 0
    kv0 = kv_col // dk

    def kv_spec(i):
        return pl.BlockSpec((seq, dk), lambda b, g, qi: (b, kv0 + i * G + g))

    x_spec_k = pl.BlockSpec((None, None, None, LANES, CMP_STRIDE * dk), lambda b, g, qi: (b, 0, g, 0, 0))
    x_spec_v = pl.BlockSpec((None, None, None, LANES, CMP_STRIDE * dk), lambda b, g, qi: (b, 1, g, 0, 0))
    w_spec = pl.BlockSpec((CMP_BLOCK * dk, dk), lambda b, g, qi: (0, 0))
    rows = R * tq
    return pl.pallas_call(
        functools.partial(_nsa_kernel, seq=seq, tq=tq, tk=tk, heads_per_group=R),
        grid=(batch, G, nq),
        in_specs=[
            pl.BlockSpec(memory_space=pltpu.SMEM),
            pl.BlockSpec((tq, R * dk), lambda b, g, qi: (b * nq + qi, g)),
            x_spec_k, x_spec_v, kv_spec(2), kv_spec(3), kv_spec(4), kv_spec(5),
            pl.BlockSpec((None, None, tq, 3 * R), lambda b, g, qi: (b, g, qi, 0)),
            w_spec, w_spec,
            pl.BlockSpec((2, 1, CMP_BLOCK * dk), lambda b, g, qi: (0, 0, 0)),
        ],
        out_specs=pl.BlockSpec((tq, R * dk), lambda b, g, qi: (b * nq + qi, g)),
        out_shape=jax.ShapeDtypeStruct((batch * seq, NSA_HEADS * dk), BF16),
        scratch_shapes=[
            pltpu.VMEM((LANES, dk), BF16), pltpu.VMEM((LANES, dk), BF16),
            pltpu.VMEM((rows, LANES), F32), pltpu.VMEM((rows, LANES), F32), pltpu.VMEM((rows, dk), F32),
        ],
        compiler_params=_params("parallel", "parallel", "arbitrary"),
        name="nsa",
    )(slopes, z, x_cmp, x_cmp, z, z, z, z, gates, wck, wcv, pos)


def _log_sigmoid(x):
    return jnp.minimum(x, 0.0) - jnp.log1p(jnp.exp(-jnp.abs(x)))


def _mlstm_kernel(q_ref, k_ref, v_ref, og_ref, gr_ref, gc_ref, nw_ref, o_ref, c_ref, *, seq, L):
    dqk = ML_QK_DIM
    dv = ML_V_DIM
    nc = seq // L
    ext = dv + LANES
    ii = lax.broadcasted_iota(jnp.int32, (L, L), 0)
    kk = lax.broadcasted_iota(jnp.int32, (L, L), 1)
    causal = kk <= ii
    tri = _ind(causal, BF16)
    tri_t = _ind(ii <= kk, BF16)
    ones_slab = _ind(lax.broadcasted_iota(jnp.int32, (L, LANES), 1) == 0, BF16)
    c_ref[...] = jnp.zeros((dqk, ext), F32)

    def cumsum_mats(lf_row, lf_col):
        lr = jnp.broadcast_to(lf_row, (L, L))
        lc = jnp.broadcast_to(lf_col, (L, L))
        b_row = jnp.zeros((L, L), F32)
        b_col = jnp.zeros((L, L), F32)
        for part in _split3(lr)[:2]:
            b_row = b_row + jnp.dot(part, tri_t, preferred_element_type=F32)
        for part in _split3(lc)[:2]:
            b_col = b_col + jnp.dot(tri, part, preferred_element_type=F32)
        return b_row, b_col

    def body(c, m):
        r0 = pl.multiple_of(c * L, L)
        qc = q_ref[pl.ds(r0, L), :]
        kc = k_ref[pl.ds(r0, L), :] * (dqk ** -0.5)
        v_ext = jnp.concatenate([v_ref[pl.ds(r0, L), :], ones_slab], axis=1)
        i_row = gr_ref[0, pl.ds(c, 1), :]
        lf_row = _log_sigmoid(gr_ref[1, pl.ds(c, 1), :])
        i_col = gc_ref[0, pl.ds(r0, L), :]
        lf_col = _log_sigmoid(gc_ref[1, pl.ds(r0, L), :])
        b_row, b_col = cumsum_mats(lf_row, lf_col)
        log_d = jnp.where(causal, b_col - b_row + i_row, NEG_INF)
        b_c = b_col[:, 0:1]
        m_inter = b_c + m
        m_t = jnp.maximum(m_inter, jnp.max(log_d, axis=-1, keepdims=True))
        d = jnp.exp(log_d - m_t)
        s = lax.dot_general(qc, kc, _NT, preferred_element_type=F32) * d
        w_inter = jnp.exp(m_inter - m_t)
        inter = jnp.dot(qc, c_ref[...].astype(BF16), preferred_element_type=F32)
        intra = jnp.dot(s.astype(BF16), v_ext, preferred_element_type=F32)
        numden = w_inter * inter + intra
        den = numden[:, dv:dv + 1]
        h = numden[:, 0:dv] / jnp.maximum(jnp.abs(den), jnp.exp(-m_t))
        g_tot = b_row[0:1, L - 1:L]
        a_row = g_tot - b_row[0:1, :] + i_row
        a_col = g_tot - b_c + i_col
        m_new = jnp.maximum(g_tot + m, jnp.max(a_row, axis=-1, keepdims=True))
        decay = jnp.exp(g_tot + m - m_new)
        w_tok = jnp.exp(a_col - m_new)
        kw_t = (kc.astype(F32) * w_tok).T.astype(BF16)
        c_ref[...] = decay * c_ref[...] + jnp.dot(kw_t, v_ext, preferred_element_type=F32)
        hn = _rms(h, nw_ref[...]) * jax.nn.sigmoid(og_ref[pl.ds(r0, L), :].astype(F32))
        o_ref[pl.ds(r0, L), :] = hn.astype(o_ref.dtype)
        return m_new

    lax.fori_loop(0, nc, body, jnp.zeros((1, 1), F32))


def _mlstm(z, gr, gc, norm_w, *, batch, seq, q_col, k_col, v_col, o_col):
    H = ML_HEADS
    dqk = ML_QK_DIM
    dv = ML_V_DIM
    L = min(ML_CHUNK_LEN, seq)
    nc = seq // L
    return pl.pallas_call(
        functools.partial(_mlstm_kernel, seq=seq, L=L),
        grid=(batch, H),
        in_specs=[
            pl.BlockSpec((seq, dqk), lambda b, h: (b, q_col // dqk + h)),
            pl.BlockSpec((seq, dqk), lambda b, h: (b, k_col // dqk + h)),
            pl.BlockSpec((seq, dv), lambda b, h: (b, v_col // dv + h)),
            pl.BlockSpec((seq, dv), lambda b, h: (b, o_col // dv + h)),
            pl.BlockSpec((None, None, 2, nc, L), lambda b, h: (b, h, 0, 0, 0)),
            pl.BlockSpec((None, None, 2, seq, 1), lambda b, h: (b, h, 0, 0, 0)),
            pl.BlockSpec((None, 1, dv), lambda b, h: (h, 0, 0)),
        ],
        out_specs=pl.BlockSpec((seq, dv), lambda b, h: (b, h)),
        out_shape=jax.ShapeDtypeStruct((batch * seq, H * dv), BF16),
        scratch_shapes=[pltpu.VMEM((dqk, dv + LANES), F32)],
        compiler_params=_params("parallel", "parallel"),
        name="mlstm",
    )(z, z, z, z, gr, gc, norm_w.reshape(H, 1, dv))


def _peer_pairs():
    return [(a, PEER_TOPK // (a + 1)) for a in range(PEER_TOPK)]


def _topk_multiset(s, k):
    tt = s.shape[1]
    riota = lax.broadcasted_iota(jnp.int32, (k, tt), 0).astype(F32)
    v = jnp.full((k, tt), -jnp.inf, F32)
    taken = jnp.zeros((1, tt), F32)
    rem = s
    for _ in range(k):
        m = jnp.max(rem, axis=0, keepdims=True)
        eq = rem == m
        cnt = jnp.sum(_ind(eq, F32), axis=0, keepdims=True)
        v = jnp.where((riota >= taken) & (riota < taken + cnt), m, v)
        taken = taken + cnt
        rem = jnp.where(eq, -jnp.inf, rem)
    return v


def _peer_topk_kernel(q_ref, khi_ref, klo_ref, s1_ref, s2_ref, e1_ref, e2_ref, tau_ref, cand_ref):
    H = PEER_HEADS
    half = PEER_KEY_DIM // 2
    K = PEER_TOPK
    pairs = _peer_pairs()
    n_cand = sum(nb for _, nb in pairs)
    tt = q_ref.shape[0]
    for h in range(H):
        s_parts = []
        for side in range(2):
            qf = q_ref[:, (2 * h + side) * half:(2 * h + side + 1) * half]
            q_hi = qf.astype(BF16)
            q_lo = (qf - q_hi.astype(F32)).astype(BF16)
            k_hi = khi_ref[side, h]
            k_lo = klo_ref[side, h]
            s = (lax.dot_general(k_hi, q_hi, _NT, preferred_element_type=F32)
                 + lax.dot_general(k_hi, q_lo, _NT, preferred_element_type=F32)
                 + lax.dot_general(k_lo, q_hi, _NT, preferred_element_type=F32))
            s_parts.append(s)
        s1, s2 = s_parts
        v1 = _topk_multiset(s1, K)
        v2 = _topk_multiset(s2, K)
        off = 0
        for a, nb in pairs:
            cand_ref[off:off + nb, :] = v1[a:a + 1, :] + v2[0:nb, :]
            off += nb
        cand_ref[n_cand:, :] = jnp.full((cand_ref.shape[0] - n_cand, tt), -jnp.inf, F32)
        rem = cand_ref[...]
        taken = jnp.zeros((1, tt), F32)
        tau = jnp.zeros((1, tt), F32)
        zsum = jnp.zeros((1, tt), F32)
        cmax = v1[0:1, :] + v2[0:1, :]
        for _ in range(K):
            m = jnp.max(rem, axis=0, keepdims=True)
            eq = rem == m
            cnt = jnp.sum(_ind(eq, F32), axis=0, keepdims=True)
            take = jnp.minimum(cnt, jnp.maximum(K - taken, 0.0))
            zsum = zsum + take * jnp.exp(m - cmax)
            tau = jnp.where((taken < K) & (taken + cnt >= K), m, tau)
            taken = taken + cnt
            rem = jnp.where(eq, -jnp.inf, rem)
        s1_ref[h] = s1
        s2_ref[h] = s2
        e1_ref[h] = jnp.exp(s1 - v1[0:1, :])
        e2_ref[h] = jnp.exp(s2 - v2[0:1, :]) / zsum
        tau_ref[h:h + 1, :] = tau


def _peer_topk(pq, k_hi, k_lo):
    n = pq.shape[0]
    H = PEER_HEADS
    nk = PEER_N_KEYS
    tt = min(PEER_TOPK_TT, n)
    n_cand = sum(nb for _, nb in _peer_pairs())
    cand_rows = -(-n_cand // SUBLANES) * SUBLANES
    key_spec = pl.BlockSpec((2, H, nk, PEER_KEY_DIM // 2), lambda i: (0, 0, 0, 0))
    hkt = pl.BlockSpec((H, nk, tt), lambda i: (0, 0, i))
    hkn = jax.ShapeDtypeStruct((H, nk, n), F32)
    return pl.pallas_call(
        _peer_topk_kernel,
        grid=(n // tt,),
        in_specs=[pl.BlockSpec((tt, pq.shape[1]), lambda i: (i, 0)), key_spec, key_spec],
        out_specs=[hkt, hkt, hkt, hkt, pl.BlockSpec((H, tt), lambda i: (0, i))],
        out_shape=[hkn, hkn, hkn, hkn, jax.ShapeDtypeStruct((H, n), F32)],
        scratch_shapes=[pltpu.VMEM((cand_rows, tt), F32)],
        compiler_params=_params("parallel"),
        name="peer_topk",
    )(pq, k_hi, k_lo)


def _peer_dense_kernel(x_ref, u_ref, vt_ref, s1_ref, e1_ref, s2_ref, e2_ref, tau_ref, o_ref, acc_ref, *, n_i1):
    H = PEER_HEADS
    nk = PEER_N_KEYS
    e = pl.program_id(1)

    @pl.when(e == 0)
    def _():
        acc_ref[...] = jnp.zeros(acc_ref.shape, F32)

    g_t = lax.dot_general(u_ref[...], x_ref[...], _NT, preferred_element_type=F32)
    act = 0.5 * g_t * (1.0 + lax.erf(g_t * (2.0 ** -0.5)))
    parts = []
    for i in range(n_i1):
        row = (e * n_i1 + i) % SUBLANES
        w = jnp.zeros((nk, g_t.shape[1]), F32)
        for h in range(H):
            s1r = s1_ref[h, pl.ds(row, 1), :]
            e1r = e1_ref[h, pl.ds(row, 1), :]
            hit = (s1r + s2_ref[h]) >= tau_ref[h:h + 1, :]
            w = w + jnp.where(hit, e1r * e2_ref[h], 0.0)
        parts.append((act[i * nk:(i + 1) * nk] * w).astype(BF16))
    a_t = jnp.concatenate(parts, axis=0) if n_i1 > 1 else parts[0]
    acc_ref[...] += jnp.dot(vt_ref[...], a_t, preferred_element_type=F32)

    @pl.when(e == pl.num_programs(1) - 1)
    def _():
        o_ref[...] = acc_ref[...].T.astype(o_ref.dtype)


def _peer_dense(u2, pu, pvt, s1, s2, e1, e2, tau):
    n, d = u2.shape
    n_exp = pu.shape[0]
    H = PEER_HEADS
    nk = PEER_N_KEYS
    tt = min(PEER_TT, n)
    te = PEER_TE
    n_i1 = te // nk
    assert n_i1 * nk == te and SUBLANES % n_i1 == 0

    def i1_spec():
        return pl.BlockSpec((H, SUBLANES, tt), lambda i, e: (0, (e * n_i1) // SUBLANES, i))

    def i2_spec():
        return pl.BlockSpec((H, nk, tt), lambda i, e: (0, 0, i))

    return pl.pallas_call(
        functools.partial(_peer_dense_kernel, n_i1=n_i1),
        grid=(n // tt, n_exp // te),
        in_specs=[
            pl.BlockSpec((tt, d), lambda i, e: (i, 0)),
            pl.BlockSpec((te, d), lambda i, e: (e, 0)),
            pl.BlockSpec((d, te), lambda i, e: (0, e)),
            i1_spec(), i1_spec(), i2_spec(), i2_spec(),
            pl.BlockSpec((H, tt), lambda i, e: (0, i)),
        ],
        out_specs=pl.BlockSpec((tt, d), lambda i, e: (i, 0)),
        out_shape=jax.ShapeDtypeStruct((n, d), BF16),
        scratch_shapes=[pltpu.VMEM((d, tt), F32)],
        compiler_params=_params("parallel", "arbitrary"),
        name="peer_dense",
    )(u2, pu, pvt, s1, e1, s2, e2, tau)


def _layer(h, mod3, norm_pre_mix, norm_post_mix, norm_pre_ffn, norm_post_ffn, w_in, cmp_pos, w_cmp_k, w_cmp_v,
           gate_bias, ml_norm_w, w_up_nsa, w_up_mlstm, w_out, peer_w_q, sub_keys, peer_u, peer_v, *, batch, seq):
    d = h.shape[1]
    G = NSA_KV_GROUPS
    R = NSA_HEADS // G
    dk = NSA_HEAD_DIM
    H = ML_HEADS
    nsa_w = NSA_HEADS * dk
    kv_w = 6 * G * dk
    mq_w = H * ML_QK_DIM
    mv_w = H * ML_V_DIM
    c_kv = nsa_w
    c_gn = c_kv + kv_w
    c_mq = c_gn + 3 * NSA_HEADS
    c_mk = c_mq + mq_w
    c_mv = c_mk + mq_w
    c_mo = c_mv + mv_w
    c_if = c_mo + mv_w
    c_mg = c_if + 2 * H
    w_main = jnp.concatenate([w_in[:, :c_gn], w_in[:, c_mq:c_if], w_in[:, c_mg:]], axis=1).astype(BF16)
    n_gate = 3 * NSA_HEADS + 2 * H
    w_gate = jnp.concatenate([w_in[:, c_gn:c_mq], w_in[:, c_if:c_mg],
                              jnp.zeros((d, LANES - n_gate), F32)], axis=1).astype(BF16)
    z_kv = nsa_w
    z_mq = z_kv + kv_w
    z_mk = z_mq + mq_w
    z_mv = z_mk + mq_w
    z_mo = z_mv + mv_w
    z_mg = z_mo + mv_w

    u = _prenorm(h, norm_pre_mix, mod3, seq)
    z = _matmul(u, w_main, BF16)
    gz = _matmul(u, w_gate, F32)

    n_slab = seq // CMP_STRIDE
    x_cmp = z[:, z_kv:z_kv + 2 * G * dk].reshape(batch, n_slab, CMP_STRIDE, 2, G, dk)
    x_cmp = x_cmp.transpose(0, 3, 4, 1, 2, 5).reshape(batch, 2, G, n_slab, CMP_STRIDE * dk)
    nsa_gates = gz[:, :3 * NSA_HEADS].reshape(batch, seq, G, 3 * R).transpose(0, 2, 1, 3)
    slopes = jnp.exp2(-8.0 * jnp.arange(1, NSA_HEADS + 1, dtype=F32) / NSA_HEADS).reshape(G, R)
    o_nsa = _nsa(slopes, z, x_cmp, nsa_gates, w_cmp_k.astype(BF16), w_cmp_v.astype(BF16),
                 cmp_pos.reshape(2, 1, CMP_BLOCK * dk), batch=batch, seq=seq, kv_col=z_kv)

    L = min(ML_CHUNK_LEN, seq)
    ifg = gz[:, 3 * NSA_HEADS:n_gate].reshape(batch, seq, 2, H) + gate_bias
    ifg = ifg.transpose(0, 3, 2, 1)
    h_ml = _mlstm(z, ifg.reshape(batch, H, 2, seq // L, L), ifg.reshape(batch, H, 2, seq, 1), ml_norm_w,
                  batch=batch, seq=seq, q_col=z_mq, k_col=z_mk, v_col=z_mv, o_col=z_mo)

    t1 = _matmul(o_nsa, w_up_nsa.astype(BF16), BF16, gate=z, gate_col=z_mg)
    mix = _matmul(h_ml, w_up_mlstm.astype(BF16), BF16, gate=z, gate_col=z_mg + d, add=t1)
    y = _matmul(mix, w_out.astype(BF16), BF16)
    h1, u2 = _midnorm(h, y, norm_post_mix, norm_pre_ffn, mod3, seq)

    pq = _matmul(u2, peer_w_q.astype(BF16), F32)
    k_hi = sub_keys.astype(BF16)
    k_lo = (sub_keys - k_hi.astype(F32)).astype(BF16)
    s1, s2, e1, e2, tau = _peer_topk(pq, k_hi, k_lo)
    y2 = _peer_dense(u2, peer_u.astype(BF16), peer_v.T.astype(BF16), s1, s2, e1, e2, tau)
    return _finalnorm(h1, y2, norm_post_ffn, mod3, seq)


def kernel(x, c, w_ada, b_ada, norm_pre_mix, norm_post_mix, norm_pre_ffn, norm_post_ffn, w_in, nsa_cmp_pos, nsa_w_cmp_k, nsa_w_cmp_v, mlstm_gate_bias, mlstm_norm_w, w_up_nsa, w_up_mlstm, w_out, peer_w_q, peer_sub_keys, peer_u, peer_v):
    batch, seq, d = x.shape
    depth = w_ada.shape[0]
    h = x.reshape(batch * seq, d)
    c_pad = jnp.concatenate([c, jnp.zeros((SUBLANES - batch % SUBLANES, d), c.dtype)], axis=0)
    for l in range(depth):
        mod = _adaln(c_pad, w_ada[l], b_ada[l])[:batch]
        mod3 = mod.reshape(batch, 6, d)
        h = _layer(h, mod3, norm_pre_mix[l], norm_post_mix[l], norm_pre_ffn[l], norm_post_ffn[l], w_in[l],
                   nsa_cmp_pos[l], nsa_w_cmp_k[l], nsa_w_cmp_v[l], mlstm_gate_bias[l], mlstm_norm_w[l],
                   w_up_nsa[l], w_up_mlstm[l], w_out[l], peer_w_q[l], peer_sub_keys[l], peer_u[l], peer_v[l],
                   batch=batch, seq=seq)
    return h.reshape(batch, seq, d).astype(x.dtype)
```

```python
import functools
import math

import jax
import jax.numpy as jnp
from jax import lax
from jax.experimental import pallas as pl
from jax.experimental.pallas import tpu as pltpu

D_MODEL = 4096
BATCH = 4
SEQ = 2048
NSA_HEADS = 16
NSA_KV_GROUPS = 4
NSA_HEAD_DIM = 128
CMP_BLOCK = 32
CMP_STRIDE = 16
SLC_BLOCK = 64
SLC_TOPN = 16
WINDOW = 512
ML_HEADS = 8
ML_QK_DIM = 256
ML_V_DIM = 512
PEER_HEADS = 8
PEER_KEY_DIM = 256
PEER_N_KEYS = 128
PEER_TOPK = 16

NEG_INF = -1e30
BIG = 1e9
TINY = 1e-30
EPS = 1e-6

F32 = jnp.float32
BF16 = jnp.bfloat16

V7X_VMEM_LIMIT_BYTES = 56 * 1024 * 1024
LANES = 128
SUBLANES = 8

ROW_TILE = 256
MM_BM = 1024
MM_BN = 1024
ADA_BN = 512
NSA_TQ = 128
NSA_TK = 256
ML_CHUNK_LEN = 128
PEER_TOPK_TT = 256
PEER_TT = 512
PEER_TE = 512

_NT = (((1,), (1,)), ((), ()))


def _params(*sem):
    return pltpu.CompilerParams(dimension_semantics=sem, vmem_limit_bytes=V7X_VMEM_LIMIT_BYTES)


def _ind(cond, dtype):
    wide = jnp.int32 if jnp.issubdtype(dtype, jnp.integer) else F32
    return jnp.where(cond, jnp.ones((), wide), jnp.zeros((), wide)).astype(dtype)


def _log2(n):
    k = int(math.log2(n))
    assert 1 << k == n
    return k


def _split3(x):
    hi = x.astype(BF16)
    r1 = x - hi.astype(F32)
    mid = r1.astype(BF16)
    lo = (r1 - mid.astype(F32)).astype(BF16)
    return hi, mid, lo


def _adaln_kernel(c_ref, w_ref, b_ref, o_ref):
    c = c_ref[...]
    cond = (c * jax.nn.sigmoid(c)).astype(BF16)
    o_ref[...] = jnp.dot(cond, w_ref[...].astype(BF16), preferred_element_type=F32) + b_ref[...]


def _adaln(c_pad, w_ada, b_ada):
    rows, d = c_pad.shape
    n = w_ada.shape[1]
    bn = min(ADA_BN, n)
    return pl.pallas_call(
        _adaln_kernel,
        grid=(n // bn,),
        in_specs=[
            pl.BlockSpec((rows, d), lambda j: (0, 0)),
            pl.BlockSpec((d, bn), lambda j: (0, j)),
            pl.BlockSpec((1, bn), lambda j: (0, j)),
        ],
        out_specs=pl.BlockSpec((rows, bn), lambda j: (0, j)),
        out_shape=jax.ShapeDtypeStruct((rows, n), F32),
        compiler_params=_params("parallel"),
        name="adaln",
    )(c_pad, w_ada, b_ada.reshape(1, n))


def _rms(x, w):
    return x * lax.rsqrt(jnp.mean(x * x, axis=-1, keepdims=True) + EPS) * w


def _prenorm_kernel(x_ref, w_ref, mod_ref, u_ref):
    y = _rms(x_ref[...], w_ref[...])
    u_ref[...] = (y * (1.0 + mod_ref[1:2, :]) + mod_ref[0:1, :]).astype(u_ref.dtype)


def _prenorm(x2, w, mod3, seq):
    n, d = x2.shape
    tr = min(ROW_TILE, seq)
    return pl.pallas_call(
        _prenorm_kernel,
        grid=(n // tr,),
        in_specs=[
            pl.BlockSpec((tr, d), lambda i: (i, 0)),
            pl.BlockSpec((1, d), lambda i: (0, 0)),
            pl.BlockSpec((None, 6, d), lambda i: ((i * tr) // seq, 0, 0)),
        ],
        out_specs=pl.BlockSpec((tr, d), lambda i: (i, 0)),
        out_shape=jax.ShapeDtypeStruct((n, d), BF16),
        compiler_params=_params("parallel"),
        name="prenorm",
    )(x2, w.reshape(1, d), mod3)


def _midnorm_kernel(x_ref, y_ref, w1_ref, w2_ref, mod_ref, h_ref, u_ref, ut_ref):
    h = x_ref[...] + mod_ref[2:3, :] * _rms(y_ref[...].astype(F32), w1_ref[...])
    h_ref[...] = h
    u = _rms(h, w2_ref[...]) * (1.0 + mod_ref[4:5, :]) + mod_ref[3:4, :]
    u_ref[...] = u.astype(u_ref.dtype)
    ut_ref[...] = u.T.astype(ut_ref.dtype)


def _midnorm(x2, y, w_post, w_pre, mod3, seq):
    n, d = x2.shape
    tr = min(ROW_TILE, seq)
    row = pl.BlockSpec((tr, d), lambda i: (i, 0))
    vec = pl.BlockSpec((1, d), lambda i: (0, 0))
    return pl.pallas_call(
        _midnorm_kernel,
        grid=(n // tr,),
        in_specs=[row, row, vec, vec, pl.BlockSpec((None, 6, d), lambda i: ((i * tr) // seq, 0, 0))],
        out_specs=[row, row, pl.BlockSpec((d, tr), lambda i: (0, i))],
        out_shape=[jax.ShapeDtypeStruct((n, d), F32), jax.ShapeDtypeStruct((n, d), BF16),
                   jax.ShapeDtypeStruct((d, n), BF16)],
        compiler_params=_params("parallel"),
        name="midnorm",
    )(x2, y, w_post.reshape(1, d), w_pre.reshape(1, d), mod3)


def _finalnorm_kernel(h_ref, y_ref, w_ref, mod_ref, o_ref):
    o_ref[...] = h_ref[...] + mod_ref[5:6, :] * _rms(y_ref[...].astype(F32), w_ref[...])


def _finalnorm(h1, y, w, mod3, seq):
    n, d = h1.shape
    tr = min(ROW_TILE, seq)
    row = pl.BlockSpec((tr, d), lambda i: (i, 0))
    return pl.pallas_call(
        _finalnorm_kernel,
        grid=(n // tr,),
        in_specs=[row, row, pl.BlockSpec((1, d), lambda i: (0, 0)),
                  pl.BlockSpec((None, 6, d), lambda i: ((i * tr) // seq, 0, 0))],
        out_specs=row,
        out_shape=jax.ShapeDtypeStruct((n, d), F32),
        compiler_params=_params("parallel"),
        name="finalnorm",
    )(h1, y, w.reshape(1, d), mod3)


def _gelu(x):
    return 0.5 * x * (1.0 + lax.erf(x * (2.0 ** -0.5)))


def _mm_kernel(*refs, has_gate, has_add, gelu):
    a_ref, b_ref = refs[0], refs[1]
    o_ref = refs[-1]
    acc = jnp.dot(a_ref[...], b_ref[...], preferred_element_type=F32)
    nxt = 2
    if has_gate:
        acc = acc * jax.nn.sigmoid(refs[nxt][...].astype(F32))
        nxt += 1
    if has_add:
        acc = acc + refs[nxt][...].astype(F32)
    if gelu:
        acc = _gelu(acc)
    o_ref[...] = acc.astype(o_ref.dtype)


def _matmul(a, b, out_dtype, gate=None, gate_col=0, add=None, gelu=False):
    m, k = a.shape
    n = b.shape[1]
    bm = min(MM_BM, m)
    bn = min(MM_BN, n)
    in_specs = [pl.BlockSpec((bm, k), lambda i, j: (i, 0)), pl.BlockSpec((k, bn), lambda i, j: (0, j))]
    args = [a, b]
    if gate is not None:
        goff = gate_col // bn
        assert goff * bn == gate_col
        in_specs.append(pl.BlockSpec((bm, bn), lambda i, j: (i, goff + j)))
        args.append(gate)
    if add is not None:
        in_specs.append(pl.BlockSpec((bm, bn), lambda i, j: (i, j)))
        args.append(add)
    return pl.pallas_call(
        functools.partial(_mm_kernel, has_gate=gate is not None, has_add=add is not None, gelu=gelu),
        grid=(m // bm, n // bn),
        in_specs=in_specs,
        out_specs=pl.BlockSpec((bm, bn), lambda i, j: (i, j)),
        out_shape=jax.ShapeDtypeStruct((m, n), out_dtype),
        compiler_params=_params("parallel", "arbitrary"),
        name="matmul",
    )(*args)


def _nsa_kernel(slopes_ref, q_ref, xk_ref, xv_ref, ks_ref, vs_ref, kw_ref, vw_ref, gate_ref, wck_ref, wcv_ref,
                pos_ref, o_ref, kc_ref, vc_ref, m_ref, l_ref, acc_ref, *, seq, tq, tk, heads_per_group):
    R = heads_per_group
    dk = NSA_HEAD_DIM
    g = pl.program_id(1)
    qi = pl.program_id(2)
    t0 = qi * tq
    n_cmp = (seq - CMP_BLOCK) // CMP_STRIDE + 1
    n_slc = seq // SLC_BLOCK
    n_sel = min(SLC_TOPN, n_slc)
    half = CMP_STRIDE * dk
    scale = dk ** -0.5
    rows = R * tq

    @pl.when(qi == 0)
    def _():
        for x_ref, w_ref, dst, pi in ((xk_ref, wck_ref, kc_ref, 0), (xv_ref, wcv_ref, vc_ref, 1)):
            x = x_ref[...]
            ya = jnp.dot(x, w_ref[0:half, :], preferred_element_type=F32)
            yb = jnp.dot(x, w_ref[half:2 * half, :], preferred_element_type=F32)
            pos = jnp.broadcast_to(pos_ref[pi], (SUBLANES, 2 * half)).astype(BF16)
            pb = jnp.dot(pos, w_ref[...], preferred_element_type=F32)[0:1, :]
            dst[...] = (ya + pltpu.roll(yb, LANES - 1, 0) + pb).astype(BF16)

    q = q_ref[...]
    qs = jnp.concatenate([q[:, r * dk:(r + 1) * dk] for r in range(R)], axis=0)

    row_id = lax.broadcasted_iota(jnp.int32, (rows, 1), 0)
    t_col = t0 + (row_id & (tq - 1))
    head_of_row = row_id >> _log2(tq)
    slope_col = jnp.zeros((rows, 1), F32)
    for r in range(R):
        slope_col = jnp.where(head_of_row == r, slopes_ref[g, r], slope_col)

    def masked_softmax(s, mask):
        s = jnp.where(mask, s, NEG_INF)
        m = jnp.max(s, axis=-1, keepdims=True)
        p = jnp.where(mask, jnp.exp(s - m), 0.0)
        return p / jnp.maximum(jnp.sum(p, axis=-1, keepdims=True), TINY)

    n_idx = lax.broadcasted_iota(jnp.int32, (rows, LANES), 1)
    dist_i = t_col - (n_idx * CMP_STRIDE + CMP_BLOCK - 1)
    cvalid = (dist_i >= 0) & (n_idx < n_cmp)
    sc = lax.dot_general(qs, kc_ref[...], _NT, preferred_element_type=F32) * scale
    p_cmp = masked_softmax(sc - slope_col * dist_i.astype(F32), cvalid)
    o_cmp = jnp.dot(p_cmp.astype(BF16), vc_ref[...], preferred_element_type=F32)

    psum = p_cmp[0:tq]
    for r in range(1, R):
        psum = psum + p_cmp[r * tq:(r + 1) * tq]
    jj = lax.broadcasted_iota(jnp.int32, (LANES, LANES), 0)
    nn = lax.broadcasted_iota(jnp.int32, (LANES, LANES), 1)
    ovl = _ind((nn * CMP_STRIDE < jj * SLC_BLOCK + SLC_BLOCK) & (nn * CMP_STRIDE + CMP_BLOCK > jj * SLC_BLOCK)
               & (jj < n_slc) & (nn < n_cmp), BF16)
    imp_t = jnp.zeros((LANES, tq), F32)
    for part in _split3(psum):
        imp_t = imp_t + lax.dot_general(ovl, part, _NT, preferred_element_type=F32)
    imp_t = imp_t[0:n_slc]
    j_i = lax.broadcasted_iota(jnp.int32, (n_slc, tq), 0)
    t_i = t0 + lax.broadcasted_iota(jnp.int32, (n_slc, tq), 1)
    cur = t_i >> _log2(SLC_BLOCK)
    causal = j_i * SLC_BLOCK <= t_i
    forced = (j_i == 0) | (j_i == cur) | (j_i == cur - 1)
    score = jnp.where(causal, jnp.where(forced, BIG, imp_t), -BIG)
    rank = jnp.zeros((n_slc, tq), jnp.int32)
    for jp in range(n_slc):
        row = score[jp:jp + 1, :]
        beats = (row > score) | ((row == score) & (j_i > jp))
        rank = rank + _ind(beats, jnp.int32)
    sel_t = _ind(rank < n_sel, F32)
    sel_t = jnp.concatenate([sel_t, jnp.zeros((LANES - n_slc, tq), F32)], axis=0)
    sel = sel_t.T.astype(BF16)

    m_ref[...] = jnp.full((rows, LANES), NEG_INF, F32)
    l_ref[...] = jnp.zeros((rows, LANES), F32)
    acc_ref[...] = jnp.zeros((rows, dk), F32)
    n_kt = (t0 + tq + tk - 1) // tk

    def slc_body(kt, carry):
        k0 = pl.multiple_of(kt * tk, tk)
        kb = ks_ref[pl.ds(k0, tk), :]
        vb = vs_ref[pl.ds(k0, tk), :]
        s = lax.dot_general(qs, kb, _NT, preferred_element_type=F32) * scale
        ej = lax.broadcasted_iota(jnp.int32, (LANES, tk), 0)
        ec = lax.broadcasted_iota(jnp.int32, (LANES, tk), 1)
        expand = _ind(ej == ((k0 + ec) >> _log2(SLC_BLOCK)), BF16)
        selm = jnp.dot(sel, expand, preferred_element_type=F32)
        selm = jnp.concatenate([selm] * R, axis=0)
        pos = k0 + lax.broadcasted_iota(jnp.int32, (rows, tk), 1)
        dd_i = t_col - pos
        mask = (selm > 0.5) & (dd_i >= 0)
        s = jnp.where(mask, s - slope_col * dd_i.astype(F32), NEG_INF)
        m_old = m_ref[...]
        m_new = jnp.maximum(m_old, jnp.max(s, axis=-1, keepdims=True))
        p = jnp.where(mask, jnp.exp(s - m_new[:, 0:1]), 0.0)
        alpha = jnp.exp(m_old - m_new)
        l_ref[...] = alpha * l_ref[...] + jnp.sum(p, axis=-1, keepdims=True)
        acc_ref[...] = alpha * acc_ref[...] + jnp.dot(p.astype(BF16), vb, preferred_element_type=F32)
        m_ref[...] = m_new
        return carry

    lax.fori_loop(0, n_kt, slc_body, 0)
    o_slc = acc_ref[...] / jnp.maximum(l_ref[...], TINY)

    span = WINDOW + tq
    ws = pl.multiple_of(jnp.maximum(t0 - WINDOW, 0), tq)
    kwb = kw_ref[pl.ds(ws, span), :]
    vwb = vw_ref[pl.ds(ws, span), :]
    sw = lax.dot_general(qs, kwb, _NT, preferred_element_type=F32) * scale
    dw_i = t_col - (ws + lax.broadcasted_iota(jnp.int32, (rows, span), 1))
    wmask = (dw_i >= 0) & (dw_i < WINDOW)
    p_win = masked_softmax(sw - slope_col * dw_i.astype(F32), wmask)
    o_win = jnp.dot(p_win.astype(BF16), vwb, preferred_element_type=F32)

    gts = jax.nn.sigmoid(gate_ref[...])
    for r in range(R):
        sl = slice(r * tq, (r + 1) * tq)
        o = (gts[:, 3 * r:3 * r + 1] * o_cmp[sl] + gts[:, 3 * r + 1:3 * r + 2] * o_slc[sl]
             + gts[:, 3 * r + 2:3 * r + 3] * o_win[sl])
        o_ref[:, r * dk:(r + 1) * dk] = o.astype(o_ref.dtype)


def _nsa(slopes, z, x_cmp, gates, wck, wcv, pos, *, batch, seq, kv_col):
    G = NSA_KV_GROUPS
    R = NSA_HEADS // G
    dk = NSA_HEAD_DIM
    tq = NSA_TQ
    tk = NSA_TK
    nq = seq // tq
    assert seq // CMP_STRIDE == LANES and CMP_BLOCK == 2 * CMP_STRIDE and seq >= WINDOW + tq and seq % tk == 0
    kv0 = kv_col // dk

    def kv_spec(i):
        return pl.BlockSpec((seq, dk), lambda b, g, qi: (b, kv0 + i * G + g))

    x_spec_k = pl.BlockSpec((None, None, None, LANES, CMP_STRIDE * dk), lambda b, g, qi: (b, 0, g, 0, 0))
    x_spec_v = pl.BlockSpec((None, None, None, LANES, CMP_STRIDE * dk), lambda b, g, qi: (b, 1, g, 0, 0))
    w_spec = pl.BlockSpec((CMP_BLOCK * dk, dk), lambda b, g, qi: (0, 0))
    rows = R * tq
    return pl.pallas_call(
        functools.partial(_nsa_kernel, seq=seq, tq=tq, tk=tk, heads_per_group=R),
        grid=(batch, G, nq),
        in_specs=[
            pl.BlockSpec(memory_space=pltpu.SMEM),
            pl.BlockSpec((tq, R * dk), lambda b, g, qi: (b * nq + qi, g)),
            x_spec_k, x_spec_v, kv_spec(2), kv_spec(3), kv_spec(4), kv_spec(5),
            pl.BlockSpec((None, None, tq, 3 * R), lambda b, g, qi: (b, g, qi, 0)),
            w_spec, w_spec,
            pl.BlockSpec((2, 1, CMP_BLOCK * dk), lambda b, g, qi: (0, 0, 0)),
        ],
        out_specs=pl.BlockSpec((tq, R * dk), lambda b, g, qi: (b * nq + qi, g)),
        out_shape=jax.ShapeDtypeStruct((batch * seq, NSA_HEADS * dk), BF16),
        scratch_shapes=[
            pltpu.VMEM((LANES, dk), BF16), pltpu.VMEM((LANES, dk), BF16),
            pltpu.VMEM((rows, LANES), F32), pltpu.VMEM((rows, LANES), F32), pltpu.VMEM((rows, dk), F32),
        ],
        compiler_params=_params("parallel", "parallel", "arbitrary"),
        name="nsa",
    )(slopes, z, x_cmp, x_cmp, z, z, z, z, gates, wck, wcv, pos)


def _log_sigmoid(x):
    return jnp.minimum(x, 0.0) - jnp.log1p(jnp.exp(-jnp.abs(x)))


def _mlstm_kernel(q_ref, k_ref, v_ref, og_ref, gr_ref, gc_ref, nw_ref, o_ref, c_ref, *, seq, L):
    dqk = ML_QK_DIM
    dv = ML_V_DIM
    nc = seq // L
    ext = dv + LANES
    ii = lax.broadcasted_iota(jnp.int32, (L, L), 0)
    kk = lax.broadcasted_iota(jnp.int32, (L, L), 1)
    causal = kk <= ii
    tri = _ind(causal, BF16)
    tri_t = _ind(ii <= kk, BF16)
    ones_slab = _ind(lax.broadcasted_iota(jnp.int32, (L, LANES), 1) == 0, BF16)
    c_ref[...] = jnp.zeros((dqk, ext), F32)

    def cumsum_mats(lf_row, lf_col):
        lr = jnp.broadcast_to(lf_row, (L, L))
        lc = jnp.broadcast_to(lf_col, (L, L))
        b_row = jnp.zeros((L, L), F32)
        b_col = jnp.zeros((L, L), F32)
        for part in _split3(lr)[:2]:
            b_row = b_row + jnp.dot(part, tri_t, preferred_element_type=F32)
        for part in _split3(lc)[:2]:
            b_col = b_col + jnp.dot(tri, part, preferred_element_type=F32)
        return b_row, b_col

    def body(c, m):
        r0 = pl.multiple_of(c * L, L)
        qc = q_ref[pl.ds(r0, L), :]
        kc = k_ref[pl.ds(r0, L), :] * (dqk ** -0.5)
        v_ext = jnp.concatenate([v_ref[pl.ds(r0, L), :], ones_slab], axis=1)
        i_row = gr_ref[0, pl.ds(c, 1), :]
        lf_row = _log_sigmoid(gr_ref[1, pl.ds(c, 1), :])
        i_col = gc_ref[0, pl.ds(r0, L), :]
        lf_col = _log_sigmoid(gc_ref[1, pl.ds(r0, L), :])
        b_row, b_col = cumsum_mats(lf_row, lf_col)
        log_d = jnp.where(causal, b_col - b_row + i_row, NEG_INF)
        b_c = b_col[:, 0:1]
        m_inter = b_c + m
        m_t = jnp.maximum(m_inter, jnp.max(log_d, axis=-1, keepdims=True))
        d = jnp.exp(log_d - m_t)
        s = lax.dot_general(qc, kc, _NT, preferred_element_type=F32) * d
        w_inter = jnp.exp(m_inter - m_t)
        inter = jnp.dot(qc, c_ref[...].astype(BF16), preferred_element_type=F32)
        intra = jnp.dot(s.astype(BF16), v_ext, preferred_element_type=F32)
        numden = w_inter * inter + intra
        den = numden[:, dv:dv + 1]
        h = numden[:, 0:dv] / jnp.maximum(jnp.abs(den), jnp.exp(-m_t))
        g_tot = b_row[0:1, L - 1:L]
        a_row = g_tot - b_row[0:1, :] + i_row
        a_col = g_tot - b_c + i_col
        m_new = jnp.maximum(g_tot + m, jnp.max(a_row, axis=-1, keepdims=True))
        decay = jnp.exp(g_tot + m - m_new)
        w_tok = jnp.exp(a_col - m_new)
        kw_t = (kc.astype(F32) * w_tok).T.astype(BF16)
        c_ref[...] = decay * c_ref[...] + jnp.dot(kw_t, v_ext, preferred_element_type=F32)
        hn = _rms(h, nw_ref[...]) * jax.nn.sigmoid(og_ref[pl.ds(r0, L), :].astype(F32))
        o_ref[pl.ds(r0, L), :] = hn.astype(o_ref.dtype)
        return m_new

    lax.fori_loop(0, nc, body, jnp.zeros((1, 1), F32))


def _mlstm(z, gr, gc, norm_w, *, batch, seq, q_col, k_col, v_col, o_col):
    H = ML_HEADS
    dqk = ML_QK_DIM
    dv = ML_V_DIM
    L = min(ML_CHUNK_LEN, seq)
    nc = seq // L
    return pl.pallas_call(
        functools.partial(_mlstm_kernel, seq=seq, L=L),
        grid=(batch, H),
        in_specs=[
            pl.BlockSpec((seq, dqk), lambda b, h: (b, q_col // dqk + h)),
            pl.BlockSpec((seq, dqk), lambda b, h: (b, k_col // dqk + h)),
            pl.BlockSpec((seq, dv), lambda b, h: (b, v_col // dv + h)),
            pl.BlockSpec((seq, dv), lambda b, h: (b, o_col // dv + h)),
            pl.BlockSpec((None, None, 2, nc, L), lambda b, h: (b, h, 0, 0, 0)),
            pl.BlockSpec((None, None, 2, seq, 1), lambda b, h: (b, h, 0, 0, 0)),
            pl.BlockSpec((None, 1, dv), lambda b, h: (h, 0, 0)),
        ],
        out_specs=pl.BlockSpec((seq, dv), lambda b, h: (b, h)),
        out_shape=jax.ShapeDtypeStruct((batch * seq, H * dv), BF16),
        scratch_shapes=[pltpu.VMEM((dqk, dv + LANES), F32)],
        compiler_params=_params("parallel", "parallel"),
        name="mlstm",
    )(z, z, z, z, gr, gc, norm_w.reshape(H, 1, dv))


def _peer_pairs():
    return [(a, PEER_TOPK // (a + 1)) for a in range(PEER_TOPK)]


def _topk_multiset(s, k):
    tt = s.shape[1]
    riota = lax.broadcasted_iota(jnp.int32, (k, tt), 0).astype(F32)
    v = jnp.full((k, tt), -jnp.inf, F32)
    taken = jnp.zeros((1, tt), F32)
    rem = s
    for _ in range(k):
        m = jnp.max(rem, axis=0, keepdims=True)
        eq = rem == m
        cnt = jnp.sum(_ind(eq, F32), axis=0, keepdims=True)
        v = jnp.where((riota >= taken) & (riota < taken + cnt), m, v)
        taken = taken + cnt
        rem = jnp.where(eq, -jnp.inf, rem)
    return v


def _peer_topk_kernel(q_ref, khi_ref, klo_ref, s1_ref, s2_ref, e1_ref, e2_ref, tau_ref, cand_ref):
    H = PEER_HEADS
    half = PEER_KEY_DIM // 2
    K = PEER_TOPK
    pairs = _peer_pairs()
    n_cand = sum(nb for _, nb in pairs)
    tt = q_ref.shape[0]
    for h in range(H):
        s_parts = []
        for side in range(2):
            qf = q_ref[:, (2 * h + side) * half:(2 * h + side + 1) * half]
            q_hi = qf.astype(BF16)
            q_lo = (qf - q_hi.astype(F32)).astype(BF16)
            k_hi = khi_ref[side, h]
            k_lo = klo_ref[side, h]
            s = (lax.dot_general(k_hi, q_hi, _NT, preferred_element_type=F32)
                 + lax.dot_general(k_hi, q_lo, _NT, preferred_element_type=F32)
                 + lax.dot_general(k_lo, q_hi, _NT, preferred_element_type=F32))
            s_parts.append(s)
        s1, s2 = s_parts
        v1 = _topk_multiset(s1, K)
        v2 = _topk_multiset(s2, K)
        off = 0
        for a, nb in pairs:
            cand_ref[off:off + nb, :] = v1[a:a + 1, :] + v2[0:nb, :]
            off += nb
        cand_ref[n_cand:, :] = jnp.full((cand_ref.shape[0] - n_cand, tt), -jnp.inf, F32)
        rem = cand_ref[...]
        taken = jnp.zeros((1, tt), F32)
        tau = jnp.zeros((1, tt), F32)
        zsum = jnp.zeros((1, tt), F32)
        cmax = v1[0:1, :] + v2[0:1, :]
        for _ in range(K):
            m = jnp.max(rem, axis=0, keepdims=True)
            eq = rem == m
            cnt = jnp.sum(_ind(eq, F32), axis=0, keepdims=True)
            take = jnp.minimum(cnt, jnp.maximum(K - taken, 0.0))
            zsum = zsum + take * jnp.exp(m - cmax)
            tau = jnp.where((taken < K) & (taken + cnt >= K), m, tau)
            taken = taken + cnt
            rem = jnp.where(eq, -jnp.inf, rem)
        s1_ref[h] = s1
        s2_ref[h] = s2
        e1_ref[h] = jnp.exp(s1 - v1[0:1, :])
        e2_ref[h] = jnp.exp(s2 - v2[0:1, :]) / zsum
        tau_ref[h:h + 1, :] = tau


def _peer_topk(pq, k_hi, k_lo):
    n = pq.shape[0]
    H = PEER_HEADS
    nk = PEER_N_KEYS
    tt = min(PEER_TOPK_TT, n)
    n_cand = sum(nb for _, nb in _peer_pairs())
    cand_rows = -(-n_cand // SUBLANES) * SUBLANES
    key_spec = pl.BlockSpec((2, H, nk, PEER_KEY_DIM // 2), lambda i: (0, 0, 0, 0))
    hkt = pl.BlockSpec((H, nk, tt), lambda i: (0, 0, i))
    hkn = jax.ShapeDtypeStruct((H, nk, n), F32)
    return pl.pallas_call(
        _peer_topk_kernel,
        grid=(n // tt,),
        in_specs=[pl.BlockSpec((tt, pq.shape[1]), lambda i: (i, 0)), key_spec, key_spec],
        out_specs=[hkt, hkt, hkt, hkt, pl.BlockSpec((H, tt), lambda i: (0, i))],
        out_shape=[hkn, hkn, hkn, hkn, jax.ShapeDtypeStruct((H, n), F32)],
        scratch_shapes=[pltpu.VMEM((cand_rows, tt), F32)],
        compiler_params=_params("parallel"),
        name="peer_topk",
    )(pq, k_hi, k_lo)


def _peer_mix_kernel(act_ref, vt_ref, s1_ref, e1_ref, s2_ref, e2_ref, tau_ref, o_ref, acc_ref, a_ref, *, n_i1):
    H = PEER_HEADS
    nk = PEER_N_KEYS
    tt = act_ref.shape[1]
    e = pl.program_id(1)
    pack = 2 * SUBLANES
    dot_chunk = 2 * nk

    @pl.when(e == 0)
    def _():
        acc_ref[...] = jnp.zeros(acc_ref.shape, F32)

    dots = []
    for i in range(n_i1):
        row = (e * n_i1 + i) % SUBLANES
        s1_rows = [s1_ref[h, pl.ds(row, 1), :] for h in range(H)]
        e1_rows = [e1_ref[h, pl.ds(row, 1), :] for h in range(H)]
        for lt in range(tt // LANES):
            ls = slice(lt * LANES, (lt + 1) * LANES)
            s1b = [jnp.broadcast_to(s1_rows[h][:, ls], (SUBLANES, LANES)) for h in range(H)]
            e1b = [jnp.broadcast_to(e1_rows[h][:, ls], (SUBLANES, LANES)) for h in range(H)]
            taub = [jnp.broadcast_to(tau_ref[h:h + 1, ls], (SUBLANES, LANES)) for h in range(H)]
            for sg in range(nk // pack):
                halves = []
                for half in range(2):
                    r0 = sg * pack + half * SUBLANES
                    w = jnp.zeros((SUBLANES, LANES), F32)
                    for h in range(H):
                        hit = (s1b[h] + s2_ref[h, r0:r0 + SUBLANES, ls]) >= taub[h]
                        w = w + jnp.where(hit, e1b[h] * e2_ref[h, r0:r0 + SUBLANES, ls], 0.0)
                    halves.append(w)
                rr = slice(i * nk + sg * pack, i * nk + (sg + 1) * pack)
                a_ref[rr, ls] = (act_ref[rr, ls].astype(F32) * jnp.concatenate(halves, axis=0)).astype(BF16)
        if (i + 1) * nk % dot_chunk == 0:
            ks = slice((i + 1) * nk - dot_chunk, (i + 1) * nk)
            dots.append(jnp.dot(vt_ref[:, ks], a_ref[ks, :], preferred_element_type=F32))
    acc_ref[...] += functools.reduce(lambda x, y: x + y, dots)

    @pl.when(e == pl.num_programs(1) - 1)
    def _():
        o_ref[...] = acc_ref[...].T.astype(o_ref.dtype)


def _peer_mix(act_t, pvt, s1, s2, e1, e2, tau):
    n_exp, n = act_t.shape
    d = pvt.shape[0]
    H = PEER_HEADS
    nk = PEER_N_KEYS
    tt = min(PEER_TT, n)
    te = PEER_TE
    n_i1 = te // nk
    assert n_i1 * nk == te and SUBLANES % n_i1 == 0 and n_i1 % 2 == 0

    def i1_spec():
        return pl.BlockSpec((H, SUBLANES, tt), lambda i, e: (0, (e * n_i1) // SUBLANES, i))

    def i2_spec():
        return pl.BlockSpec((H, nk, tt), lambda i, e: (0, 0, i))

    return pl.pallas_call(
        functools.partial(_peer_mix_kernel, n_i1=n_i1),
        grid=(n // tt, n_exp // te),
        in_specs=[
            pl.BlockSpec((te, tt), lambda i, e: (e, i)),
            pl.BlockSpec((d, te), lambda i, e: (0, e)),
            i1_spec(), i1_spec(), i2_spec(), i2_spec(),
            pl.BlockSpec((H, tt), lambda i, e: (0, i)),
        ],
        out_specs=pl.BlockSpec((tt, d), lambda i, e: (i, 0)),
        out_shape=jax.ShapeDtypeStruct((n, d), BF16),
        scratch_shapes=[pltpu.VMEM((d, tt), F32), pltpu.VMEM((te, tt), BF16)],
        compiler_params=_params("parallel", "arbitrary"),
        name="peer_mix",
    )(act_t, pvt, s1, e1, s2, e2, tau)


def _layer(h, mod3, norm_pre_mix, norm_post_mix, norm_pre_ffn, norm_post_ffn, w_in, cmp_pos, w_cmp_k, w_cmp_v,
           gate_bias, ml_norm_w, w_up_nsa, w_up_mlstm, w_out, peer_w_q, sub_keys, peer_u, peer_v, *, batch, seq):
    d = h.shape[1]
    G = NSA_KV_GROUPS
    R = NSA_HEADS // G
    dk = NSA_HEAD_DIM
    H = ML_HEADS
    nsa_w = NSA_HEADS * dk
    kv_w = 6 * G * dk
    mq_w = H * ML_QK_DIM
    mv_w = H * ML_V_DIM
    c_kv = nsa_w
    c_gn = c_kv + kv_w
    c_mq = c_gn + 3 * NSA_HEADS
    c_mk = c_mq + mq_w
    c_mv = c_mk + mq_w
    c_mo = c_mv + mv_w
    c_if = c_mo + mv_w
    c_mg = c_if + 2 * H
    w_main = jnp.concatenate([w_in[:, :c_gn], w_in[:, c_mq:c_if], w_in[:, c_mg:]], axis=1).astype(BF16)
    n_gate = 3 * NSA_HEADS + 2 * H
    w_gate = jnp.concatenate([w_in[:, c_gn:c_mq], w_in[:, c_if:c_mg],
                              jnp.zeros((d, LANES - n_gate), F32)], axis=1).astype(BF16)
    z_kv = nsa_w
    z_mq = z_kv + kv_w
    z_mk = z_mq + mq_w
    z_mv = z_mk + mq_w
    z_mo = z_mv + mv_w
    z_mg = z_mo + mv_w

    u = _prenorm(h, norm_pre_mix, mod3, seq)
    z = _matmul(u, w_main, BF16)
    gz = _matmul(u, w_gate, F32)

    n_slab = seq // CMP_STRIDE
    x_cmp = z[:, z_kv:z_kv + 2 * G * dk].reshape(batch, n_slab, CMP_STRIDE, 2, G, dk)
    x_cmp = x_cmp.transpose(0, 3, 4, 1, 2, 5).reshape(batch, 2, G, n_slab, CMP_STRIDE * dk)
    nsa_gates = gz[:, :3 * NSA_HEADS].reshape(batch, seq, G, 3 * R).transpose(0, 2, 1, 3)
    slopes = jnp.exp2(-8.0 * jnp.arange(1, NSA_HEADS + 1, dtype=F32) / NSA_HEADS).reshape(G, R)
    o_nsa = _nsa(slopes, z, x_cmp, nsa_gates, w_cmp_k.astype(BF16), w_cmp_v.astype(BF16),
                 cmp_pos.reshape(2, 1, CMP_BLOCK * dk), batch=batch, seq=seq, kv_col=z_kv)

    L = min(ML_CHUNK_LEN, seq)
    ifg = gz[:, 3 * NSA_HEADS:n_gate].reshape(batch, seq, 2, H) + gate_bias
    ifg = ifg.transpose(0, 3, 2, 1)
    h_ml = _mlstm(z, ifg.reshape(batch, H, 2, seq // L, L), ifg.reshape(batch, H, 2, seq, 1), ml_norm_w,
                  batch=batch, seq=seq, q_col=z_mq, k_col=z_mk, v_col=z_mv, o_col=z_mo)

    t1 = _matmul(o_nsa, w_up_nsa.astype(BF16), BF16, gate=z, gate_col=z_mg)
    mix = _matmul(h_ml, w_up_mlstm.astype(BF16), BF16, gate=z, gate_col=z_mg + d, add=t1)
    y = _matmul(mix, w_out.astype(BF16), BF16)
    h1, u2, u2_t = _midnorm(h, y, norm_post_mix, norm_pre_ffn, mod3, seq)

    pq = _matmul(u2, peer_w_q.astype(BF16), F32)
    k_hi = sub_keys.astype(BF16)
    k_lo = (sub_keys - k_hi.astype(F32)).astype(BF16)
    s1, s2, e1, e2, tau = _peer_topk(pq, k_hi, k_lo)
    act_t = _matmul(peer_u.astype(BF16), u2_t, BF16, gelu=True)
    y2 = _peer_mix(act_t, peer_v.T.astype(BF16), s1, s2, e1, e2, tau)
    return _finalnorm(h1, y2, norm_post_ffn, mod3, seq)


def kernel(x, c, w_ada, b_ada, norm_pre_mix, norm_post_mix, norm_pre_ffn, norm_post_ffn, w_in, nsa_cmp_pos, nsa_w_cmp_k, nsa_w_cmp_v, mlstm_gate_bias, mlstm_norm_w, w_up_nsa, w_up_mlstm, w_out, peer_w_q, peer_sub_keys, peer_u, peer_v):
    batch, seq, d = x.shape
    depth = w_ada.shape[0]
    h = x.reshape(batch * seq, d)
    c_pad = jnp.concatenate([c, jnp.zeros((SUBLANES - batch % SUBLANES, d), c.dtype)], axis=0)
    for l in range(depth):
        mod = _adaln(c_pad, w_ada[l], b_ada[l])[:batch]
        mod3 = mod.reshape(batch, 6, d)
        h = _layer(h, mod3, norm_pre_mix[l], norm_post_mix[l], norm_pre_ffn[l], norm_post_ffn[l], w_in[l],
                   nsa_cmp_pos[l], nsa_w_cmp_k[l], nsa_w_cmp_v[l], mlstm_gate_bias[l], mlstm_norm_w[l],
                   w_up_nsa[l], w_up_mlstm[l], w_out[l], peer_w_q[l], peer_sub_keys[l], peer_u[l], peer_v[l],
                   batch=batch, seq=seq)
    return h.reshape(batch, seq, d).astype(x.dtype)
```

```python
import functools
import math

import jax
import jax.numpy as jnp
from jax import lax
from jax.experimental import pallas as pl
from jax.experimental.pallas import tpu as pltpu

D_MODEL = 4096
BATCH = 4
SEQ = 2048
NSA_HEADS = 16
NSA_KV_GROUPS = 4
NSA_HEAD_DIM = 128
CMP_BLOCK = 32
CMP_STRIDE = 16
SLC_BLOCK = 64
SLC_TOPN = 16
WINDOW = 512
ML_HEADS = 8
ML_QK_DIM = 256
ML_V_DIM = 512
PEER_HEADS = 8
PEER_KEY_DIM = 256
PEER_N_KEYS = 128
PEER_TOPK = 16

NEG_INF = -1e30
BIG = 1e9
TINY = 1e-30
EPS = 1e-6

F32 = jnp.float32
BF16 = jnp.bfloat16

V7X_VMEM_LIMIT_BYTES = 56 * 1024 * 1024
LANES = 128
SUBLANES = 8

ROW_TILE = 256
MM_BM = 1024
MM_BN = 1024
ADA_BN = 512
NSA_TQ = 128
NSA_TK = 512
ML_CHUNK_LEN = 128
PEER_TOPK_TT = 256
PEER_TT = 512
PEER_TE = 512

_NT = (((1,), (1,)), ((), ()))


def _params(*sem):
    return pltpu.CompilerParams(dimension_semantics=sem, vmem_limit_bytes=V7X_VMEM_LIMIT_BYTES)


def _ind(cond, dtype):
    wide = jnp.int32 if jnp.issubdtype(dtype, jnp.integer) else F32
    return jnp.where(cond, jnp.ones((), wide), jnp.zeros((), wide)).astype(dtype)


def _log2(n):
    k = int(math.log2(n))
    assert 1 << k == n
    return k


def _split3(x):
    hi = x.astype(BF16)
    r1 = x - hi.astype(F32)
    mid = r1.astype(BF16)
    lo = (r1 - mid.astype(F32)).astype(BF16)
    return hi, mid, lo


def _adaln_kernel(c_ref, w_ref, b_ref, o_ref):
    c = c_ref[...]
    cond = (c * jax.nn.sigmoid(c)).astype(BF16)
    o_ref[...] = jnp.dot(cond, w_ref[...].astype(BF16), preferred_element_type=F32) + b_ref[...]


def _adaln(c_pad, w_ada, b_ada):
    rows, d = c_pad.shape
    n = w_ada.shape[1]
    bn = min(ADA_BN, n)
    return pl.pallas_call(
        _adaln_kernel,
        grid=(n // bn,),
        in_specs=[
            pl.BlockSpec((rows, d), lambda j: (0, 0)),
            pl.BlockSpec((d, bn), lambda j: (0, j)),
            pl.BlockSpec((1, bn), lambda j: (0, j)),
        ],
        out_specs=pl.BlockSpec((rows, bn), lambda j: (0, j)),
        out_shape=jax.ShapeDtypeStruct((rows, n), F32),
        compiler_params=_params("parallel"),
        name="adaln",
    )(c_pad, w_ada, b_ada.reshape(1, n))


def _rms(x, w):
    return x * lax.rsqrt(jnp.mean(x * x, axis=-1, keepdims=True) + EPS) * w


def _prenorm_kernel(x_ref, w_ref, mod_ref, u_ref):
    y = _rms(x_ref[...], w_ref[...])
    u_ref[...] = (y * (1.0 + mod_ref[1:2, :]) + mod_ref[0:1, :]).astype(u_ref.dtype)


def _prenorm(x2, w, mod3, seq):
    n, d = x2.shape
    tr = min(ROW_TILE, seq)
    return pl.pallas_call(
        _prenorm_kernel,
        grid=(n // tr,),
        in_specs=[
            pl.BlockSpec((tr, d), lambda i: (i, 0)),
            pl.BlockSpec((1, d), lambda i: (0, 0)),
            pl.BlockSpec((None, 6, d), lambda i: ((i * tr) // seq, 0, 0)),
        ],
        out_specs=pl.BlockSpec((tr, d), lambda i: (i, 0)),
        out_shape=jax.ShapeDtypeStruct((n, d), BF16),
        compiler_params=_params("parallel"),
        name="prenorm",
    )(x2, w.reshape(1, d), mod3)


def _midnorm_kernel(x_ref, y_ref, w1_ref, w2_ref, mod_ref, h_ref, u_ref, ut_ref):
    h = x_ref[...] + mod_ref[2:3, :] * _rms(y_ref[...].astype(F32), w1_ref[...])
    h_ref[...] = h
    u = _rms(h, w2_ref[...]) * (1.0 + mod_ref[4:5, :]) + mod_ref[3:4, :]
    u_ref[...] = u.astype(u_ref.dtype)
    ut_ref[...] = u.T.astype(ut_ref.dtype)


def _midnorm(x2, y, w_post, w_pre, mod3, seq):
    n, d = x2.shape
    tr = min(ROW_TILE, seq)
    row = pl.BlockSpec((tr, d), lambda i: (i, 0))
    vec = pl.BlockSpec((1, d), lambda i: (0, 0))
    return pl.pallas_call(
        _midnorm_kernel,
        grid=(n // tr,),
        in_specs=[row, row, vec, vec, pl.BlockSpec((None, 6, d), lambda i: ((i * tr) // seq, 0, 0))],
        out_specs=[row, row, pl.BlockSpec((d, tr), lambda i: (0, i))],
        out_shape=[jax.ShapeDtypeStruct((n, d), F32), jax.ShapeDtypeStruct((n, d), BF16),
                   jax.ShapeDtypeStruct((d, n), BF16)],
        compiler_params=_params("parallel"),
        name="midnorm",
    )(x2, y, w_post.reshape(1, d), w_pre.reshape(1, d), mod3)


def _finalnorm_kernel(h_ref, y_ref, w_ref, mod_ref, o_ref):
    o_ref[...] = h_ref[...] + mod_ref[5:6, :] * _rms(y_ref[...].astype(F32), w_ref[...])


def _finalnorm(h1, y, w, mod3, seq):
    n, d = h1.shape
    tr = min(ROW_TILE, seq)
    row = pl.BlockSpec((tr, d), lambda i: (i, 0))
    return pl.pallas_call(
        _finalnorm_kernel,
        grid=(n // tr,),
        in_specs=[row, row, pl.BlockSpec((1, d), lambda i: (0, 0)),
                  pl.BlockSpec((None, 6, d), lambda i: ((i * tr) // seq, 0, 0))],
        out_specs=row,
        out_shape=jax.ShapeDtypeStruct((n, d), F32),
        compiler_params=_params("parallel"),
        name="finalnorm",
    )(h1, y, w.reshape(1, d), mod3)


def _gelu(x):
    return 0.5 * x * (1.0 + lax.erf(x * (2.0 ** -0.5)))


def _mm_kernel(*refs, has_gate, has_add, gelu):
    a_ref, b_ref = refs[0], refs[1]
    o_ref = refs[-1]
    acc = jnp.dot(a_ref[...], b_ref[...], preferred_element_type=F32)
    nxt = 2
    if has_gate:
        acc = acc * jax.nn.sigmoid(refs[nxt][...].astype(F32))
        nxt += 1
    if has_add:
        acc = acc + refs[nxt][...].astype(F32)
    if gelu:
        acc = _gelu(acc)
    o_ref[...] = acc.astype(o_ref.dtype)


def _matmul(a, b, out_dtype, gate=None, gate_col=0, add=None, gelu=False):
    m, k = a.shape
    n = b.shape[1]
    bm = min(MM_BM, m)
    bn = min(MM_BN, n)
    in_specs = [pl.BlockSpec((bm, k), lambda i, j: (i, 0)), pl.BlockSpec((k, bn), lambda i, j: (0, j))]
    args = [a, b]
    if gate is not None:
        goff = gate_col // bn
        assert goff * bn == gate_col
        in_specs.append(pl.BlockSpec((bm, bn), lambda i, j: (i, goff + j)))
        args.append(gate)
    if add is not None:
        in_specs.append(pl.BlockSpec((bm, bn), lambda i, j: (i, j)))
        args.append(add)
    return pl.pallas_call(
        functools.partial(_mm_kernel, has_gate=gate is not None, has_add=add is not None, gelu=gelu),
        grid=(m // bm, n // bn),
        in_specs=in_specs,
        out_specs=pl.BlockSpec((bm, bn), lambda i, j: (i, j)),
        out_shape=jax.ShapeDtypeStruct((m, n), out_dtype),
        compiler_params=_params("parallel", "arbitrary"),
        name="matmul",
    )(*args)


def _pos_columns(pos, width):
    lane = lax.broadcasted_iota(jnp.int32, (pos.shape[0], width), 1)
    coarse = (pos >> 6) << 6
    vals = jnp.where(lane < 3, coarse, jnp.where(lane < 6, pos & 63, 0))
    return vals.astype(F32).astype(BF16)


def _nsa_kernel(slopes_ref, q_ref, kcm_ref, vcm_ref, ks_ref, vs_ref, kw_ref, vw_ref, gz_ref, wck_ref, wcv_ref,
                pos_ref, o_ref, kc_ref, vc_ref, pext_ref, xs_ref, acc_ref, qext_ref, *, seq, tq, tk,
                heads_per_group):
    R = heads_per_group
    dk = NSA_HEAD_DIM
    g = pl.program_id(1)
    qi = pl.program_id(2)
    t0 = qi * tq
    n_cmp = (seq - CMP_BLOCK) // CMP_STRIDE + 1
    n_slc = seq // SLC_BLOCK
    n_sel = min(SLC_TOPN, n_slc)
    half = CMP_STRIDE * dk
    scale = dk ** -0.5
    c_exp = scale * math.log2(math.e)
    rows = R * tq

    @pl.when(qi == 0)
    def _():
        pext_ref[...] = _pos_columns(lax.broadcasted_iota(jnp.int32, (seq, 1), 0), LANES)
        cpos = lax.broadcasted_iota(jnp.int32, (LANES, 1), 0) * CMP_STRIDE + (CMP_BLOCK - 1)
        for src_ref, w_ref, dst, pi in ((kcm_ref, wck_ref, kc_ref, 0), (vcm_ref, wcv_ref, vc_ref, 1)):
            xs_ref[...] = src_ref[...].astype(F32)
            x = jnp.concatenate([xs_ref[pl.ds(l, LANES, stride=CMP_STRIDE), :].astype(BF16)
                                 for l in range(CMP_STRIDE)], axis=1)
            ya = jnp.dot(x, w_ref[0:half, :], preferred_element_type=F32)
            yb = jnp.dot(x, w_ref[half:2 * half, :], preferred_element_type=F32)
            pos = jnp.broadcast_to(pos_ref[pi], (SUBLANES, 2 * half)).astype(BF16)
            pb = jnp.dot(pos, w_ref[...], preferred_element_type=F32)[0:1, :]
            comp = (ya + pltpu.roll(yb, LANES - 1, 0) + pb).astype(BF16)
            dst[...] = jnp.concatenate([comp, _pos_columns(cpos, LANES)], axis=1) if pi == 0 else comp

        head_of_row = lax.broadcasted_iota(jnp.int32, (rows, LANES), 0) >> _log2(tq)
        lane6 = lax.broadcasted_iota(jnp.int32, (rows, LANES), 1)
        term = jnp.where(lane6 < 3, lane6, lane6 - 3)
        q_ext = jnp.zeros((rows, LANES), F32)
        for r in range(R):
            for k in range(3):
                q_ext = jnp.where((head_of_row == r) & (term == k) & (lane6 < 6), slopes_ref[g, r, k], q_ext)
        qext_ref[...] = q_ext.astype(BF16)

    row_id = lax.broadcasted_iota(jnp.int32, (rows, 1), 0)
    t_col = t0 + (row_id & (tq - 1))
    q = q_ref[...]
    qs = jnp.concatenate([q[:, r * dk:(r + 1) * dk] for r in range(R)], axis=0)
    qa = jnp.concatenate([qs, qext_ref[...]], axis=1)

    n_idx = lax.broadcasted_iota(jnp.int32, (rows, LANES), 1)
    cvalid = (t_col >= n_idx * CMP_STRIDE + CMP_BLOCK - 1) & (n_idx < n_cmp)
    sc = lax.dot_general(qa, kc_ref[...], _NT, preferred_element_type=F32)
    sc = jnp.where(cvalid, sc, NEG_INF)
    mc = jnp.max(sc, axis=-1, keepdims=True)
    p_cmp = jnp.where(cvalid, jnp.exp2((sc - mc) * c_exp), 0.0)
    p_cmp = p_cmp * (1.0 / jnp.maximum(jnp.sum(p_cmp, axis=-1, keepdims=True), TINY))
    o_cmp = jnp.dot(p_cmp.astype(BF16), vc_ref[...], preferred_element_type=F32)

    psum = p_cmp[0:tq]
    for r in range(1, R):
        psum = psum + p_cmp[r * tq:(r + 1) * tq]
    jj = lax.broadcasted_iota(jnp.int32, (LANES, LANES), 0)
    nn = lax.broadcasted_iota(jnp.int32, (LANES, LANES), 1)
    ovl = _ind((nn * CMP_STRIDE < jj * SLC_BLOCK + SLC_BLOCK) & (nn * CMP_STRIDE + CMP_BLOCK > jj * SLC_BLOCK)
               & (jj < n_slc) & (nn < n_cmp), BF16)
    imp_t = jnp.zeros((LANES, tq), F32)
    for part in _split3(psum):
        imp_t = imp_t + lax.dot_general(ovl, part, _NT, preferred_element_type=F32)
    imp_t = imp_t[0:n_slc]
    j_i = lax.broadcasted_iota(jnp.int32, (n_slc, tq), 0)
    t_i = t0 + lax.broadcasted_iota(jnp.int32, (n_slc, tq), 1)
    cur = t_i >> _log2(SLC_BLOCK)
    causal = j_i * SLC_BLOCK <= t_i
    forced = (j_i == 0) | (j_i == cur) | (j_i == cur - 1)
    score = jnp.where(causal, jnp.where(forced, BIG, imp_t), -BIG)
    rank = jnp.zeros((n_slc, tq), jnp.int32)
    for jp in range(n_slc):
        row = score[jp:jp + 1, :]
        beats = (row > score) | ((row == score) & (j_i > jp))
        rank = rank + _ind(beats, jnp.int32)
    sel_t = jnp.where(rank < n_sel, 0.0, NEG_INF)
    sel_t = jnp.concatenate([sel_t, jnp.zeros((LANES - n_slc, tq), F32)], axis=0)
    sel_neg = sel_t.T.astype(BF16)
    tq_col = t0 + lax.broadcasted_iota(jnp.int32, (tq, 1), 0)

    def attend(s, v):
        p = jnp.exp2((s - jnp.max(s, axis=-1, keepdims=True)) * c_exp).astype(BF16)
        ones = _ind(lax.broadcasted_iota(jnp.int32, (v.shape[0], LANES), 1) == 0, BF16)
        pv = jnp.dot(p, jnp.concatenate([v, ones], axis=1), preferred_element_type=F32)
        return pv[:, 0:dk] / jnp.maximum(pv[:, dk:dk + 1], TINY)

    def slc_variant(nk):
        ka = jnp.concatenate([ks_ref[0:nk, :], pext_ref[0:nk, :]], axis=1)
        ej = lax.broadcasted_iota(jnp.int32, (LANES, nk), 0)
        ec = lax.broadcasted_iota(jnp.int32, (LANES, nk), 1)
        expand = _ind(ej == (ec >> _log2(SLC_BLOCK)), BF16)
        madd = jnp.dot(sel_neg, expand, preferred_element_type=F32)
        madd = madd + jnp.where(lax.broadcasted_iota(jnp.int32, (tq, nk), 1) <= tq_col, 0.0, NEG_INF)
        s = lax.dot_general(qa, ka, _NT, preferred_element_type=F32) + jnp.concatenate([madd] * R, axis=0)
        acc_ref[...] = attend(s, vs_ref[0:nk, :])

    variant = (t0 + tq - 1) // tk
    for vi in range(seq // tk):
        pl.when(variant == vi)(functools.partial(slc_variant, (vi + 1) * tk))
    o_slc = acc_ref[...]

    span = WINDOW + tq
    ws = pl.multiple_of(jnp.maximum(t0 - WINDOW, 0), tq)
    kwa = jnp.concatenate([kw_ref[pl.ds(ws, span), :], pext_ref[pl.ds(ws, span), :]], axis=1)
    dw_i = tq_col - (ws + lax.broadcasted_iota(jnp.int32, (tq, span), 1))
    wadd = jnp.where((dw_i >= 0) & (dw_i < WINDOW), 0.0, NEG_INF)
    sw = lax.dot_general(qa, kwa, _NT, preferred_element_type=F32) + jnp.concatenate([wadd] * R, axis=0)
    o_win = attend(sw, vw_ref[pl.ds(ws, span), :])

    gz = gz_ref[...]
    gsel = jnp.zeros((tq, 3 * R), F32)
    for gg in range(NSA_KV_GROUPS):
        gsel = jnp.where(g == gg, gz[:, 3 * R * gg:3 * R * (gg + 1)], gsel)
    gts = jax.nn.sigmoid(gsel)
    for r in range(R):
        sl = slice(r * tq, (r + 1) * tq)
        o = (gts[:, 3 * r:3 * r + 1] * o_cmp[sl] + gts[:, 3 * r + 1:3 * r + 2] * o_slc[sl]
             + gts[:, 3 * r + 2:3 * r + 3] * o_win[sl])
        o_ref[:, r * dk:(r + 1) * dk] = o.astype(o_ref.dtype)


def _nsa(slopes, z, gz, wck, wcv, pos, *, batch, seq, kv_col):
    G = NSA_KV_GROUPS
    R = NSA_HEADS // G
    dk = NSA_HEAD_DIM
    tq = NSA_TQ
    tk = NSA_TK
    nq = seq // tq
    assert seq // CMP_STRIDE == LANES and CMP_BLOCK == 2 * CMP_STRIDE and seq >= WINDOW + tq and seq % tk == 0
    kv0 = kv_col // dk

    def kv_spec(i):
        return pl.BlockSpec((seq, dk), lambda b, g, qi: (b, kv0 + i * G + g))

    w_spec = pl.BlockSpec((CMP_BLOCK * dk, dk), lambda b, g, qi: (0, 0))
    rows = R * tq
    assert dk == LANES and 3 * NSA_HEADS <= LANES
    return pl.pallas_call(
        functools.partial(_nsa_kernel, seq=seq, tq=tq, tk=tk, heads_per_group=R),
        grid=(batch, G, nq),
        in_specs=[
            pl.BlockSpec(memory_space=pltpu.SMEM),
            pl.BlockSpec((tq, R * dk), lambda b, g, qi: (b * nq + qi, g)),
            kv_spec(0), kv_spec(1), kv_spec(2), kv_spec(3), kv_spec(4), kv_spec(5),
            pl.BlockSpec((tq, LANES), lambda b, g, qi: (b * nq + qi, 0)),
            w_spec, w_spec,
            pl.BlockSpec((2, 1, CMP_BLOCK * dk), lambda b, g, qi: (0, 0, 0)),
        ],
        out_specs=pl.BlockSpec((tq, R * dk), lambda b, g, qi: (b * nq + qi, g)),
        out_shape=jax.ShapeDtypeStruct((batch * seq, NSA_HEADS * dk), BF16),
        scratch_shapes=[
            pltpu.VMEM((LANES, 2 * dk), BF16), pltpu.VMEM((LANES, dk), BF16),
            pltpu.VMEM((seq, LANES), BF16), pltpu.VMEM((seq, dk), F32), pltpu.VMEM((rows, dk), F32),
            pltpu.VMEM((rows, LANES), BF16),
        ],
        compiler_params=_params("parallel", "parallel", "arbitrary"),
        name="nsa",
    )(slopes, z, z, z, z, z, z, z, gz, wck, wcv, pos)


def _log_sigmoid(x):
    return jnp.minimum(x, 0.0) - jnp.log1p(jnp.exp(-jnp.abs(x)))


def _mlstm_kernel(bias_ref, q_ref, k_ref, v_ref, og_ref, gz_ref, nw_ref, o_ref, c_ref, *, seq, L, gate_lane):
    dqk = ML_QK_DIM
    dv = ML_V_DIM
    nc = seq // L
    h_id = pl.program_id(1)
    lane = lax.broadcasted_iota(jnp.int32, (L, LANES), 1)
    i_lane = gate_lane + h_id
    f_lane = i_lane + ML_HEADS

    def gate_forms(blk, which, bias):
        col = jnp.sum(jnp.where(lane == which, blk, 0.0), axis=1, keepdims=True) + bias
        return col, jnp.broadcast_to(col, (L, LANES)).T[0:1, :]

    ext = dv + LANES
    ii = lax.broadcasted_iota(jnp.int32, (L, L), 0)
    kk = lax.broadcasted_iota(jnp.int32, (L, L), 1)
    causal = kk <= ii
    tri = _ind(causal, BF16)
    tri_t = _ind(ii <= kk, BF16)
    ones_slab = _ind(lax.broadcasted_iota(jnp.int32, (L, LANES), 1) == 0, BF16)
    c_ref[...] = jnp.zeros((dqk, ext), F32)

    def cumsum_mats(lf_row, lf_col):
        lr = jnp.broadcast_to(lf_row, (L, L))
        lc = jnp.broadcast_to(lf_col, (L, L))
        b_row = jnp.zeros((L, L), F32)
        b_col = jnp.zeros((L, L), F32)
        for part in _split3(lr)[:2]:
            b_row = b_row + jnp.dot(part, tri_t, preferred_element_type=F32)
        for part in _split3(lc)[:2]:
            b_col = b_col + jnp.dot(tri, part, preferred_element_type=F32)
        return b_row, b_col

    def body(c, m):
        r0 = pl.multiple_of(c * L, L)
        qc = q_ref[pl.ds(r0, L), :]
        kc = k_ref[pl.ds(r0, L), :] * (dqk ** -0.5)
        v_ext = jnp.concatenate([v_ref[pl.ds(r0, L), :], ones_slab], axis=1)
        gblk = gz_ref[pl.ds(r0, L), :]
        i_col, i_row = gate_forms(gblk, i_lane, bias_ref[0, h_id])
        f_col, f_row = gate_forms(gblk, f_lane, bias_ref[1, h_id])
        lf_col = _log_sigmoid(f_col)
        lf_row = _log_sigmoid(f_row)
        b_row, b_col = cumsum_mats(lf_row, lf_col)
        log_d = jnp.where(causal, b_col - b_row + i_row, NEG_INF)
        b_c = b_col[:, 0:1]
        m_inter = b_c + m
        m_t = jnp.maximum(m_inter, jnp.max(log_d, axis=-1, keepdims=True))
        d = jnp.exp(log_d - m_t)
        s = lax.dot_general(qc, kc, _NT, preferred_element_type=F32) * d
        w_inter = jnp.exp(m_inter - m_t)
        inter = jnp.dot(qc, c_ref[...].astype(BF16), preferred_element_type=F32)
        intra = jnp.dot(s.astype(BF16), v_ext, preferred_element_type=F32)
        numden = w_inter * inter + intra
        den = numden[:, dv:dv + 1]
        h = numden[:, 0:dv] / jnp.maximum(jnp.abs(den), jnp.exp(-m_t))
        g_tot = b_row[0:1, L - 1:L]
        a_row = g_tot - b_row[0:1, :] + i_row
        a_col = g_tot - b_c + i_col
        m_new = jnp.maximum(g_tot + m, jnp.max(a_row, axis=-1, keepdims=True))
        decay = jnp.exp(g_tot + m - m_new)
        w_tok = jnp.exp(a_col - m_new)
        kw_t = (kc.astype(F32) * w_tok).T.astype(BF16)
        c_ref[...] = decay * c_ref[...] + jnp.dot(kw_t, v_ext, preferred_element_type=F32)
        hn = _rms(h, nw_ref[...]) * jax.nn.sigmoid(og_ref[pl.ds(r0, L), :].astype(F32))
        o_ref[pl.ds(r0, L), :] = hn.astype(o_ref.dtype)
        return m_new

    lax.fori_loop(0, nc, body, jnp.zeros((1, 1), F32))


def _mlstm(gate_bias, z, gz, norm_w, *, batch, seq, q_col, k_col, v_col, o_col, gate_lane):
    H = ML_HEADS
    dqk = ML_QK_DIM
    dv = ML_V_DIM
    L = min(ML_CHUNK_LEN, seq)
    assert L == LANES and gate_lane + 2 * H <= LANES
    return pl.pallas_call(
        functools.partial(_mlstm_kernel, seq=seq, L=L, gate_lane=gate_lane),
        grid=(batch, H),
        in_specs=[
            pl.BlockSpec(memory_space=pltpu.SMEM),
            pl.BlockSpec((seq, dqk), lambda b, h: (b, q_col // dqk + h)),
            pl.BlockSpec((seq, dqk), lambda b, h: (b, k_col // dqk + h)),
            pl.BlockSpec((seq, dv), lambda b, h: (b, v_col // dv + h)),
            pl.BlockSpec((seq, dv), lambda b, h: (b, o_col // dv + h)),
            pl.BlockSpec((seq, LANES), lambda b, h: (b, 0)),
            pl.BlockSpec((None, 1, dv), lambda b, h: (h, 0, 0)),
        ],
        out_specs=pl.BlockSpec((seq, dv), lambda b, h: (b, h)),
        out_shape=jax.ShapeDtypeStruct((batch * seq, H * dv), BF16),
        scratch_shapes=[pltpu.VMEM((dqk, dv + LANES), F32)],
        compiler_params=_params("parallel", "parallel"),
        name="mlstm",
    )(gate_bias, z, z, z, z, gz, norm_w.reshape(H, 1, dv))


def _peer_pairs():
    return [(a, PEER_TOPK // (a + 1)) for a in range(PEER_TOPK)]


def _topk_multiset(s, k):
    tt = s.shape[1]
    riota = lax.broadcasted_iota(jnp.int32, (k, tt), 0).astype(F32)
    v = jnp.full((k, tt), -jnp.inf, F32)
    taken = jnp.zeros((1, tt), F32)
    rem = s
    for _ in range(k):
        m = jnp.max(rem, axis=0, keepdims=True)
        eq = rem == m
        cnt = jnp.sum(_ind(eq, F32), axis=0, keepdims=True)
        v = jnp.where((riota >= taken) & (riota < taken + cnt), m, v)
        taken = taken + cnt
        rem = jnp.where(eq, -jnp.inf, rem)
    return v


def _peer_topk_kernel(q_ref, khi_ref, klo_ref, s1_ref, s2_ref, e1_ref, e2_ref, tau_ref, cand_ref):
    H = PEER_HEADS
    half = PEER_KEY_DIM // 2
    K = PEER_TOPK
    pairs = _peer_pairs()
    n_cand = sum(nb for _, nb in pairs)
    tt = q_ref.shape[0]
    for h in range(H):
        s_parts = []
        for side in range(2):
            qf = q_ref[:, (2 * h + side) * half:(2 * h + side + 1) * half]
            q_hi = qf.astype(BF16)
            q_lo = (qf - q_hi.astype(F32)).astype(BF16)
            k_hi = khi_ref[side, h]
            k_lo = klo_ref[side, h]
            s = (lax.dot_general(k_hi, q_hi, _NT, preferred_element_type=F32)
                 + lax.dot_general(k_hi, q_lo, _NT, preferred_element_type=F32)
                 + lax.dot_general(k_lo, q_hi, _NT, preferred_element_type=F32))
            s_parts.append(s)
        s1, s2 = s_parts
        v1 = _topk_multiset(s1, K)
        v2 = _topk_multiset(s2, K)
        off = 0
        for a, nb in pairs:
            cand_ref[off:off + nb, :] = v1[a:a + 1, :] + v2[0:nb, :]
            off += nb
        cand_ref[n_cand:, :] = jnp.full((cand_ref.shape[0] - n_cand, tt), -jnp.inf, F32)
        rem = cand_ref[...]
        taken = jnp.zeros((1, tt), F32)
        tau = jnp.zeros((1, tt), F32)
        zsum = jnp.zeros((1, tt), F32)
        cmax = v1[0:1, :] + v2[0:1, :]
        for _ in range(K):
            m = jnp.max(rem, axis=0, keepdims=True)
            eq = rem == m
            cnt = jnp.sum(_ind(eq, F32), axis=0, keepdims=True)
            take = jnp.minimum(cnt, jnp.maximum(K - taken, 0.0))
            zsum = zsum + take * jnp.exp(m - cmax)
            tau = jnp.where((taken < K) & (taken + cnt >= K), m, tau)
            taken = taken + cnt
            rem = jnp.where(eq, -jnp.inf, rem)
        s1_ref[h] = s1
        s2_ref[h] = s2
        e1_ref[h] = jnp.exp(s1 - v1[0:1, :])
        e2_ref[h] = jnp.exp(s2 - v2[0:1, :]) / zsum
        tau_ref[h:h + 1, :] = tau


def _peer_topk(pq, k_hi, k_lo):
    n = pq.shape[0]
    H = PEER_HEADS
    nk = PEER_N_KEYS
    tt = min(PEER_TOPK_TT, n)
    n_cand = sum(nb for _, nb in _peer_pairs())
    cand_rows = -(-n_cand // SUBLANES) * SUBLANES
    key_spec = pl.BlockSpec((2, H, nk, PEER_KEY_DIM // 2), lambda i: (0, 0, 0, 0))
    hkt = pl.BlockSpec((H, nk, tt), lambda i: (0, 0, i))
    hkn = jax.ShapeDtypeStruct((H, nk, n), F32)
    return pl.pallas_call(
        _peer_topk_kernel,
        grid=(n // tt,),
        in_specs=[pl.BlockSpec((tt, pq.shape[1]), lambda i: (i, 0)), key_spec, key_spec],
        out_specs=[hkt, hkt, hkt, hkt, pl.BlockSpec((H, tt), lambda i: (0, i))],
        out_shape=[hkn, hkn, hkn, hkn, jax.ShapeDtypeStruct((H, n), F32)],
        scratch_shapes=[pltpu.VMEM((cand_rows, tt), F32)],
        compiler_params=_params("parallel"),
        name="peer_topk",
    )(pq, k_hi, k_lo)


def _peer_mix_kernel(act_ref, vt_ref, s1_ref, e1_ref, s2_ref, e2_ref, tau_ref, o_ref, acc_ref, a_ref, *, n_i1):
    H = PEER_HEADS
    nk = PEER_N_KEYS
    tt = act_ref.shape[1]
    e = pl.program_id(1)
    pack = 2 * SUBLANES
    dot_chunk = 2 * nk

    @pl.when(e == 0)
    def _():
        acc_ref[...] = jnp.zeros(acc_ref.shape, F32)

    dots = []
    for i in range(n_i1):
        row = (e * n_i1 + i) % SUBLANES
        s1_rows = [s1_ref[h, pl.ds(row, 1), :] for h in range(H)]
        e1_rows = [e1_ref[h, pl.ds(row, 1), :] for h in range(H)]
        for lt in range(tt // LANES):
            ls = slice(lt * LANES, (lt + 1) * LANES)
            s1b = [jnp.broadcast_to(s1_rows[h][:, ls], (SUBLANES, LANES)) for h in range(H)]
            e1b = [jnp.broadcast_to(e1_rows[h][:, ls], (SUBLANES, LANES)) for h in range(H)]
            taub = [jnp.broadcast_to(tau_ref[h:h + 1, ls], (SUBLANES, LANES)) for h in range(H)]
            for sg in range(nk // pack):
                halves = []
                for half in range(2):
                    r0 = sg * pack + half * SUBLANES
                    w = jnp.zeros((SUBLANES, LANES), F32)
                    for h in range(H):
                        hit = (s1b[h] + s2_ref[h, r0:r0 + SUBLANES, ls]) >= taub[h]
                        w = w + jnp.where(hit, e1b[h] * e2_ref[h, r0:r0 + SUBLANES, ls], 0.0)
                    halves.append(w)
                rr = slice(i * nk + sg * pack, i * nk + (sg + 1) * pack)
                a_ref[rr, ls] = (act_ref[rr, ls].astype(F32) * jnp.concatenate(halves, axis=0)).astype(BF16)
        if (i + 1) * nk % dot_chunk == 0:
            ks = slice((i + 1) * nk - dot_chunk, (i + 1) * nk)
            dots.append(jnp.dot(vt_ref[:, ks], a_ref[ks, :], preferred_element_type=F32))
    acc_ref[...] += functools.reduce(lambda x, y: x + y, dots)

    @pl.when(e == pl.num_programs(1) - 1)
    def _():
        o_ref[...] = acc_ref[...].T.astype(o_ref.dtype)


def _peer_mix(act_t, pvt, s1, s2, e1, e2, tau):
    n_exp, n = act_t.shape
    d = pvt.shape[0]
    H = PEER_HEADS
    nk = PEER_N_KEYS
    tt = min(PEER_TT, n)
    te = PEER_TE
    n_i1 = te // nk
    assert n_i1 * nk == te and SUBLANES % n_i1 == 0 and n_i1 % 2 == 0

    def i1_spec():
        return pl.BlockSpec((H, SUBLANES, tt), lambda i, e: (0, (e * n_i1) // SUBLANES, i))

    def i2_spec():
        return pl.BlockSpec((H, nk, tt), lambda i, e: (0, 0, i))

    return pl.pallas_call(
        functools.partial(_peer_mix_kernel, n_i1=n_i1),
        grid=(n // tt, n_exp // te),
        in_specs=[
            pl.BlockSpec((te, tt), lambda i, e: (e, i)),
            pl.BlockSpec((d, te), lambda i, e: (0, e)),
            i1_spec(), i1_spec(), i2_spec(), i2_spec(),
            pl.BlockSpec((H, tt), lambda i, e: (0, i)),
        ],
        out_specs=pl.BlockSpec((tt, d), lambda i, e: (i, 0)),
        out_shape=jax.ShapeDtypeStruct((n, d), BF16),
        scratch_shapes=[pltpu.VMEM((d, tt), F32), pltpu.VMEM((te, tt), BF16)],
        compiler_params=_params("parallel", "arbitrary"),
        name="peer_mix",
    )(act_t, pvt, s1, e1, s2, e2, tau)


def _cast_transpose_kernel(x_ref, o_ref):
    o_ref[...] = x_ref[...].T.astype(o_ref.dtype)


def _cast_transpose(x, dtype):
    r, c = x.shape
    tr = min(PEER_TE, r)
    return pl.pallas_call(
        _cast_transpose_kernel,
        grid=(r // tr,),
        in_specs=[pl.BlockSpec((tr, c), lambda i: (i, 0))],
        out_specs=pl.BlockSpec((c, tr), lambda i: (0, i)),
        out_shape=jax.ShapeDtypeStruct((c, r), dtype),
        compiler_params=_params("parallel"),
        name="cast_transpose",
    )(x)


def _layer(h, mod3, norm_pre_mix, norm_post_mix, norm_pre_ffn, norm_post_ffn, w_in, cmp_pos, w_cmp_k, w_cmp_v,
           gate_bias, ml_norm_w, w_up_nsa, w_up_mlstm, w_out, peer_w_q, sub_keys, peer_u, peer_v, *, batch, seq):
    d = h.shape[1]
    G = NSA_KV_GROUPS
    R = NSA_HEADS // G
    dk = NSA_HEAD_DIM
    H = ML_HEADS
    nsa_w = NSA_HEADS * dk
    kv_w = 6 * G * dk
    mq_w = H * ML_QK_DIM
    mv_w = H * ML_V_DIM
    c_kv = nsa_w
    c_gn = c_kv + kv_w
    c_mq = c_gn + 3 * NSA_HEADS
    c_mk = c_mq + mq_w
    c_mv = c_mk + mq_w
    c_mo = c_mv + mv_w
    c_if = c_mo + mv_w
    c_mg = c_if + 2 * H
    w_main = jnp.concatenate([w_in[:, :c_gn], w_in[:, c_mq:c_if], w_in[:, c_mg:]], axis=1).astype(BF16)
    n_gate = 3 * NSA_HEADS + 2 * H
    w_gate = jnp.concatenate([w_in[:, c_gn:c_mq], w_in[:, c_if:c_mg],
                              jnp.zeros((d, LANES - n_gate), F32)], axis=1).astype(BF16)
    z_kv = nsa_w
    z_mq = z_kv + kv_w
    z_mk = z_mq + mq_w
    z_mv = z_mk + mq_w
    z_mo = z_mv + mv_w
    z_mg = z_mo + mv_w

    u = _prenorm(h, norm_pre_mix, mod3, seq)
    z = _matmul(u, w_main, BF16)
    gz = _matmul(u, w_gate, F32)

    slopes = jnp.exp2(-8.0 * jnp.arange(1, NSA_HEADS + 1, dtype=F32) / NSA_HEADS) / (dk ** -0.5)
    slopes = jnp.stack([t.astype(F32) for t in _split3(slopes)], axis=-1).reshape(G, R, 3)
    o_nsa = _nsa(slopes, z, gz, w_cmp_k.astype(BF16), w_cmp_v.astype(BF16),
                 cmp_pos.reshape(2, 1, CMP_BLOCK * dk), batch=batch, seq=seq, kv_col=z_kv)

    h_ml = _mlstm(gate_bias, z, gz, ml_norm_w, batch=batch, seq=seq, q_col=z_mq, k_col=z_mk, v_col=z_mv,
                  o_col=z_mo, gate_lane=3 * NSA_HEADS)

    t1 = _matmul(o_nsa, w_up_nsa.astype(BF16), BF16, gate=z, gate_col=z_mg)
    mix = _matmul(h_ml, w_up_mlstm.astype(BF16), BF16, gate=z, gate_col=z_mg + d, add=t1)
    y = _matmul(mix, w_out.astype(BF16), BF16)
    h1, u2, u2_t = _midnorm(h, y, norm_post_mix, norm_pre_ffn, mod3, seq)

    pq = _matmul(u2, peer_w_q.astype(BF16), F32)
    k_hi = sub_keys.astype(BF16)
    k_lo = (sub_keys - k_hi.astype(F32)).astype(BF16)
    s1, s2, e1, e2, tau = _peer_topk(pq, k_hi, k_lo)
    act_t = _matmul(peer_u.astype(BF16), u2_t, BF16, gelu=True)
    y2 = _peer_mix(act_t, _cast_transpose(peer_v, BF16), s1, s2, e1, e2, tau)
    return _finalnorm(h1, y2, norm_post_ffn, mod3, seq)


def kernel(x, c, w_ada, b_ada, norm_pre_mix, norm_post_mix, norm_pre_ffn, norm_post_ffn, w_in, nsa_cmp_pos, nsa_w_cmp_k, nsa_w_cmp_v, mlstm_gate_bias, mlstm_norm_w, w_up_nsa, w_up_mlstm, w_out, peer_w_q, peer_sub_keys, peer_u, peer_v):
    batch, seq, d = x.shape
    depth = w_ada.shape[0]
    h = x.reshape(batch * seq, d)
    c_pad = jnp.concatenate([c, jnp.zeros((SUBLANES - batch % SUBLANES, d), c.dtype)], axis=0)
    for l in range(depth):
        mod = _adaln(c_pad, w_ada[l], b_ada[l])[:batch]
        mod3 = mod.reshape(batch, 6, d)
        h = _layer(h, mod3, norm_pre_mix[l], norm_post_mix[l], norm_pre_ffn[l], norm_post_ffn[l], w_in[l],
                   nsa_cmp_pos[l], nsa_w_cmp_k[l], nsa_w_cmp_v[l], mlstm_gate_bias[l], mlstm_norm_w[l],
                   w_up_nsa[l], w_up_mlstm[l], w_out[l], peer_w_q[l], peer_sub_keys[l], peer_u[l], peer_v[l],
                   batch=batch, seq=seq)
    return h.reshape(batch, seq, d).astype(x.dtype)
```

```python
import functools
import math

import jax
import jax.numpy as jnp
from jax import lax
from jax.experimental import pallas as pl
from jax.experimental.pallas import tpu as pltpu

D_MODEL = 4096
BATCH = 4
SEQ = 2048
NSA_HEADS = 16
NSA_KV_GROUPS = 4
NSA_HEAD_DIM = 128
CMP_BLOCK = 32
CMP_STRIDE = 16
SLC_BLOCK = 64
SLC_TOPN = 16
WINDOW = 512
ML_HEADS = 8
ML_QK_DIM = 256
ML_V_DIM = 512
PEER_HEADS = 8
PEER_KEY_DIM = 256
PEER_N_KEYS = 128
PEER_TOPK = 16

NEG_INF = -1e30
BIG = 1e9
TINY = 1e-30
EPS = 1e-6

F32 = jnp.float32
BF16 = jnp.bfloat16

V7X_VMEM_LIMIT_BYTES = 56 * 1024 * 1024
LANES = 128
SUBLANES = 8

ROW_TILE = 256
MM_BM = 1024
MM_BN = 1024
MMW_BN = 512
MMA_BM = 512
ADA_BN = 512
NSA_TQ = 128
NSA_TK = 512
ML_CHUNK_LEN = 128
PEER_TOPK_TT = 256
PEER_TT = 512
PEER_TE = 512

_NT = (((1,), (1,)), ((), ()))


def _params(*sem):
    return pltpu.CompilerParams(dimension_semantics=sem, vmem_limit_bytes=V7X_VMEM_LIMIT_BYTES)


def _ind(cond, dtype):
    wide = jnp.int32 if jnp.issubdtype(dtype, jnp.integer) else F32
    return jnp.where(cond, jnp.ones((), wide), jnp.zeros((), wide)).astype(dtype)


def _log2(n):
    k = int(math.log2(n))
    assert 1 << k == n
    return k


def _split3(x):
    hi = x.astype(BF16)
    r1 = x - hi.astype(F32)
    mid = r1.astype(BF16)
    lo = (r1 - mid.astype(F32)).astype(BF16)
    return hi, mid, lo


def _adaln_kernel(c_ref, w_ref, b_ref, o_ref):
    c = c_ref[...]
    cond = (c * jax.nn.sigmoid(c)).astype(BF16)
    o_ref[...] = jnp.dot(cond, w_ref[...].astype(BF16), preferred_element_type=F32) + b_ref[...]


def _adaln(c_pad, w_ada, b_ada):
    rows, d = c_pad.shape
    n = w_ada.shape[1]
    bn = min(ADA_BN, n)
    return pl.pallas_call(
        _adaln_kernel,
        grid=(n // bn,),
        in_specs=[
            pl.BlockSpec((rows, d), lambda j: (0, 0)),
            pl.BlockSpec((d, bn), lambda j: (0, j)),
            pl.BlockSpec((1, bn), lambda j: (0, j)),
        ],
        out_specs=pl.BlockSpec((rows, bn), lambda j: (0, j)),
        out_shape=jax.ShapeDtypeStruct((rows, n), F32),
        compiler_params=_params("parallel"),
        name="adaln",
    )(c_pad, w_ada, b_ada.reshape(1, n))


def _rms(x, w):
    return x * lax.rsqrt(jnp.mean(x * x, axis=-1, keepdims=True) + EPS) * w


def _prenorm_kernel(x_ref, w_ref, mod_ref, u_ref):
    y = _rms(x_ref[...], w_ref[...])
    u_ref[...] = (y * (1.0 + mod_ref[1:2, :]) + mod_ref[0:1, :]).astype(u_ref.dtype)


def _prenorm(x2, w, mod3, seq):
    n, d = x2.shape
    tr = min(ROW_TILE, seq)
    return pl.pallas_call(
        _prenorm_kernel,
        grid=(n // tr,),
        in_specs=[
            pl.BlockSpec((tr, d), lambda i: (i, 0)),
            pl.BlockSpec((1, d), lambda i: (0, 0)),
            pl.BlockSpec((None, 6, d), lambda i: ((i * tr) // seq, 0, 0)),
        ],
        out_specs=pl.BlockSpec((tr, d), lambda i: (i, 0)),
        out_shape=jax.ShapeDtypeStruct((n, d), BF16),
        compiler_params=_params("parallel"),
        name="prenorm",
    )(x2, w.reshape(1, d), mod3)


def _midnorm_kernel(x_ref, y_ref, w1_ref, w2_ref, mod_ref, h_ref, u_ref, ut_ref):
    h = x_ref[...] + mod_ref[2:3, :] * _rms(y_ref[...].astype(F32), w1_ref[...])
    h_ref[...] = h
    u = _rms(h, w2_ref[...]) * (1.0 + mod_ref[4:5, :]) + mod_ref[3:4, :]
    u_ref[...] = u.astype(u_ref.dtype)
    ut_ref[...] = u.T.astype(ut_ref.dtype)


def _midnorm(x2, y, w_post, w_pre, mod3, seq):
    n, d = x2.shape
    tr = min(ROW_TILE, seq)
    row = pl.BlockSpec((tr, d), lambda i: (i, 0))
    vec = pl.BlockSpec((1, d), lambda i: (0, 0))
    return pl.pallas_call(
        _midnorm_kernel,
        grid=(n // tr,),
        in_specs=[row, row, vec, vec, pl.BlockSpec((None, 6, d), lambda i: ((i * tr) // seq, 0, 0))],
        out_specs=[row, row, pl.BlockSpec((d, tr), lambda i: (0, i))],
        out_shape=[jax.ShapeDtypeStruct((n, d), F32), jax.ShapeDtypeStruct((n, d), BF16),
                   jax.ShapeDtypeStruct((d, n), BF16)],
        compiler_params=_params("parallel"),
        name="midnorm",
    )(x2, y, w_post.reshape(1, d), w_pre.reshape(1, d), mod3)


def _finalnorm_kernel(h_ref, y_ref, w_ref, mod_ref, o_ref):
    o_ref[...] = h_ref[...] + mod_ref[5:6, :] * _rms(y_ref[...].astype(F32), w_ref[...])


def _finalnorm(h1, y, w, mod3, seq):
    n, d = h1.shape
    tr = min(ROW_TILE, seq)
    row = pl.BlockSpec((tr, d), lambda i: (i, 0))
    return pl.pallas_call(
        _finalnorm_kernel,
        grid=(n // tr,),
        in_specs=[row, row, pl.BlockSpec((1, d), lambda i: (0, 0)),
                  pl.BlockSpec((None, 6, d), lambda i: ((i * tr) // seq, 0, 0))],
        out_specs=row,
        out_shape=jax.ShapeDtypeStruct((n, d), F32),
        compiler_params=_params("parallel"),
        name="finalnorm",
    )(h1, y, w.reshape(1, d), mod3)


def _gelu(x):
    return 0.5 * x * (1.0 + lax.erf(x * (2.0 ** -0.5)))


def _mm_kernel(*refs, has_gate, has_add, gelu):
    a_ref, b_ref = refs[0], refs[1]
    o_ref = refs[-1]
    acc = jnp.dot(a_ref[...], b_ref[...], preferred_element_type=F32)
    nxt = 2
    if has_gate:
        acc = acc * jax.nn.sigmoid(refs[nxt][...].astype(F32))
        nxt += 1
    if has_add:
        acc = acc + refs[nxt][...].astype(F32)
    if gelu:
        acc = _gelu(acc)
    o_ref[...] = acc.astype(o_ref.dtype)


def _matmul(a, b, out_dtype, gate=None, gate_col=0, add=None, gelu=False):
    m, k = a.shape
    n = b.shape[1]
    bm = min(MM_BM, m)
    bn = min(MM_BN, n)
    in_specs = [pl.BlockSpec((bm, k), lambda i, j: (i, 0)), pl.BlockSpec((k, bn), lambda i, j: (0, j))]
    args = [a, b]
    if gate is not None:
        goff = gate_col // bn
        assert goff * bn == gate_col
        in_specs.append(pl.BlockSpec((bm, bn), lambda i, j: (i, goff + j)))
        args.append(gate)
    if add is not None:
        in_specs.append(pl.BlockSpec((bm, bn), lambda i, j: (i, j)))
        args.append(add)
    return pl.pallas_call(
        functools.partial(_mm_kernel, has_gate=gate is not None, has_add=add is not None, gelu=gelu),
        grid=(m // bm, n // bn),
        in_specs=in_specs,
        out_specs=pl.BlockSpec((bm, bn), lambda i, j: (i, j)),
        out_shape=jax.ShapeDtypeStruct((m, n), out_dtype),
        compiler_params=_params("parallel", "arbitrary"),
        name="matmul",
    )(*args)


def _mmw_kernel(*refs, shifts, has_gate, has_add, row_chunk):
    a_ref, w_ref = refs[0], refs[1]
    nxt = 2
    wn_ref = None
    if shifts is not None:
        wn_ref = refs[nxt]
        nxt += 1
    o_ref, wb_ref = refs[-2], refs[-1]
    j = pl.program_id(0)
    k, bn = wb_ref.shape

    def fill(shift):
        for r0 in range(0, k, row_chunk):
            rs = slice(r0, r0 + row_chunk)
            if shift == 0:
                wb_ref[rs, :] = w_ref[rs, :].astype(BF16)
            else:
                wide = jnp.concatenate([w_ref[rs, :], wn_ref[rs, :]], axis=1)
                wb_ref[rs, :] = wide[:, shift:shift + bn].astype(BF16)

    @pl.when(pl.program_id(1) == 0)
    def _():
        if shifts is None:
            fill(0)
        else:
            for j0, j1, shift in shifts:
                pl.when((j >= j0) & (j < j1))(functools.partial(fill, shift))

    acc = jnp.dot(a_ref[...], wb_ref[...], preferred_element_type=F32)
    if has_gate:
        acc = acc * jax.nn.sigmoid(refs[nxt][...].astype(F32))
        nxt += 1
    if has_add:
        acc = acc + refs[nxt][...].astype(F32)
    o_ref[...] = acc.astype(o_ref.dtype)


def _matmul_w(a, w, out_dtype, *, n_out=None, shifts=None, gate=None, gate_col=0, add=None):
    m, k = a.shape
    n = w.shape[1] if n_out is None else n_out
    bm = min(MM_BM, m)
    bn = min(MMW_BN, n)
    assert n % bn == 0 and m % bm == 0
    in_specs = [pl.BlockSpec((bm, k), lambda j, i: (i, 0)), pl.BlockSpec((k, bn), lambda j, i: (0, j))]
    args = [a, w]
    if shifts is not None:
        per = bn // LANES
        in_specs.append(pl.BlockSpec((k, LANES), lambda j, i: (0, (j + 1) * per)))
        args.append(w)
    if gate is not None:
        goff = gate_col // bn
        assert goff * bn == gate_col
        in_specs.append(pl.BlockSpec((bm, bn), lambda j, i: (i, goff + j)))
        args.append(gate)
    if add is not None:
        in_specs.append(pl.BlockSpec((bm, bn), lambda j, i: (i, j)))
        args.append(add)
    return pl.pallas_call(
        functools.partial(_mmw_kernel, shifts=shifts, has_gate=gate is not None, has_add=add is not None,
                          row_chunk=min(512, k)),
        grid=(n // bn, m // bm),
        in_specs=in_specs,
        out_specs=pl.BlockSpec((bm, bn), lambda j, i: (i, j)),
        out_shape=jax.ShapeDtypeStruct((m, n), out_dtype),
        scratch_shapes=[pltpu.VMEM((k, bn), BF16)],
        compiler_params=_params("parallel", "arbitrary"),
        name="matmul_w",
    )(*args)


def _mma_kernel(a_ref, b_ref, o_ref, ab_ref, *, gelu, row_chunk):
    @pl.when(pl.program_id(1) == 0)
    def _():
        for r0 in range(0, ab_ref.shape[0], row_chunk):
            ab_ref[r0:r0 + row_chunk, :] = a_ref[r0:r0 + row_chunk, :].astype(BF16)

    acc = jnp.dot(ab_ref[...], b_ref[...], preferred_element_type=F32)
    if gelu:
        acc = _gelu(acc)
    o_ref[...] = acc.astype(o_ref.dtype)


def _matmul_a(a, b, out_dtype, gelu=False):
    m, k = a.shape
    n = b.shape[1]
    bm = min(MMA_BM, m)
    bn = min(MM_BN, n)
    return pl.pallas_call(
        functools.partial(_mma_kernel, gelu=gelu, row_chunk=min(128, bm)),
        grid=(m // bm, n // bn),
        in_specs=[pl.BlockSpec((bm, k), lambda i, j: (i, 0)), pl.BlockSpec((k, bn), lambda i, j: (0, j))],
        out_specs=pl.BlockSpec((bm, bn), lambda i, j: (i, j)),
        out_shape=jax.ShapeDtypeStruct((m, n), out_dtype),
        scratch_shapes=[pltpu.VMEM((bm, k), BF16)],
        compiler_params=_params("parallel", "arbitrary"),
        name="matmul_a",
    )(a, b)


def _pos_columns(pos, width):
    lane = lax.broadcasted_iota(jnp.int32, (pos.shape[0], width), 1)
    coarse = (pos >> 6) << 6
    vals = jnp.where(lane < 3, coarse, jnp.where(lane < 6, pos & 63, 0))
    return vals.astype(F32).astype(BF16)


def _nsa_kernel(slopes_ref, q_ref, kcm_ref, vcm_ref, ks_ref, vs_ref, kw_ref, vw_ref, gz_ref, wck_ref, wcv_ref,
                pos_ref, o_ref, kc_ref, vc_ref, pext_ref, xs_ref, acc_ref, qext_ref, *, seq, tq, tk,
                heads_per_group):
    R = heads_per_group
    dk = NSA_HEAD_DIM
    g = pl.program_id(1)
    qi = pl.program_id(2)
    t0 = qi * tq
    n_cmp = (seq - CMP_BLOCK) // CMP_STRIDE + 1
    n_slc = seq // SLC_BLOCK
    n_sel = min(SLC_TOPN, n_slc)
    half = CMP_STRIDE * dk
    scale = dk ** -0.5
    c_exp = scale * math.log2(math.e)
    rows = R * tq

    @pl.when(qi == 0)
    def _():
        pext_ref[...] = _pos_columns(lax.broadcasted_iota(jnp.int32, (seq, 1), 0), LANES)
        cpos = lax.broadcasted_iota(jnp.int32, (LANES, 1), 0) * CMP_STRIDE + (CMP_BLOCK - 1)
        for src_ref, w_ref, dst, pi in ((kcm_ref, wck_ref, kc_ref, 0), (vcm_ref, wcv_ref, vc_ref, 1)):
            xs_ref[...] = src_ref[...].astype(F32)
            x = jnp.concatenate([xs_ref[pl.ds(l, LANES, stride=CMP_STRIDE), :].astype(BF16)
                                 for l in range(CMP_STRIDE)], axis=1)
            ya = jnp.dot(x, w_ref[0:half, :], preferred_element_type=F32)
            yb = jnp.dot(x, w_ref[half:2 * half, :], preferred_element_type=F32)
            pos = jnp.broadcast_to(pos_ref[pi], (SUBLANES, 2 * half)).astype(BF16)
            pb = jnp.dot(pos, w_ref[...], preferred_element_type=F32)[0:1, :]
            comp = (ya + pltpu.roll(yb, LANES - 1, 0) + pb).astype(BF16)
            dst[...] = jnp.concatenate([comp, _pos_columns(cpos, LANES)], axis=1) if pi == 0 else comp

        head_of_row = lax.broadcasted_iota(jnp.int32, (rows, LANES), 0) >> _log2(tq)
        lane6 = lax.broadcasted_iota(jnp.int32, (rows, LANES), 1)
        term = jnp.where(lane6 < 3, lane6, lane6 - 3)
        q_ext = jnp.zeros((rows, LANES), F32)
        for r in range(R):
            for k in range(3):
                q_ext = jnp.where((head_of_row == r) & (term == k) & (lane6 < 6), slopes_ref[g, r, k], q_ext)
        qext_ref[...] = q_ext.astype(BF16)

    row_id = lax.broadcasted_iota(jnp.int32, (rows, 1), 0)
    t_col = t0 + (row_id & (tq - 1))
    q = q_ref[...]
    qs = jnp.concatenate([q[:, r * dk:(r + 1) * dk] for r in range(R)], axis=0)
    qa = jnp.concatenate([qs, qext_ref[...]], axis=1)

    n_idx = lax.broadcasted_iota(jnp.int32, (rows, LANES), 1)
    cvalid = (t_col >= n_idx * CMP_STRIDE + CMP_BLOCK - 1) & (n_idx < n_cmp)
    sc = lax.dot_general(qa, kc_ref[...], _NT, preferred_element_type=F32)
    sc = jnp.where(cvalid, sc, NEG_INF)
    mc = jnp.max(sc, axis=-1, keepdims=True)
    p_cmp = jnp.where(cvalid, jnp.exp2((sc - mc) * c_exp), 0.0)
    p_cmp = p_cmp * (1.0 / jnp.maximum(jnp.sum(p_cmp, axis=-1, keepdims=True), TINY))
    o_cmp = jnp.dot(p_cmp.astype(BF16), vc_ref[...], preferred_element_type=F32)

    psum = p_cmp[0:tq]
    for r in range(1, R):
        psum = psum + p_cmp[r * tq:(r + 1) * tq]
    jj = lax.broadcasted_iota(jnp.int32, (LANES, LANES), 0)
    nn = lax.broadcasted_iota(jnp.int32, (LANES, LANES), 1)
    ovl = _ind((nn * CMP_STRIDE < jj * SLC_BLOCK + SLC_BLOCK) & (nn * CMP_STRIDE + CMP_BLOCK > jj * SLC_BLOCK)
               & (jj < n_slc) & (nn < n_cmp), BF16)
    imp_t = jnp.zeros((LANES, tq), F32)
    for part in _split3(psum):
        imp_t = imp_t + lax.dot_general(ovl, part, _NT, preferred_element_type=F32)
    imp_t = imp_t[0:n_slc]
    j_i = lax.broadcasted_iota(jnp.int32, (n_slc, tq), 0)
    t_i = t0 + lax.broadcasted_iota(jnp.int32, (n_slc, tq), 1)
    cur = t_i >> _log2(SLC_BLOCK)
    causal = j_i * SLC_BLOCK <= t_i
    forced = (j_i == 0) | (j_i == cur) | (j_i == cur - 1)
    score = jnp.where(causal, jnp.where(forced, BIG, imp_t), -BIG)
    rank = jnp.zeros((n_slc, tq), jnp.int32)
    for jp in range(n_slc):
        row = score[jp:jp + 1, :]
        beats = (row > score) | ((row == score) & (j_i > jp))
        rank = rank + _ind(beats, jnp.int32)
    sel_t = jnp.where(rank < n_sel, 0.0, NEG_INF)
    sel_t = jnp.concatenate([sel_t, jnp.zeros((LANES - n_slc, tq), F32)], axis=0)
    sel_neg = sel_t.T.astype(BF16)
    tq_col = t0 + lax.broadcasted_iota(jnp.int32, (tq, 1), 0)

    def attend(s, v):
        p = jnp.exp2((s - jnp.max(s, axis=-1, keepdims=True)) * c_exp).astype(BF16)
        ones = _ind(lax.broadcasted_iota(jnp.int32, (v.shape[0], LANES), 1) == 0, BF16)
        pv = jnp.dot(p, jnp.concatenate([v, ones], axis=1), preferred_element_type=F32)
        return pv[:, 0:dk] / jnp.maximum(pv[:, dk:dk + 1], TINY)

    def slc_variant(nk):
        ka = jnp.concatenate([ks_ref[0:nk, :], pext_ref[0:nk, :]], axis=1)
        ej = lax.broadcasted_iota(jnp.int32, (LANES, nk), 0)
        ec = lax.broadcasted_iota(jnp.int32, (LANES, nk), 1)
        expand = _ind(ej == (ec >> _log2(SLC_BLOCK)), BF16)
        madd = jnp.dot(sel_neg, expand, preferred_element_type=F32)
        madd = madd + jnp.where(lax.broadcasted_iota(jnp.int32, (tq, nk), 1) <= tq_col, 0.0, NEG_INF)
        s = lax.dot_general(qa, ka, _NT, preferred_element_type=F32) + jnp.concatenate([madd] * R, axis=0)
        acc_ref[...] = attend(s, vs_ref[0:nk, :])

    variant = (t0 + tq - 1) // tk
    for vi in range(seq // tk):
        pl.when(variant == vi)(functools.partial(slc_variant, (vi + 1) * tk))
    o_slc = acc_ref[...]

    span = WINDOW + tq
    ws = pl.multiple_of(jnp.maximum(t0 - WINDOW, 0), tq)
    kwa = jnp.concatenate([kw_ref[pl.ds(ws, span), :], pext_ref[pl.ds(ws, span), :]], axis=1)
    dw_i = tq_col - (ws + lax.broadcasted_iota(jnp.int32, (tq, span), 1))
    wadd = jnp.where((dw_i >= 0) & (dw_i < WINDOW), 0.0, NEG_INF)
    sw = lax.dot_general(qa, kwa, _NT, preferred_element_type=F32) + jnp.concatenate([wadd] * R, axis=0)
    o_win = attend(sw, vw_ref[pl.ds(ws, span), :])

    gz = gz_ref[...]
    gsel = jnp.zeros((tq, 3 * R), F32)
    for gg in range(NSA_KV_GROUPS):
        gsel = jnp.where(g == gg, gz[:, 3 * R * gg:3 * R * (gg + 1)], gsel)
    gts = jax.nn.sigmoid(gsel)
    for r in range(R):
        sl = slice(r * tq, (r + 1) * tq)
        o = (gts[:, 3 * r:3 * r + 1] * o_cmp[sl] + gts[:, 3 * r + 1:3 * r + 2] * o_slc[sl]
             + gts[:, 3 * r + 2:3 * r + 3] * o_win[sl])
        o_ref[:, r * dk:(r + 1) * dk] = o.astype(o_ref.dtype)


def _nsa(slopes, z, gz, wck, wcv, pos, *, batch, seq, kv_col):
    G = NSA_KV_GROUPS
    R = NSA_HEADS // G
    dk = NSA_HEAD_DIM
    tq = NSA_TQ
    tk = NSA_TK
    nq = seq // tq
    assert seq // CMP_STRIDE == LANES and CMP_BLOCK == 2 * CMP_STRIDE and seq >= WINDOW + tq and seq % tk == 0
    kv0 = kv_col // dk

    def kv_spec(i):
        return pl.BlockSpec((seq, dk), lambda b, g, qi: (b, kv0 + i * G + g))

    w_spec = pl.BlockSpec((CMP_BLOCK * dk, dk), lambda b, g, qi: (0, 0))
    rows = R * tq
    assert dk == LANES and 3 * NSA_HEADS <= LANES
    return pl.pallas_call(
        functools.partial(_nsa_kernel, seq=seq, tq=tq, tk=tk, heads_per_group=R),
        grid=(batch, G, nq),
        in_specs=[
            pl.BlockSpec(memory_space=pltpu.SMEM),
            pl.BlockSpec((tq, R * dk), lambda b, g, qi: (b * nq + qi, g)),
            kv_spec(0), kv_spec(1), kv_spec(2), kv_spec(3), kv_spec(4), kv_spec(5),
            pl.BlockSpec((tq, LANES), lambda b, g, qi: (b * nq + qi, 0)),
            w_spec, w_spec,
            pl.BlockSpec((2, 1, CMP_BLOCK * dk), lambda b, g, qi: (0, 0, 0)),
        ],
        out_specs=pl.BlockSpec((tq, R * dk), lambda b, g, qi: (b * nq + qi, g)),
        out_shape=jax.ShapeDtypeStruct((batch * seq, NSA_HEADS * dk), BF16),
        scratch_shapes=[
            pltpu.VMEM((LANES, 2 * dk), BF16), pltpu.VMEM((LANES, dk), BF16),
            pltpu.VMEM((seq, LANES), BF16), pltpu.VMEM((seq, dk), F32), pltpu.VMEM((rows, dk), F32),
            pltpu.VMEM((rows, LANES), BF16),
        ],
        compiler_params=_params("parallel", "parallel", "arbitrary"),
        name="nsa",
    )(slopes, z, z, z, z, z, z, z, gz, wck, wcv, pos)


def _log_sigmoid(x):
    return jnp.minimum(x, 0.0) - jnp.log1p(jnp.exp(-jnp.abs(x)))


def _mlstm_kernel(bias_ref, q_ref, k_ref, v_ref, og_ref, gz_ref, nw_ref, o_ref, c_ref, *, seq, L, gate_lane):
    dqk = ML_QK_DIM
    dv = ML_V_DIM
    nc = seq // L
    h_id = pl.program_id(1)
    lane = lax.broadcasted_iota(jnp.int32, (L, LANES), 1)
    i_lane = gate_lane + h_id
    f_lane = i_lane + ML_HEADS

    def gate_forms(blk, which, bias):
        col = jnp.sum(jnp.where(lane == which, blk, 0.0), axis=1, keepdims=True) + bias
        return col, jnp.broadcast_to(col, (L, LANES)).T[0:1, :]

    ext = dv + LANES
    ii = lax.broadcasted_iota(jnp.int32, (L, L), 0)
    kk = lax.broadcasted_iota(jnp.int32, (L, L), 1)
    causal = kk <= ii
    tri = _ind(causal, BF16)
    tri_t = _ind(ii <= kk, BF16)
    ones_slab = _ind(lax.broadcasted_iota(jnp.int32, (L, LANES), 1) == 0, BF16)
    c_ref[...] = jnp.zeros((dqk, ext), F32)

    def cumsum_mats(lf_row, lf_col):
        lr = jnp.broadcast_to(lf_row, (L, L))
        lc = jnp.broadcast_to(lf_col, (L, L))
        b_row = jnp.zeros((L, L), F32)
        b_col = jnp.zeros((L, L), F32)
        for part in _split3(lr)[:2]:
            b_row = b_row + jnp.dot(part, tri_t, preferred_element_type=F32)
        for part in _split3(lc)[:2]:
            b_col = b_col + jnp.dot(tri, part, preferred_element_type=F32)
        return b_row, b_col

    def body(c, m):
        r0 = pl.multiple_of(c * L, L)
        qc = q_ref[pl.ds(r0, L), :]
        kc = k_ref[pl.ds(r0, L), :] * (dqk ** -0.5)
        v_ext = jnp.concatenate([v_ref[pl.ds(r0, L), :], ones_slab], axis=1)
        gblk = gz_ref[pl.ds(r0, L), :]
        i_col, i_row = gate_forms(gblk, i_lane, bias_ref[0, h_id])
        f_col, f_row = gate_forms(gblk, f_lane, bias_ref[1, h_id])
        lf_col = _log_sigmoid(f_col)
        lf_row = _log_sigmoid(f_row)
        b_row, b_col = cumsum_mats(lf_row, lf_col)
        log_d = jnp.where(causal, b_col - b_row + i_row, NEG_INF)
        b_c = b_col[:, 0:1]
        m_inter = b_c + m
        m_t = jnp.maximum(m_inter, jnp.max(log_d, axis=-1, keepdims=True))
        d = jnp.exp(log_d - m_t)
        s = lax.dot_general(qc, kc, _NT, preferred_element_type=F32) * d
        w_inter = jnp.exp(m_inter - m_t)
        inter = jnp.dot(qc, c_ref[...].astype(BF16), preferred_element_type=F32)
        intra = jnp.dot(s.astype(BF16), v_ext, preferred_element_type=F32)
        numden = w_inter * inter + intra
        den = numden[:, dv:dv + 1]
        h = numden[:, 0:dv] / jnp.maximum(jnp.abs(den), jnp.exp(-m_t))
        g_tot = b_row[0:1, L - 1:L]
        a_row = g_tot - b_row[0:1, :] + i_row
        a_col = g_tot - b_c + i_col
        m_new = jnp.maximum(g_tot + m, jnp.max(a_row, axis=-1, keepdims=True))
        decay = jnp.exp(g_tot + m - m_new)
        w_tok = jnp.exp(a_col - m_new)
        kw_t = (kc.astype(F32) * w_tok).T.astype(BF16)
        c_ref[...] = decay * c_ref[...] + jnp.dot(kw_t, v_ext, preferred_element_type=F32)
        hn = _rms(h, nw_ref[...]) * jax.nn.sigmoid(og_ref[pl.ds(r0, L), :].astype(F32))
        o_ref[pl.ds(r0, L), :] = hn.astype(o_ref.dtype)
        return m_new

    lax.fori_loop(0, nc, body, jnp.zeros((1, 1), F32))


def _mlstm(gate_bias, z, gz, norm_w, *, batch, seq, q_col, k_col, v_col, o_col, gate_lane):
    H = ML_HEADS
    dqk = ML_QK_DIM
    dv = ML_V_DIM
    L = min(ML_CHUNK_LEN, seq)
    assert L == LANES and gate_lane + 2 * H <= LANES
    return pl.pallas_call(
        functools.partial(_mlstm_kernel, seq=seq, L=L, gate_lane=gate_lane),
        grid=(batch, H),
        in_specs=[
            pl.BlockSpec(memory_space=pltpu.SMEM),
            pl.BlockSpec((seq, dqk), lambda b, h: (b, q_col // dqk + h)),
            pl.BlockSpec((seq, dqk), lambda b, h: (b, k_col // dqk + h)),
            pl.BlockSpec((seq, dv), lambda b, h: (b, v_col // dv + h)),
            pl.BlockSpec((seq, dv), lambda b, h: (b, o_col // dv + h)),
            pl.BlockSpec((seq, LANES), lambda b, h: (b, 0)),
            pl.BlockSpec((None, 1, dv), lambda b, h: (h, 0, 0)),
        ],
        out_specs=pl.BlockSpec((seq, dv), lambda b, h: (b, h)),
        out_shape=jax.ShapeDtypeStruct((batch * seq, H * dv), BF16),
        scratch_shapes=[pltpu.VMEM((dqk, dv + LANES), F32)],
        compiler_params=_params("parallel", "parallel"),
        name="mlstm",
    )(gate_bias, z, z, z, z, gz, norm_w.reshape(H, 1, dv))


def _peer_pairs():
    return [(a, (PEER_TOPK + 1) // (a + 1)) for a in range(PEER_TOPK + 1)]


def _topk_multiset(s, k):
    tt = s.shape[1]
    riota = lax.broadcasted_iota(jnp.int32, (k, tt), 0).astype(F32)
    v = jnp.full((k, tt), -jnp.inf, F32)
    taken = jnp.zeros((1, tt), F32)
    rem = s
    for _ in range(k):
        m = jnp.max(rem, axis=0, keepdims=True)
        eq = rem == m
        cnt = jnp.sum(_ind(eq, F32), axis=0, keepdims=True)
        v = jnp.where((riota >= taken) & (riota < taken + cnt), m, v)
        taken = taken + cnt
        rem = jnp.where(eq, -jnp.inf, rem)
    return v


def _peer_topk_kernel(q_ref, khi_ref, klo_ref, r1_ref, s2_ref, e1_ref, e2_ref, cand_ref):
    H = PEER_HEADS
    half = PEER_KEY_DIM // 2
    K = PEER_TOPK
    pairs = _peer_pairs()
    n_cand = sum(nb for _, nb in pairs)
    tt = q_ref.shape[0]
    for h in range(H):
        s_parts = []
        for side in range(2):
            qf = q_ref[:, (2 * h + side) * half:(2 * h + side + 1) * half]
            q_hi = qf.astype(BF16)
            q_lo = (qf - q_hi.astype(F32)).astype(BF16)
            k_hi = khi_ref[side, h]
            k_lo = klo_ref[side, h]
            s = (lax.dot_general(k_hi, q_hi, _NT, preferred_element_type=F32)
                 + lax.dot_general(k_hi, q_lo, _NT, preferred_element_type=F32)
                 + lax.dot_general(k_lo, q_hi, _NT, preferred_element_type=F32))
            s_parts.append(s)
        s1, s2 = s_parts
        v1 = _topk_multiset(s1, K + 1)
        v2 = _topk_multiset(s2, K + 1)
        off = 0
        for a, nb in pairs:
            cand_ref[off:off + nb, :] = v1[a:a + 1, :] + v2[0:nb, :]
            off += nb
        cand_ref[n_cand:, :] = jnp.full((cand_ref.shape[0] - n_cand, tt), -jnp.inf, F32)
        rem = cand_ref[...]
        taken = jnp.zeros((1, tt), F32)
        tau16 = jnp.zeros((1, tt), F32)
        tau17 = jnp.zeros((1, tt), F32)
        zsum = jnp.zeros((1, tt), F32)
        cmax = v1[0:1, :] + v2[0:1, :]
        for _ in range(K + 1):
            m = jnp.max(rem, axis=0, keepdims=True)
            eq = rem == m
            cnt = jnp.sum(_ind(eq, F32), axis=0, keepdims=True)
            take = jnp.minimum(cnt, jnp.maximum(K - taken, 0.0))
            zsum = zsum + take * jnp.exp(m - cmax)
            after = taken + cnt
            tau16 = jnp.where((taken < K) & (after >= K), m, tau16)
            tau17 = jnp.where((taken <= K) & (after > K), m, tau17)
            taken = after
            rem = jnp.where(eq, -jnp.inf, rem)
        theta = 0.5 * (tau16 + tau17)
        r1_ref[h] = theta - s1
        s2_ref[h] = s2
        e1_ref[h] = jnp.exp(s1 - v1[0:1, :])
        e2_ref[h] = jnp.exp(s2 - v2[0:1, :]) / zsum


def _peer_topk(pq, k_hi, k_lo):
    n = pq.shape[0]
    H = PEER_HEADS
    nk = PEER_N_KEYS
    tt = min(PEER_TOPK_TT, n)
    n_cand = sum(nb for _, nb in _peer_pairs())
    cand_rows = -(-n_cand // SUBLANES) * SUBLANES
    key_spec = pl.BlockSpec((2, H, nk, PEER_KEY_DIM // 2), lambda i: (0, 0, 0, 0))
    hkt = pl.BlockSpec((H, nk, tt), lambda i: (0, 0, i))
    hkn = jax.ShapeDtypeStruct((H, nk, n), F32)
    return pl.pallas_call(
        _peer_topk_kernel,
        grid=(n // tt,),
        in_specs=[pl.BlockSpec((tt, pq.shape[1]), lambda i: (i, 0)), key_spec, key_spec],
        out_specs=[hkt, hkt, hkt, hkt],
        out_shape=[hkn, hkn, hkn, hkn],
        scratch_shapes=[pltpu.VMEM((cand_rows, tt), F32)],
        compiler_params=_params("parallel"),
        name="peer_topk",
    )(pq, k_hi, k_lo)


def _peer_mix_kernel(act_ref, vtp_ref, vtc_ref, r1_ref, e1_ref, s2_ref, e2_ref, o_ref, acc_ref, a0_ref, a1_ref,
                     a2_ref):
    H = PEER_HEADS
    nk = PEER_N_KEYS
    te, tt = a0_ref.shape
    n_i1 = te // nk
    s = pl.program_id(1)
    last = pl.num_programs(1) - 1
    live = jnp.where(s < last, 1.0, 0.0)
    pack = 2 * SUBLANES

    @pl.when(s == 0)
    def _():
        acc_ref[...] = jnp.zeros(acc_ref.shape, F32)
        a1_ref[...] = jnp.zeros(a1_ref.shape, BF16)

    def build(dst, tile):
        for i in range(n_i1):
            row = tile * n_i1 + i
            r1_rows = [r1_ref[h, row:row + 1, :] for h in range(H)]
            e1_rows = [e1_ref[h, row:row + 1, :] * live for h in range(H)]
            for lt in range(tt // LANES):
                ls = slice(lt * LANES, (lt + 1) * LANES)
                r1b = [jnp.broadcast_to(r1_rows[h][:, ls], (SUBLANES, LANES)) for h in range(H)]
                e1b = [jnp.broadcast_to(e1_rows[h][:, ls], (SUBLANES, LANES)) for h in range(H)]
                for sg in range(nk // pack):
                    halves = []
                    for half in range(2):
                        r0 = sg * pack + half * SUBLANES
                        w = jnp.zeros((SUBLANES, LANES), F32)
                        for h in range(H):
                            hit = s2_ref[h, r0:r0 + SUBLANES, ls] >= r1b[h]
                            w = w + e1b[h] * jnp.where(hit, e2_ref[h, r0:r0 + SUBLANES, ls], 0.0)
                        halves.append(w)
                    rr = slice(i * nk + sg * pack, i * nk + (sg + 1) * pack)
                    src = act_ref[tile * te + rr.start:tile * te + rr.stop, ls].astype(F32)
                    dst[rr, ls] = (src * jnp.concatenate(halves, axis=0)).astype(BF16)

    def step(carried, carry_out):
        acc_ref[...] += jnp.dot(vtp_ref[...], carried[...], preferred_element_type=F32)
        build(a0_ref, 0)
        acc_ref[...] += jnp.dot(vtc_ref[...], a0_ref[...], preferred_element_type=F32)
        build(carry_out, 1)

    pl.when(s % 2 == 0)(functools.partial(step, a1_ref, a2_ref))
    pl.when(s % 2 == 1)(functools.partial(step, a2_ref, a1_ref))

    @pl.when(s == last)
    def _():
        o_ref[...] = acc_ref[...].T.astype(o_ref.dtype)


def _peer_mix(act_t, pvt, r1, s2, e1, e2):
    n_exp, n = act_t.shape
    d = pvt.shape[0]
    H = PEER_HEADS
    nk = PEER_N_KEYS
    tt = min(PEER_TT, n)
    te = PEER_TE
    assert 2 * te == SUBLANES * nk and n_exp % (2 * te) == 0
    n_pair = n_exp // (2 * te)

    def i1_spec():
        return pl.BlockSpec((H, SUBLANES, tt), lambda i, s: (0, jnp.minimum(s, n_pair - 1), i))

    def i2_spec():
        return pl.BlockSpec((H, nk, tt), lambda i, s: (0, 0, i))

    return pl.pallas_call(
        _peer_mix_kernel,
        grid=(n // tt, n_pair + 1),
        in_specs=[
            pl.BlockSpec((2 * te, tt), lambda i, s: (jnp.minimum(s, n_pair - 1), i)),
            pl.BlockSpec((d, te), lambda i, s: (0, jnp.maximum(2 * s - 1, 0))),
            pl.BlockSpec((d, te), lambda i, s: (0, jnp.minimum(2 * s, 2 * n_pair - 1))),
            i1_spec(), i1_spec(), i2_spec(), i2_spec(),
        ],
        out_specs=pl.BlockSpec((tt, d), lambda i, s: (i, 0)),
        out_shape=jax.ShapeDtypeStruct((n, d), BF16),
        scratch_shapes=[pltpu.VMEM((d, tt), F32)] + [pltpu.VMEM((te, tt), BF16)] * 3,
        compiler_params=_params("parallel", "arbitrary"),
        name="peer_mix",
    )(act_t, pvt, pvt, r1, e1, s2, e2)


def _cast_transpose_kernel(x_ref, o_ref):
    o_ref[...] = x_ref[...].T.astype(o_ref.dtype)


def _cast_transpose(x, dtype):
    r, c = x.shape
    tr = min(PEER_TE, r)
    return pl.pallas_call(
        _cast_transpose_kernel,
        grid=(r // tr,),
        in_specs=[pl.BlockSpec((tr, c), lambda i: (i, 0))],
        out_specs=pl.BlockSpec((c, tr), lambda i: (0, i)),
        out_shape=jax.ShapeDtypeStruct((c, r), dtype),
        compiler_params=_params("parallel"),
        name="cast_transpose",
    )(x)


def _layer(h, mod3, norm_pre_mix, norm_post_mix, norm_pre_ffn, norm_post_ffn, w_in, cmp_pos, w_cmp_k, w_cmp_v,
           gate_bias, ml_norm_w, w_up_nsa, w_up_mlstm, w_out, peer_w_q, sub_keys, peer_u, peer_v, *, batch, seq):
    d = h.shape[1]
    G = NSA_KV_GROUPS
    R = NSA_HEADS // G
    dk = NSA_HEAD_DIM
    H = ML_HEADS
    nsa_w = NSA_HEADS * dk
    kv_w = 6 * G * dk
    mq_w = H * ML_QK_DIM
    mv_w = H * ML_V_DIM
    c_kv = nsa_w
    c_gn = c_kv + kv_w
    c_mq = c_gn + 3 * NSA_HEADS
    c_mk = c_mq + mq_w
    c_mv = c_mk + mq_w
    c_mo = c_mv + mv_w
    c_if = c_mo + mv_w
    c_mg = c_if + 2 * H
    n_gate = 3 * NSA_HEADS + 2 * H
    w_gate = jnp.concatenate([w_in[:, c_gn:c_mq], w_in[:, c_if:c_mg],
                              jnp.zeros((d, LANES - n_gate), F32)], axis=1).astype(BF16)
    z_kv = nsa_w
    z_mq = z_kv + kv_w
    z_mk = z_mq + mq_w
    z_mv = z_mk + mq_w
    z_mo = z_mv + mv_w
    z_mg = z_mo + mv_w
    z_w = z_mg + 2 * d
    assert z_mq % MMW_BN == 0 and z_mg % MMW_BN == 0 and z_w % MMW_BN == 0
    z_shifts = [(0, z_mq // MMW_BN, 0), (z_mq // MMW_BN, z_mg // MMW_BN, 3 * NSA_HEADS),
                (z_mg // MMW_BN, z_w // MMW_BN, n_gate)]

    u = _prenorm(h, norm_pre_mix, mod3, seq)
    z = _matmul_w(u, w_in, BF16, n_out=z_w, shifts=z_shifts)
    gz = _matmul(u, w_gate, F32)

    slopes = jnp.exp2(-8.0 * jnp.arange(1, NSA_HEADS + 1, dtype=F32) / NSA_HEADS) / (dk ** -0.5)
    slopes = jnp.stack([t.astype(F32) for t in _split3(slopes)], axis=-1).reshape(G, R, 3)
    o_nsa = _nsa(slopes, z, gz, w_cmp_k.astype(BF16), w_cmp_v.astype(BF16),
                 cmp_pos.reshape(2, 1, CMP_BLOCK * dk), batch=batch, seq=seq, kv_col=z_kv)

    h_ml = _mlstm(gate_bias, z, gz, ml_norm_w, batch=batch, seq=seq, q_col=z_mq, k_col=z_mk, v_col=z_mv,
                  o_col=z_mo, gate_lane=3 * NSA_HEADS)

    t1 = _matmul_w(o_nsa, w_up_nsa, BF16, gate=z, gate_col=z_mg)
    mix = _matmul_w(h_ml, w_up_mlstm, BF16, gate=z, gate_col=z_mg + d, add=t1)
    y = _matmul_w(mix, w_out, BF16)
    h1, u2, u2_t = _midnorm(h, y, norm_post_mix, norm_pre_ffn, mod3, seq)

    pq = _matmul_w(u2, peer_w_q, F32)
    k_hi = sub_keys.astype(BF16)
    k_lo = (sub_keys - k_hi.astype(F32)).astype(BF16)
    r1, s2, e1, e2 = _peer_topk(pq, k_hi, k_lo)
    act_t = _matmul_a(peer_u, u2_t, BF16, gelu=True)
    y2 = _peer_mix(act_t, _cast_transpose(peer_v, BF16), r1, s2, e1, e2)
    return _finalnorm(h1, y2, norm_post_ffn, mod3, seq)


def kernel(x, c, w_ada, b_ada, norm_pre_mix, norm_post_mix, norm_pre_ffn, norm_post_ffn, w_in, nsa_cmp_pos, nsa_w_cmp_k, nsa_w_cmp_v, mlstm_gate_bias, mlstm_norm_w, w_up_nsa, w_up_mlstm, w_out, peer_w_q, peer_sub_keys, peer_u, peer_v):
    batch, seq, d = x.shape
    depth = w_ada.shape[0]
    h = x.reshape(batch * seq, d)
    c_pad = jnp.concatenate([c, jnp.zeros((SUBLANES - batch % SUBLANES, d), c.dtype)], axis=0)
    for l in range(depth):
        mod = _adaln(c_pad, w_ada[l], b_ada[l])[:batch]
        mod3 = mod.reshape(batch, 6, d)
        h = _layer(h, mod3, norm_pre_mix[l], norm_post_mix[l], norm_pre_ffn[l], norm_post_ffn[l], w_in[l],
                   nsa_cmp_pos[l], nsa_w_cmp_k[l], nsa_w_cmp_v[l], mlstm_gate_bias[l], mlstm_norm_w[l],
                   w_up_nsa[l], w_up_mlstm[l], w_out[l], peer_w_q[l], peer_sub_keys[l], peer_u[l], peer_v[l],
                   batch=batch, seq=seq)
    return h.reshape(batch, seq, d).astype(x.dtype)
```

```python
import functools
import math

import jax
import jax.numpy as jnp
from jax import lax
from jax.experimental import pallas as pl
from jax.experimental.pallas import tpu as pltpu

D_MODEL = 4096
BATCH = 4
SEQ = 2048
NSA_HEADS = 16
NSA_KV_GROUPS = 4
NSA_HEAD_DIM = 128
CMP_BLOCK = 32
CMP_STRIDE = 16
SLC_BLOCK = 64
SLC_TOPN = 16
WINDOW = 512
ML_HEADS = 8
ML_QK_DIM = 256
ML_V_DIM = 512
PEER_HEADS = 8
PEER_KEY_DIM = 256
PEER_N_KEYS = 128
PEER_TOPK = 16

NEG_INF = -1e30
BIG = 1e9
TINY = 1e-30
EPS = 1e-6

F32 = jnp.float32
BF16 = jnp.bfloat16

V7X_VMEM_LIMIT_BYTES = 56 * 1024 * 1024
LANES = 128
SUBLANES = 8

ROW_TILE = 256
MM_BM = 1024
MM_BN = 1024
MMW_BN = 512
MMA_BM = 512
ADA_BN = 512
NSA_TQ = 128
NSA_TK = 512
ML_CHUNK_LEN = 128
PEER_TOPK_TT = 256
PEER_TT = 512
PEER_TE = 1024
PEER_DOT_CHUNK = 256

_NT = (((1,), (1,)), ((), ()))


def _params(*sem):
    return pltpu.CompilerParams(dimension_semantics=sem, vmem_limit_bytes=V7X_VMEM_LIMIT_BYTES)


def _ind(cond, dtype):
    wide = jnp.int32 if jnp.issubdtype(dtype, jnp.integer) else F32
    return jnp.where(cond, jnp.ones((), wide), jnp.zeros((), wide)).astype(dtype)


def _log2(n):
    k = int(math.log2(n))
    assert 1 << k == n
    return k


def _split3(x):
    hi = x.astype(BF16)
    r1 = x - hi.astype(F32)
    mid = r1.astype(BF16)
    lo = (r1 - mid.astype(F32)).astype(BF16)
    return hi, mid, lo


def _adaln_kernel(c_ref, w_ref, b_ref, o_ref):
    c = c_ref[...]
    cond = (c * jax.nn.sigmoid(c)).astype(BF16)
    o_ref[...] = jnp.dot(cond, w_ref[...].astype(BF16), preferred_element_type=F32) + b_ref[...]


def _adaln(c_pad, w_ada, b_ada):
    rows, d = c_pad.shape
    n = w_ada.shape[1]
    bn = min(ADA_BN, n)
    return pl.pallas_call(
        _adaln_kernel,
        grid=(n // bn,),
        in_specs=[
            pl.BlockSpec((rows, d), lambda j: (0, 0)),
            pl.BlockSpec((d, bn), lambda j: (0, j)),
            pl.BlockSpec((1, bn), lambda j: (0, j)),
        ],
        out_specs=pl.BlockSpec((rows, bn), lambda j: (0, j)),
        out_shape=jax.ShapeDtypeStruct((rows, n), F32),
        compiler_params=_params("parallel"),
        name="adaln",
    )(c_pad, w_ada, b_ada.reshape(1, n))


def _rms(x, w):
    return x * lax.rsqrt(jnp.mean(x * x, axis=-1, keepdims=True) + EPS) * w


def _prenorm_kernel(x_ref, w_ref, mod_ref, u_ref):
    y = _rms(x_ref[...], w_ref[...])
    u_ref[...] = (y * (1.0 + mod_ref[1:2, :]) + mod_ref[0:1, :]).astype(u_ref.dtype)


def _prenorm(x2, w, mod3, seq):
    n, d = x2.shape
    tr = min(ROW_TILE, seq)
    return pl.pallas_call(
        _prenorm_kernel,
        grid=(n // tr,),
        in_specs=[
            pl.BlockSpec((tr, d), lambda i: (i, 0)),
            pl.BlockSpec((1, d), lambda i: (0, 0)),
            pl.BlockSpec((None, 6, d), lambda i: ((i * tr) // seq, 0, 0)),
        ],
        out_specs=pl.BlockSpec((tr, d), lambda i: (i, 0)),
        out_shape=jax.ShapeDtypeStruct((n, d), BF16),
        compiler_params=_params("parallel"),
        name="prenorm",
    )(x2, w.reshape(1, d), mod3)


def _midnorm_kernel(x_ref, y_ref, w1_ref, w2_ref, mod_ref, h_ref, u_ref, ut_ref):
    h = x_ref[...] + mod_ref[2:3, :] * _rms(y_ref[...].astype(F32), w1_ref[...])
    h_ref[...] = h
    u = _rms(h, w2_ref[...]) * (1.0 + mod_ref[4:5, :]) + mod_ref[3:4, :]
    u_ref[...] = u.astype(u_ref.dtype)
    ut_ref[...] = u.T.astype(ut_ref.dtype)


def _midnorm(x2, y, w_post, w_pre, mod3, seq):
    n, d = x2.shape
    tr = min(ROW_TILE, seq)
    row = pl.BlockSpec((tr, d), lambda i: (i, 0))
    vec = pl.BlockSpec((1, d), lambda i: (0, 0))
    return pl.pallas_call(
        _midnorm_kernel,
        grid=(n // tr,),
        in_specs=[row, row, vec, vec, pl.BlockSpec((None, 6, d), lambda i: ((i * tr) // seq, 0, 0))],
        out_specs=[row, row, pl.BlockSpec((d, tr), lambda i: (0, i))],
        out_shape=[jax.ShapeDtypeStruct((n, d), F32), jax.ShapeDtypeStruct((n, d), BF16),
                   jax.ShapeDtypeStruct((d, n), BF16)],
        compiler_params=_params("parallel"),
        name="midnorm",
    )(x2, y, w_post.reshape(1, d), w_pre.reshape(1, d), mod3)


def _finalnorm_kernel(h_ref, y_ref, w_ref, mod_ref, o_ref):
    o_ref[...] = h_ref[...] + mod_ref[5:6, :] * _rms(y_ref[...].astype(F32), w_ref[...])


def _finalnorm(h1, y, w, mod3, seq):
    n, d = h1.shape
    tr = min(ROW_TILE, seq)
    row = pl.BlockSpec((tr, d), lambda i: (i, 0))
    return pl.pallas_call(
        _finalnorm_kernel,
        grid=(n // tr,),
        in_specs=[row, row, pl.BlockSpec((1, d), lambda i: (0, 0)),
                  pl.BlockSpec((None, 6, d), lambda i: ((i * tr) // seq, 0, 0))],
        out_specs=row,
        out_shape=jax.ShapeDtypeStruct((n, d), F32),
        compiler_params=_params("parallel"),
        name="finalnorm",
    )(h1, y, w.reshape(1, d), mod3)


def _gelu(x):
    return 0.5 * x * (1.0 + lax.erf(x * (2.0 ** -0.5)))


def _mm_kernel(*refs, has_gate, has_add, gelu):
    a_ref, b_ref = refs[0], refs[1]
    o_ref = refs[-1]
    acc = jnp.dot(a_ref[...], b_ref[...], preferred_element_type=F32)
    nxt = 2
    if has_gate:
        acc = acc * jax.nn.sigmoid(refs[nxt][...].astype(F32))
        nxt += 1
    if has_add:
        acc = acc + refs[nxt][...].astype(F32)
    if gelu:
        acc = _gelu(acc)
    o_ref[...] = acc.astype(o_ref.dtype)


def _matmul(a, b, out_dtype, gate=None, gate_col=0, add=None, gelu=False):
    m, k = a.shape
    n = b.shape[1]
    bm = min(MM_BM, m)
    bn = min(MM_BN, n)
    in_specs = [pl.BlockSpec((bm, k), lambda i, j: (i, 0)), pl.BlockSpec((k, bn), lambda i, j: (0, j))]
    args = [a, b]
    if gate is not None:
        goff = gate_col // bn
        assert goff * bn == gate_col
        in_specs.append(pl.BlockSpec((bm, bn), lambda i, j: (i, goff + j)))
        args.append(gate)
    if add is not None:
        in_specs.append(pl.BlockSpec((bm, bn), lambda i, j: (i, j)))
        args.append(add)
    return pl.pallas_call(
        functools.partial(_mm_kernel, has_gate=gate is not None, has_add=add is not None, gelu=gelu),
        grid=(m // bm, n // bn),
        in_specs=in_specs,
        out_specs=pl.BlockSpec((bm, bn), lambda i, j: (i, j)),
        out_shape=jax.ShapeDtypeStruct((m, n), out_dtype),
        compiler_params=_params("parallel", "arbitrary"),
        name="matmul",
    )(*args)


def _mmw_kernel(*refs, shifts, has_gate, has_add, row_chunk):
    a_ref, w_ref = refs[0], refs[1]
    nxt = 2
    wn_ref = None
    if shifts is not None:
        wn_ref = refs[nxt]
        nxt += 1
    o_ref, wb_ref = refs[-2], refs[-1]
    j = pl.program_id(0)
    k, bn = wb_ref.shape

    def fill(shift):
        for r0 in range(0, k, row_chunk):
            rs = slice(r0, r0 + row_chunk)
            if shift == 0:
                wb_ref[rs, :] = w_ref[rs, :].astype(BF16)
            else:
                wide = jnp.concatenate([w_ref[rs, :], wn_ref[rs, :]], axis=1)
                wb_ref[rs, :] = wide[:, shift:shift + bn].astype(BF16)

    @pl.when(pl.program_id(1) == 0)
    def _():
        if shifts is None:
            fill(0)
        else:
            for j0, j1, shift in shifts:
                pl.when((j >= j0) & (j < j1))(functools.partial(fill, shift))

    acc = jnp.dot(a_ref[...], wb_ref[...], preferred_element_type=F32)
    if has_gate:
        acc = acc * jax.nn.sigmoid(refs[nxt][...].astype(F32))
        nxt += 1
    if has_add:
        acc = acc + refs[nxt][...].astype(F32)
    o_ref[...] = acc.astype(o_ref.dtype)


def _matmul_w(a, w, out_dtype, *, n_out=None, shifts=None, gate=None, gate_col=0, add=None):
    m, k = a.shape
    n = w.shape[1] if n_out is None else n_out
    bm = min(MM_BM, m)
    bn = min(MMW_BN, n)
    assert n % bn == 0 and m % bm == 0
    in_specs = [pl.BlockSpec((bm, k), lambda j, i: (i, 0)), pl.BlockSpec((k, bn), lambda j, i: (0, j))]
    args = [a, w]
    if shifts is not None:
        per = bn // LANES
        in_specs.append(pl.BlockSpec((k, LANES), lambda j, i: (0, (j + 1) * per)))
        args.append(w)
    if gate is not None:
        goff = gate_col // bn
        assert goff * bn == gate_col
        in_specs.append(pl.BlockSpec((bm, bn), lambda j, i: (i, goff + j)))
        args.append(gate)
    if add is not None:
        in_specs.append(pl.BlockSpec((bm, bn), lambda j, i: (i, j)))
        args.append(add)
    return pl.pallas_call(
        functools.partial(_mmw_kernel, shifts=shifts, has_gate=gate is not None, has_add=add is not None,
                          row_chunk=min(512, k)),
        grid=(n // bn, m // bm),
        in_specs=in_specs,
        out_specs=pl.BlockSpec((bm, bn), lambda j, i: (i, j)),
        out_shape=jax.ShapeDtypeStruct((m, n), out_dtype),
        scratch_shapes=[pltpu.VMEM((k, bn), BF16)],
        compiler_params=_params("parallel", "arbitrary"),
        name="matmul_w",
    )(*args)


def _inproj_kernel(a_ref, wt_ref, wtn_ref, o_ref, wb_ref, *, shifts, col_chunk):
    j = pl.program_id(0)
    bn, k = wb_ref.shape

    def fill(shift):
        for c0 in range(0, k, col_chunk):
            cs = slice(c0, c0 + col_chunk)
            if shift == 0:
                wb_ref[:, cs] = wt_ref[:, cs].astype(BF16)
            else:
                tall = jnp.concatenate([wt_ref[:, cs], wtn_ref[:, cs]], axis=0)
                wb_ref[:, cs] = tall[shift:shift + bn, :].astype(BF16)

    @pl.when(pl.program_id(1) == 0)
    def _():
        for j0, j1, shift in shifts:
            pl.when((j >= j0) & (j < j1))(functools.partial(fill, shift))

    acc = lax.dot_general(a_ref[...], wb_ref[...], _NT, preferred_element_type=F32)
    o_ref[...] = acc.astype(o_ref.dtype)


def _inproj(a, wt, n_out, shifts):
    m, k = a.shape
    bm = min(MM_BM, m)
    bn = MMW_BN
    assert n_out % bn == 0 and m % bm == 0 and all(s % SUBLANES == 0 and s <= LANES for _, _, s in shifts)
    per = bn // LANES
    return pl.pallas_call(
        functools.partial(_inproj_kernel, shifts=shifts, col_chunk=min(512, k)),
        grid=(n_out // bn, m // bm),
        in_specs=[pl.BlockSpec((bm, k), lambda j, i: (i, 0)),
                  pl.BlockSpec((bn, k), lambda j, i: (j, 0)),
                  pl.BlockSpec((LANES, k), lambda j, i: ((j + 1) * per, 0))],
        out_specs=pl.BlockSpec((bm, bn), lambda j, i: (i, j)),
        out_shape=jax.ShapeDtypeStruct((m, n_out), BF16),
        scratch_shapes=[pltpu.VMEM((bn, k), BF16)],
        compiler_params=_params("parallel", "arbitrary"),
        name="inproj",
    )(a, wt, wt)


def _gate_proj_kernel(a_ref, wa_ref, wb_ref, o_ref, *, n_a, off_b, n_b):
    k = wa_ref.shape[1]
    rows = jnp.concatenate([wa_ref[0:n_a, :], wb_ref[off_b:off_b + n_b, :],
                            jnp.zeros((LANES - n_a - n_b, k), F32)], axis=0).astype(BF16)
    o_ref[...] = lax.dot_general(a_ref[...], rows, _NT, preferred_element_type=F32)


def _gate_proj(a, wt, row_a, n_a, row_b, n_b):
    m, k = a.shape
    bm = min(MM_BM, m)
    off_b = row_b % LANES
    assert row_a % LANES == 0 and n_a % SUBLANES == 0 and off_b % SUBLANES == 0 and n_b % SUBLANES == 0
    assert off_b + n_b <= LANES and n_a + n_b <= LANES
    return pl.pallas_call(
        functools.partial(_gate_proj_kernel, n_a=n_a, off_b=off_b, n_b=n_b),
        grid=(m // bm,),
        in_specs=[pl.BlockSpec((bm, k), lambda i: (i, 0)),
                  pl.BlockSpec((LANES, k), lambda i: (row_a // LANES, 0)),
                  pl.BlockSpec((LANES, k), lambda i: (row_b // LANES, 0))],
        out_specs=pl.BlockSpec((bm, LANES), lambda i: (i, 0)),
        out_shape=jax.ShapeDtypeStruct((m, LANES), F32),
        compiler_params=_params("parallel"),
        name="gate_proj",
    )(a, wt, wt)


def _mma_kernel(a_ref, b_ref, o_ref, ab_ref, *, gelu, row_chunk):
    @pl.when(pl.program_id(1) == 0)
    def _():
        for r0 in range(0, ab_ref.shape[0], row_chunk):
            ab_ref[r0:r0 + row_chunk, :] = a_ref[r0:r0 + row_chunk, :].astype(BF16)

    acc = jnp.dot(ab_ref[...], b_ref[...], preferred_element_type=F32)
    if gelu:
        acc = _gelu(acc)
    o_ref[...] = acc.astype(o_ref.dtype)


def _matmul_a(a, b, out_dtype, gelu=False):
    m, k = a.shape
    n = b.shape[1]
    bm = min(MMA_BM, m)
    bn = min(MM_BN, n)
    return pl.pallas_call(
        functools.partial(_mma_kernel, gelu=gelu, row_chunk=min(128, bm)),
        grid=(m // bm, n // bn),
        in_specs=[pl.BlockSpec((bm, k), lambda i, j: (i, 0)), pl.BlockSpec((k, bn), lambda i, j: (0, j))],
        out_specs=pl.BlockSpec((bm, bn), lambda i, j: (i, j)),
        out_shape=jax.ShapeDtypeStruct((m, n), out_dtype),
        scratch_shapes=[pltpu.VMEM((bm, k), BF16)],
        compiler_params=_params("parallel", "arbitrary"),
        name="matmul_a",
    )(a, b)


def _pos_columns(pos, width):
    lane = lax.broadcasted_iota(jnp.int32, (pos.shape[0], width), 1)
    coarse = (pos >> 6) << 6
    vals = jnp.where(lane < 3, coarse, jnp.where(lane < 6, pos & 63, 0))
    return vals.astype(F32).astype(BF16)


def _nsa_kernel(slopes_ref, q_ref, kcm_ref, vcm_ref, ks_ref, vs_ref, kw_ref, vw_ref, gz_ref, wck_ref, wcv_ref,
                pos_ref, o_ref, kc_ref, vc_ref, pext_ref, xs_ref, acc_ref, qext_ref, *, seq, tq, tk,
                heads_per_group):
    R = heads_per_group
    dk = NSA_HEAD_DIM
    g = pl.program_id(1)
    qi = pl.program_id(2)
    t0 = qi * tq
    n_cmp = (seq - CMP_BLOCK) // CMP_STRIDE + 1
    n_slc = seq // SLC_BLOCK
    n_sel = min(SLC_TOPN, n_slc)
    half = CMP_STRIDE * dk
    scale = dk ** -0.5
    c_exp = scale * math.log2(math.e)
    rows = R * tq

    @pl.when(qi == 0)
    def _():
        pext_ref[...] = _pos_columns(lax.broadcasted_iota(jnp.int32, (seq, 1), 0), LANES)
        cpos = lax.broadcasted_iota(jnp.int32, (LANES, 1), 0) * CMP_STRIDE + (CMP_BLOCK - 1)
        for src_ref, w_ref, dst, pi in ((kcm_ref, wck_ref, kc_ref, 0), (vcm_ref, wcv_ref, vc_ref, 1)):
            xs_ref[...] = src_ref[...].astype(F32)
            x = jnp.concatenate([xs_ref[pl.ds(l, LANES, stride=CMP_STRIDE), :].astype(BF16)
                                 for l in range(CMP_STRIDE)], axis=1)
            ya = jnp.dot(x, w_ref[0:half, :], preferred_element_type=F32)
            yb = jnp.dot(x, w_ref[half:2 * half, :], preferred_element_type=F32)
            pos = jnp.broadcast_to(pos_ref[pi], (SUBLANES, 2 * half)).astype(BF16)
            pb = jnp.dot(pos, w_ref[...], preferred_element_type=F32)[0:1, :]
            comp = (ya + pltpu.roll(yb, LANES - 1, 0) + pb).astype(BF16)
            dst[...] = jnp.concatenate([comp, _pos_columns(cpos, LANES)], axis=1) if pi == 0 else comp

        head_of_row = lax.broadcasted_iota(jnp.int32, (rows, LANES), 0) >> _log2(tq)
        lane6 = lax.broadcasted_iota(jnp.int32, (rows, LANES), 1)
        term = jnp.where(lane6 < 3, lane6, lane6 - 3)
        q_ext = jnp.zeros((rows, LANES), F32)
        for r in range(R):
            for k in range(3):
                q_ext = jnp.where((head_of_row == r) & (term == k) & (lane6 < 6), slopes_ref[g, r, k], q_ext)
        qext_ref[...] = q_ext.astype(BF16)

    row_id = lax.broadcasted_iota(jnp.int32, (rows, 1), 0)
    t_col = t0 + (row_id & (tq - 1))
    q = q_ref[...]
    qs = jnp.concatenate([q[:, r * dk:(r + 1) * dk] for r in range(R)], axis=0)
    qa = jnp.concatenate([qs, qext_ref[...]], axis=1)

    n_idx = lax.broadcasted_iota(jnp.int32, (rows, LANES), 1)
    cvalid = (t_col >= n_idx * CMP_STRIDE + CMP_BLOCK - 1) & (n_idx < n_cmp)
    sc = lax.dot_general(qa, kc_ref[...], _NT, preferred_element_type=F32)
    sc = jnp.where(cvalid, sc, NEG_INF)
    mc = jnp.max(sc, axis=-1, keepdims=True)
    p_cmp = jnp.where(cvalid, jnp.exp2((sc - mc) * c_exp), 0.0)
    p_cmp = p_cmp * (1.0 / jnp.maximum(jnp.sum(p_cmp, axis=-1, keepdims=True), TINY))
    o_cmp = jnp.dot(p_cmp.astype(BF16), vc_ref[...], preferred_element_type=F32)

    psum = p_cmp[0:tq]
    for r in range(1, R):
        psum = psum + p_cmp[r * tq:(r + 1) * tq]
    jj = lax.broadcasted_iota(jnp.int32, (LANES, LANES), 0)
    nn = lax.broadcasted_iota(jnp.int32, (LANES, LANES), 1)
    ovl = _ind((nn * CMP_STRIDE < jj * SLC_BLOCK + SLC_BLOCK) & (nn * CMP_STRIDE + CMP_BLOCK > jj * SLC_BLOCK)
               & (jj < n_slc) & (nn < n_cmp), BF16)
    imp_t = jnp.zeros((LANES, tq), F32)
    for part in _split3(psum):
        imp_t = imp_t + lax.dot_general(ovl, part, _NT, preferred_element_type=F32)
    imp_t = imp_t[0:n_slc]
    j_i = lax.broadcasted_iota(jnp.int32, (n_slc, tq), 0)
    t_i = t0 + lax.broadcasted_iota(jnp.int32, (n_slc, tq), 1)
    cur = t_i >> _log2(SLC_BLOCK)
    causal = j_i * SLC_BLOCK <= t_i
    forced = (j_i == 0) | (j_i == cur) | (j_i == cur - 1)
    score = jnp.where(causal, jnp.where(forced, BIG, imp_t), -BIG)
    rank = jnp.zeros((n_slc, tq), jnp.int32)
    for jp in range(n_slc):
        row = score[jp:jp + 1, :]
        beats = (row > score) | ((row == score) & (j_i > jp))
        rank = rank + _ind(beats, jnp.int32)
    sel_t = jnp.where(rank < n_sel, 0.0, NEG_INF)
    sel_t = jnp.concatenate([sel_t, jnp.zeros((LANES - n_slc, tq), F32)], axis=0)
    sel_neg = sel_t.T.astype(BF16)
    tq_col = t0 + lax.broadcasted_iota(jnp.int32, (tq, 1), 0)

    def attend(s, v):
        p = jnp.exp2((s - jnp.max(s, axis=-1, keepdims=True)) * c_exp).astype(BF16)
        ones = _ind(lax.broadcasted_iota(jnp.int32, (v.shape[0], LANES), 1) == 0, BF16)
        pv = jnp.dot(p, jnp.concatenate([v, ones], axis=1), preferred_element_type=F32)
        return pv[:, 0:dk] / jnp.maximum(pv[:, dk:dk + 1], TINY)

    def slc_variant(nk):
        ka = jnp.concatenate([ks_ref[0:nk, :], pext_ref[0:nk, :]], axis=1)
        ej = lax.broadcasted_iota(jnp.int32, (LANES, nk), 0)
        ec = lax.broadcasted_iota(jnp.int32, (LANES, nk), 1)
        expand = _ind(ej == (ec >> _log2(SLC_BLOCK)), BF16)
        madd = jnp.dot(sel_neg, expand, preferred_element_type=F32)
        madd = madd + jnp.where(lax.broadcasted_iota(jnp.int32, (tq, nk), 1) <= tq_col, 0.0, NEG_INF)
        s = lax.dot_general(qa, ka, _NT, preferred_element_type=F32) + jnp.concatenate([madd] * R, axis=0)
        acc_ref[...] = attend(s, vs_ref[0:nk, :])

    variant = (t0 + tq - 1) // tk
    for vi in range(seq // tk):
        pl.when(variant == vi)(functools.partial(slc_variant, (vi + 1) * tk))
    o_slc = acc_ref[...]

    span = WINDOW + tq
    ws = pl.multiple_of(jnp.maximum(t0 - WINDOW, 0), tq)
    kwa = jnp.concatenate([kw_ref[pl.ds(ws, span), :], pext_ref[pl.ds(ws, span), :]], axis=1)
    dw_i = tq_col - (ws + lax.broadcasted_iota(jnp.int32, (tq, span), 1))
    wadd = jnp.where((dw_i >= 0) & (dw_i < WINDOW), 0.0, NEG_INF)
    sw = lax.dot_general(qa, kwa, _NT, preferred_element_type=F32) + jnp.concatenate([wadd] * R, axis=0)
    o_win = attend(sw, vw_ref[pl.ds(ws, span), :])

    gz = gz_ref[...]
    gsel = jnp.zeros((tq, 3 * R), F32)
    for gg in range(NSA_KV_GROUPS):
        gsel = jnp.where(g == gg, gz[:, 3 * R * gg:3 * R * (gg + 1)], gsel)
    gts = jax.nn.sigmoid(gsel)
    for r in range(R):
        sl = slice(r * tq, (r + 1) * tq)
        o = (gts[:, 3 * r:3 * r + 1] * o_cmp[sl] + gts[:, 3 * r + 1:3 * r + 2] * o_slc[sl]
             + gts[:, 3 * r + 2:3 * r + 3] * o_win[sl])
        o_ref[:, r * dk:(r + 1) * dk] = o.astype(o_ref.dtype)


def _nsa(slopes, z, gz, wck, wcv, pos, *, batch, seq, kv_col):
    G = NSA_KV_GROUPS
    R = NSA_HEADS // G
    dk = NSA_HEAD_DIM
    tq = NSA_TQ
    tk = NSA_TK
    nq = seq // tq
    assert seq // CMP_STRIDE == LANES and CMP_BLOCK == 2 * CMP_STRIDE and seq >= WINDOW + tq and seq % tk == 0
    kv0 = kv_col // dk

    def kv_spec(i):
        return pl.BlockSpec((seq, dk), lambda b, g, qi: (b, kv0 + i * G + g))

    w_spec = pl.BlockSpec((CMP_BLOCK * dk, dk), lambda b, g, qi: (0, 0))
    rows = R * tq
    assert dk == LANES and 3 * NSA_HEADS <= LANES
    return pl.pallas_call(
        functools.partial(_nsa_kernel, seq=seq, tq=tq, tk=tk, heads_per_group=R),
        grid=(batch, G, nq),
        in_specs=[
            pl.BlockSpec(memory_space=pltpu.SMEM),
            pl.BlockSpec((tq, R * dk), lambda b, g, qi: (b * nq + qi, g)),
            kv_spec(0), kv_spec(1), kv_spec(2), kv_spec(3), kv_spec(4), kv_spec(5),
            pl.BlockSpec((tq, LANES), lambda b, g, qi: (b * nq + qi, 0)),
            w_spec, w_spec,
            pl.BlockSpec((2, 1, CMP_BLOCK * dk), lambda b, g, qi: (0, 0, 0)),
        ],
        out_specs=pl.BlockSpec((tq, R * dk), lambda b, g, qi: (b * nq + qi, g)),
        out_shape=jax.ShapeDtypeStruct((batch * seq, NSA_HEADS * dk), BF16),
        scratch_shapes=[
            pltpu.VMEM((LANES, 2 * dk), BF16), pltpu.VMEM((LANES, dk), BF16),
            pltpu.VMEM((seq, LANES), BF16), pltpu.VMEM((seq, dk), F32), pltpu.VMEM((rows, dk), F32),
            pltpu.VMEM((rows, LANES), BF16),
        ],
        compiler_params=_params("parallel", "parallel", "arbitrary"),
        name="nsa",
    )(slopes, z, z, z, z, z, z, z, gz, wck, wcv, pos)


def _log_sigmoid(x):
    return jnp.minimum(x, 0.0) - jnp.log1p(jnp.exp(-jnp.abs(x)))


def _mlstm_kernel(bias_ref, q_ref, k_ref, v_ref, og_ref, gz_ref, nw_ref, o_ref, c_ref, *, seq, L, gate_lane):
    dqk = ML_QK_DIM
    dv = ML_V_DIM
    nc = seq // L
    h_id = pl.program_id(1)
    lane = lax.broadcasted_iota(jnp.int32, (L, LANES), 1)
    i_lane = gate_lane + h_id
    f_lane = i_lane + ML_HEADS

    def gate_forms(blk, which, bias):
        col = jnp.sum(jnp.where(lane == which, blk, 0.0), axis=1, keepdims=True) + bias
        return col, jnp.broadcast_to(col, (L, LANES)).T[0:1, :]

    ext = dv + LANES
    ii = lax.broadcasted_iota(jnp.int32, (L, L), 0)
    kk = lax.broadcasted_iota(jnp.int32, (L, L), 1)
    causal = kk <= ii
    tri = _ind(causal, BF16)
    tri_t = _ind(ii <= kk, BF16)
    ones_slab = _ind(lax.broadcasted_iota(jnp.int32, (L, LANES), 1) == 0, BF16)
    c_ref[...] = jnp.zeros((dqk, ext), F32)

    def cumsum_mats(lf_row, lf_col):
        lr = jnp.broadcast_to(lf_row, (L, L))
        lc = jnp.broadcast_to(lf_col, (L, L))
        b_row = jnp.zeros((L, L), F32)
        b_col = jnp.zeros((L, L), F32)
        for part in _split3(lr)[:2]:
            b_row = b_row + jnp.dot(part, tri_t, preferred_element_type=F32)
        for part in _split3(lc)[:2]:
            b_col = b_col + jnp.dot(tri, part, preferred_element_type=F32)
        return b_row, b_col

    def body(c, m):
        r0 = pl.multiple_of(c * L, L)
        qc = q_ref[pl.ds(r0, L), :]
        kc = k_ref[pl.ds(r0, L), :] * (dqk ** -0.5)
        v_ext = jnp.concatenate([v_ref[pl.ds(r0, L), :], ones_slab], axis=1)
        gblk = gz_ref[pl.ds(r0, L), :]
        i_col, i_row = gate_forms(gblk, i_lane, bias_ref[0, h_id])
        f_col, f_row = gate_forms(gblk, f_lane, bias_ref[1, h_id])
        lf_col = _log_sigmoid(f_col)
        lf_row = _log_sigmoid(f_row)
        b_row, b_col = cumsum_mats(lf_row, lf_col)
        log_d = jnp.where(causal, b_col - b_row + i_row, NEG_INF)
        b_c = b_col[:, 0:1]
        m_inter = b_c + m
        m_t = jnp.maximum(m_inter, jnp.max(log_d, axis=-1, keepdims=True))
        d = jnp.exp(log_d - m_t)
        s = lax.dot_general(qc, kc, _NT, preferred_element_type=F32) * d
        w_inter = jnp.exp(m_inter - m_t)
        inter = jnp.dot(qc, c_ref[...].astype(BF16), preferred_element_type=F32)
        intra = jnp.dot(s.astype(BF16), v_ext, preferred_element_type=F32)
        numden = w_inter * inter + intra
        den = numden[:, dv:dv + 1]
        h = numden[:, 0:dv] / jnp.maximum(jnp.abs(den), jnp.exp(-m_t))
        g_tot = b_row[0:1, L - 1:L]
        a_row = g_tot - b_row[0:1, :] + i_row
        a_col = g_tot - b_c + i_col
        m_new = jnp.maximum(g_tot + m, jnp.max(a_row, axis=-1, keepdims=True))
        decay = jnp.exp(g_tot + m - m_new)
        w_tok = jnp.exp(a_col - m_new)
        kw_t = (kc.astype(F32) * w_tok).T.astype(BF16)
        c_ref[...] = decay * c_ref[...] + jnp.dot(kw_t, v_ext, preferred_element_type=F32)
        hn = _rms(h, nw_ref[...]) * jax.nn.sigmoid(og_ref[pl.ds(r0, L), :].astype(F32))
        o_ref[pl.ds(r0, L), :] = hn.astype(o_ref.dtype)
        return m_new

    lax.fori_loop(0, nc, body, jnp.zeros((1, 1), F32))


def _mlstm(gate_bias, z, gz, norm_w, *, batch, seq, q_col, k_col, v_col, o_col, gate_lane):
    H = ML_HEADS
    dqk = ML_QK_DIM
    dv = ML_V_DIM
    L = min(ML_CHUNK_LEN, seq)
    assert L == LANES and gate_lane + 2 * H <= LANES
    return pl.pallas_call(
        functools.partial(_mlstm_kernel, seq=seq, L=L, gate_lane=gate_lane),
        grid=(batch, H),
        in_specs=[
            pl.BlockSpec(memory_space=pltpu.SMEM),
            pl.BlockSpec((seq, dqk), lambda b, h: (b, q_col // dqk + h)),
            pl.BlockSpec((seq, dqk), lambda b, h: (b, k_col // dqk + h)),
            pl.BlockSpec((seq, dv), lambda b, h: (b, v_col // dv + h)),
            pl.BlockSpec((seq, dv), lambda b, h: (b, o_col // dv + h)),
            pl.BlockSpec((seq, LANES), lambda b, h: (b, 0)),
            pl.BlockSpec((None, 1, dv), lambda b, h: (h, 0, 0)),
        ],
        out_specs=pl.BlockSpec((seq, dv), lambda b, h: (b, h)),
        out_shape=jax.ShapeDtypeStruct((batch * seq, H * dv), BF16),
        scratch_shapes=[pltpu.VMEM((dqk, dv + LANES), F32)],
        compiler_params=_params("parallel", "parallel"),
        name="mlstm",
    )(gate_bias, z, z, z, z, gz, norm_w.reshape(H, 1, dv))


def _peer_pairs():
    return [(a, (PEER_TOPK + 1) // (a + 1)) for a in range(PEER_TOPK + 1)]


def _topk_multiset(s, k, with_rank=False):
    tt = s.shape[1]
    riota = lax.broadcasted_iota(jnp.int32, (k, tt), 0).astype(F32)
    v = jnp.full((k, tt), -jnp.inf, F32)
    taken = jnp.zeros((1, tt), F32)
    rank = jnp.full(s.shape, float(s.shape[0]), F32)
    rem = s
    for _ in range(k):
        m = jnp.max(rem, axis=0, keepdims=True)
        eq = rem == m
        cnt = jnp.sum(_ind(eq, F32), axis=0, keepdims=True)
        v = jnp.where((riota >= taken) & (riota < taken + cnt), m, v)
        if with_rank:
            rank = jnp.where(eq, taken, rank)
        taken = taken + cnt
        rem = jnp.where(eq, -jnp.inf, rem)
    return (v, rank) if with_rank else v


def _peer_topk_kernel(q_ref, khi_ref, klo_ref, cnt_ref, rank_ref, e1_ref, e2_ref, cand_ref):
    H = PEER_HEADS
    half = PEER_KEY_DIM // 2
    K = PEER_TOPK
    pairs = _peer_pairs()
    n_cand = sum(nb for _, nb in pairs)
    tt = q_ref.shape[0]
    for h in range(H):
        s_parts = []
        for side in range(2):
            qf = q_ref[:, (2 * h + side) * half:(2 * h + side + 1) * half]
            q_hi = qf.astype(BF16)
            q_lo = (qf - q_hi.astype(F32)).astype(BF16)
            k_hi = khi_ref[side, h]
            k_lo = klo_ref[side, h]
            s = (lax.dot_general(k_hi, q_hi, _NT, preferred_element_type=F32)
                 + lax.dot_general(k_hi, q_lo, _NT, preferred_element_type=F32)
                 + lax.dot_general(k_lo, q_hi, _NT, preferred_element_type=F32))
            s_parts.append(s)
        s1, s2 = s_parts
        v1 = _topk_multiset(s1, K + 1)
        v2, rank2 = _topk_multiset(s2, K + 1, with_rank=True)
        off = 0
        for a, nb in pairs:
            cand_ref[off:off + nb, :] = v1[a:a + 1, :] + v2[0:nb, :]
            off += nb
        cand_ref[n_cand:, :] = jnp.full((cand_ref.shape[0] - n_cand, tt), -jnp.inf, F32)
        rem = cand_ref[...]
        taken = jnp.zeros((1, tt), F32)
        tau16 = jnp.zeros((1, tt), F32)
        tau17 = jnp.zeros((1, tt), F32)
        zsum = jnp.zeros((1, tt), F32)
        cmax = v1[0:1, :] + v2[0:1, :]
        for _ in range(K + 1):
            m = jnp.max(rem, axis=0, keepdims=True)
            eq = rem == m
            cnt = jnp.sum(_ind(eq, F32), axis=0, keepdims=True)
            take = jnp.minimum(cnt, jnp.maximum(K - taken, 0.0))
            zsum = zsum + take * jnp.exp(m - cmax)
            after = taken + cnt
            tau16 = jnp.where((taken < K) & (after >= K), m, tau16)
            tau17 = jnp.where((taken <= K) & (after > K), m, tau17)
            taken = after
            rem = jnp.where(eq, -jnp.inf, rem)
        theta = 0.5 * (tau16 + tau17)
        r1 = theta - s1
        cnt1 = jnp.zeros(r1.shape, F32)
        for b in range(K + 1):
            cnt1 = cnt1 + _ind(v2[b:b + 1, :] >= r1, F32)
        cnt_ref[h] = cnt1
        rank_ref[h] = rank2.astype(rank_ref.dtype)
        e1_ref[h] = jnp.exp(s1 - v1[0:1, :])
        e2_ref[h] = (jnp.exp(s2 - v2[0:1, :]) / zsum).astype(e2_ref.dtype)


def _peer_topk(pq, k_hi, k_lo):
    n = pq.shape[0]
    H = PEER_HEADS
    nk = PEER_N_KEYS
    tt = min(PEER_TOPK_TT, n)
    n_cand = sum(nb for _, nb in _peer_pairs())
    cand_rows = -(-n_cand // SUBLANES) * SUBLANES
    key_spec = pl.BlockSpec((2, H, nk, PEER_KEY_DIM // 2), lambda i: (0, 0, 0, 0))
    hkt = pl.BlockSpec((H, nk, tt), lambda i: (0, 0, i))
    wide = jax.ShapeDtypeStruct((H, nk, n), F32)
    narrow = jax.ShapeDtypeStruct((H, nk, n), BF16)
    return pl.pallas_call(
        _peer_topk_kernel,
        grid=(n // tt,),
        in_specs=[pl.BlockSpec((tt, pq.shape[1]), lambda i: (i, 0)), key_spec, key_spec],
        out_specs=[hkt, hkt, hkt, hkt],
        out_shape=[wide, narrow, wide, narrow],
        scratch_shapes=[pltpu.VMEM((cand_rows, tt), F32)],
        compiler_params=_params("parallel"),
        name="peer_topk",
    )(pq, k_hi, k_lo)


def _peer_mix_kernel(act_ref, vt_ref, cnt_ref, e1_ref, rank_ref, e2_ref, o_ref, acc_ref, a_ref, *, chunk):
    H = PEER_HEADS
    nk = PEER_N_KEYS
    te, tt = a_ref.shape
    e = pl.program_id(1)
    pack = 2 * SUBLANES

    @pl.when(e == 0)
    def _():
        acc_ref[...] = jnp.zeros(acc_ref.shape, F32)

    for i in range(te // nk):
        cnt_rows = [cnt_ref[h, i:i + 1, :] for h in range(H)]
        e1_rows = [e1_ref[h, i:i + 1, :] for h in range(H)]
        for lt in range(tt // LANES):
            ls = slice(lt * LANES, (lt + 1) * LANES)
            cntb = [jnp.broadcast_to(cnt_rows[h][:, ls], (pack, LANES)).astype(BF16) for h in range(H)]
            e1b = [jnp.broadcast_to(e1_rows[h][:, ls], (pack, LANES)).astype(BF16) for h in range(H)]
            for sg in range(nk // pack):
                rs = slice(sg * pack, (sg + 1) * pack)
                w = jnp.zeros((pack, LANES), BF16)
                for h in range(H):
                    hit = rank_ref[h, rs, ls] < cntb[h]
                    w = w + e1b[h] * jnp.where(hit, e2_ref[h, rs, ls], jnp.zeros((), BF16))
                rr = slice(i * nk + sg * pack, i * nk + (sg + 1) * pack)
                a_ref[rr, ls] = act_ref[rr, ls] * w
        done = (i + 1) * nk
        if done % chunk == 0:
            ks = slice(done - chunk, done)
            acc_ref[...] += jnp.dot(vt_ref[:, ks], a_ref[ks, :], preferred_element_type=F32)

    @pl.when(e == pl.num_programs(1) - 1)
    def _():
        o_ref[...] = acc_ref[...].T.astype(o_ref.dtype)


def _peer_mix(act_t, pvt, cnt1, rank2, e1, e2):
    n_exp, n = act_t.shape
    d = pvt.shape[0]
    H = PEER_HEADS
    nk = PEER_N_KEYS
    tt = min(PEER_TT, n)
    te = PEER_TE
    assert te == SUBLANES * nk and n_exp % te == 0

    def i1_spec():
        return pl.BlockSpec((H, SUBLANES, tt), lambda i, e: (0, e, i))

    def i2_spec():
        return pl.BlockSpec((H, nk, tt), lambda i, e: (0, 0, i))

    return pl.pallas_call(
        functools.partial(_peer_mix_kernel, chunk=PEER_DOT_CHUNK),
        grid=(n // tt, n_exp // te),
        in_specs=[
            pl.BlockSpec((te, tt), lambda i, e: (e, i)),
            pl.BlockSpec((d, te), lambda i, e: (0, e)),
            i1_spec(), i1_spec(), i2_spec(), i2_spec(),
        ],
        out_specs=pl.BlockSpec((tt, d), lambda i, e: (i, 0)),
        out_shape=jax.ShapeDtypeStruct((n, d), BF16),
        scratch_shapes=[pltpu.VMEM((d, tt), F32), pltpu.VMEM((te, tt), BF16)],
        compiler_params=_params("parallel", "arbitrary"),
        name="peer_mix",
    )(act_t, pvt, cnt1, e1, rank2, e2)


def _cast_transpose_kernel(x_ref, o_ref):
    o_ref[...] = x_ref[...].T.astype(o_ref.dtype)


def _cast_transpose(x, dtype):
    r, c = x.shape
    tr = min(MMA_BM, r)
    return pl.pallas_call(
        _cast_transpose_kernel,
        grid=(r // tr,),
        in_specs=[pl.BlockSpec((tr, c), lambda i: (i, 0))],
        out_specs=pl.BlockSpec((c, tr), lambda i: (0, i)),
        out_shape=jax.ShapeDtypeStruct((c, r), dtype),
        compiler_params=_params("parallel"),
        name="cast_transpose",
    )(x)


def _layer(h, mod3, norm_pre_mix, norm_post_mix, norm_pre_ffn, norm_post_ffn, w_in, cmp_pos, w_cmp_k, w_cmp_v,
           gate_bias, ml_norm_w, w_up_nsa, w_up_mlstm, w_out, peer_w_q, sub_keys, peer_u, peer_v, *, batch, seq):
    d = h.shape[1]
    G = NSA_KV_GROUPS
    R = NSA_HEADS // G
    dk = NSA_HEAD_DIM
    H = ML_HEADS
    nsa_w = NSA_HEADS * dk
    kv_w = 6 * G * dk
    mq_w = H * ML_QK_DIM
    mv_w = H * ML_V_DIM
    c_kv = nsa_w
    c_gn = c_kv + kv_w
    c_mq = c_gn + 3 * NSA_HEADS
    c_mk = c_mq + mq_w
    c_mv = c_mk + mq_w
    c_mo = c_mv + mv_w
    c_if = c_mo + mv_w
    c_mg = c_if + 2 * H
    n_gate = 3 * NSA_HEADS + 2 * H
    z_kv = nsa_w
    z_mq = z_kv + kv_w
    z_mk = z_mq + mq_w
    z_mv = z_mk + mq_w
    z_mo = z_mv + mv_w
    z_mg = z_mo + mv_w
    z_w = z_mg + 2 * d
    assert z_mq % MMW_BN == 0 and z_mg % MMW_BN == 0 and z_w % MMW_BN == 0
    z_shifts = [(0, z_mq // MMW_BN, 0), (z_mq // MMW_BN, z_mg // MMW_BN, 3 * NSA_HEADS),
                (z_mg // MMW_BN, z_w // MMW_BN, n_gate)]

    u = _prenorm(h, norm_pre_mix, mod3, seq)
    w_in_t = w_in.T
    z = _inproj(u, w_in_t, z_w, z_shifts)
    gz = _gate_proj(u, w_in_t, c_gn, 3 * NSA_HEADS, c_if, 2 * H)

    slopes = jnp.exp2(-8.0 * jnp.arange(1, NSA_HEADS + 1, dtype=F32) / NSA_HEADS) / (dk ** -0.5)
    slopes = jnp.stack([t.astype(F32) for t in _split3(slopes)], axis=-1).reshape(G, R, 3)
    o_nsa = _nsa(slopes, z, gz, w_cmp_k.astype(BF16), w_cmp_v.astype(BF16),
                 cmp_pos.reshape(2, 1, CMP_BLOCK * dk), batch=batch, seq=seq, kv_col=z_kv)

    h_ml = _mlstm(gate_bias, z, gz, ml_norm_w, batch=batch, seq=seq, q_col=z_mq, k_col=z_mk, v_col=z_mv,
                  o_col=z_mo, gate_lane=3 * NSA_HEADS)

    t1 = _matmul_w(o_nsa, w_up_nsa, BF16, gate=z, gate_col=z_mg)
    mix = _matmul_w(h_ml, w_up_mlstm, BF16, gate=z, gate_col=z_mg + d, add=t1)
    y = _matmul_w(mix, w_out, BF16)
    h1, u2, u2_t = _midnorm(h, y, norm_post_mix, norm_pre_ffn, mod3, seq)

    pq = _matmul_w(u2, peer_w_q, F32)
    k_hi = sub_keys.astype(BF16)
    k_lo = (sub_keys - k_hi.astype(F32)).astype(BF16)
    cnt1, rank2, e1, e2 = _peer_topk(pq, k_hi, k_lo)
    act_t = _matmul_a(peer_u, u2_t, BF16, gelu=True)
    y2 = _peer_mix(act_t, _cast_transpose(peer_v, BF16), cnt1, rank2, e1, e2)
    return _finalnorm(h1, y2, norm_post_ffn, mod3, seq)


def kernel(x, c, w_ada, b_ada, norm_pre_mix, norm_post_mix, norm_pre_ffn, norm_post_ffn, w_in, nsa_cmp_pos, nsa_w_cmp_k, nsa_w_cmp_v, mlstm_gate_bias, mlstm_norm_w, w_up_nsa, w_up_mlstm, w_out, peer_w_q, peer_sub_keys, peer_u, peer_v):
    batch, seq, d = x.shape
    depth = w_ada.shape[0]
    h = x.reshape(batch * seq, d)
    c_pad = jnp.concatenate([c, jnp.zeros((SUBLANES - batch % SUBLANES, d), c.dtype)], axis=0)
    for l in range(depth):
        mod = _adaln(c_pad, w_ada[l], b_ada[l])[:batch]
        mod3 = mod.reshape(batch, 6, d)
        h = _layer(h, mod3, norm_pre_mix[l], norm_post_mix[l], norm_pre_ffn[l], norm_post_ffn[l], w_in[l],
                   nsa_cmp_pos[l], nsa_w_cmp_k[l], nsa_w_cmp_v[l], mlstm_gate_bias[l], mlstm_norm_w[l],
                   w_up_nsa[l], w_up_mlstm[l], w_out[l], peer_w_q[l], peer_sub_keys[l], peer_u[l], peer_v[l],
                   batch=batch, seq=seq)
    return h.reshape(batch, seq, d).astype(x.dtype)
```

```python
import functools
import math

import jax
import jax.numpy as jnp
from jax import lax
from jax.experimental import pallas as pl
from jax.experimental.pallas import tpu as pltpu

D_MODEL = 4096
BATCH = 4
SEQ = 2048
NSA_HEADS = 16
NSA_KV_GROUPS = 4
NSA_HEAD_DIM = 128
CMP_BLOCK = 32
CMP_STRIDE = 16
SLC_BLOCK = 64
SLC_TOPN = 16
WINDOW = 512
ML_HEADS = 8
ML_QK_DIM = 256
ML_V_DIM = 512
PEER_HEADS = 8
PEER_KEY_DIM = 256
PEER_N_KEYS = 128
PEER_TOPK = 16

NEG_INF = -1e30
BIG = 1e9
TINY = 1e-30
EPS = 1e-6

F32 = jnp.float32
BF16 = jnp.bfloat16

V7X_VMEM_LIMIT_BYTES = 56 * 1024 * 1024
LANES = 128
SUBLANES = 8

ROW_TILE = 256
MM_BM = 1024
MM_BN = 1024
MMW_BN = 512
MMA_BM = 512
ADA_BN = 512
NSA_TQ = 128
NSA_TK = 512
ML_CHUNK_LEN = 128
ML_HEADS_PER_STEP = 2
PEER_TOPK_TT = 256
PEER_TT = 512
PEER_TE = 1024
PEER_DOT_CHUNK = 512

_NT = (((1,), (1,)), ((), ()))


def _params(*sem):
    return pltpu.CompilerParams(dimension_semantics=sem, vmem_limit_bytes=V7X_VMEM_LIMIT_BYTES)


def _ind(cond, dtype):
    wide = jnp.int32 if jnp.issubdtype(dtype, jnp.integer) else F32
    return jnp.where(cond, jnp.ones((), wide), jnp.zeros((), wide)).astype(dtype)


def _log2(n):
    k = int(math.log2(n))
    assert 1 << k == n
    return k


def _split3(x):
    hi = x.astype(BF16)
    r1 = x - hi.astype(F32)
    mid = r1.astype(BF16)
    lo = (r1 - mid.astype(F32)).astype(BF16)
    return hi, mid, lo


def _adaln_kernel(c_ref, w_ref, b_ref, o_ref):
    c = c_ref[...]
    cond = (c * jax.nn.sigmoid(c)).astype(BF16)
    o_ref[...] = jnp.dot(cond, w_ref[...].astype(BF16), preferred_element_type=F32) + b_ref[...]


def _adaln(c_pad, w_ada, b_ada):
    rows, d = c_pad.shape
    n = w_ada.shape[1]
    bn = min(ADA_BN, n)
    return pl.pallas_call(
        _adaln_kernel,
        grid=(n // bn,),
        in_specs=[
            pl.BlockSpec((rows, d), lambda j: (0, 0)),
            pl.BlockSpec((d, bn), lambda j: (0, j)),
            pl.BlockSpec((1, bn), lambda j: (0, j)),
        ],
        out_specs=pl.BlockSpec((rows, bn), lambda j: (0, j)),
        out_shape=jax.ShapeDtypeStruct((rows, n), F32),
        compiler_params=_params("parallel"),
        name="adaln",
    )(c_pad, w_ada, b_ada.reshape(1, n))


def _rms(x, w):
    return x * lax.rsqrt(jnp.mean(x * x, axis=-1, keepdims=True) + EPS) * w


def _prenorm_kernel(x_ref, w_ref, mod_ref, u_ref):
    y = _rms(x_ref[...], w_ref[...])
    u_ref[...] = (y * (1.0 + mod_ref[1:2, :]) + mod_ref[0:1, :]).astype(u_ref.dtype)


def _prenorm(x2, w, mod3, seq):
    n, d = x2.shape
    tr = min(ROW_TILE, seq)
    return pl.pallas_call(
        _prenorm_kernel,
        grid=(n // tr,),
        in_specs=[
            pl.BlockSpec((tr, d), lambda i: (i, 0)),
            pl.BlockSpec((1, d), lambda i: (0, 0)),
            pl.BlockSpec((None, 6, d), lambda i: ((i * tr) // seq, 0, 0)),
        ],
        out_specs=pl.BlockSpec((tr, d), lambda i: (i, 0)),
        out_shape=jax.ShapeDtypeStruct((n, d), BF16),
        compiler_params=_params("parallel"),
        name="prenorm",
    )(x2, w.reshape(1, d), mod3)


def _midnorm_kernel(x_ref, y_ref, w1_ref, w2_ref, mod_ref, h_ref, u_ref, ut_ref):
    h = x_ref[...] + mod_ref[2:3, :] * _rms(y_ref[...].astype(F32), w1_ref[...])
    h_ref[...] = h
    u = _rms(h, w2_ref[...]) * (1.0 + mod_ref[4:5, :]) + mod_ref[3:4, :]
    u_ref[...] = u.astype(u_ref.dtype)
    ut_ref[...] = u.T.astype(ut_ref.dtype)


def _midnorm(x2, y, w_post, w_pre, mod3, seq):
    n, d = x2.shape
    tr = min(ROW_TILE, seq)
    row = pl.BlockSpec((tr, d), lambda i: (i, 0))
    vec = pl.BlockSpec((1, d), lambda i: (0, 0))
    return pl.pallas_call(
        _midnorm_kernel,
        grid=(n // tr,),
        in_specs=[row, row, vec, vec, pl.BlockSpec((None, 6, d), lambda i: ((i * tr) // seq, 0, 0))],
        out_specs=[row, row, pl.BlockSpec((d, tr), lambda i: (0, i))],
        out_shape=[jax.ShapeDtypeStruct((n, d), F32), jax.ShapeDtypeStruct((n, d), BF16),
                   jax.ShapeDtypeStruct((d, n), BF16)],
        compiler_params=_params("parallel"),
        name="midnorm",
    )(x2, y, w_post.reshape(1, d), w_pre.reshape(1, d), mod3)


def _finalnorm_kernel(h_ref, y_ref, w_ref, mod_ref, o_ref):
    o_ref[...] = h_ref[...] + mod_ref[5:6, :] * _rms(y_ref[...].astype(F32), w_ref[...])


def _finalnorm(h1, y, w, mod3, seq):
    n, d = h1.shape
    tr = min(ROW_TILE, seq)
    row = pl.BlockSpec((tr, d), lambda i: (i, 0))
    return pl.pallas_call(
        _finalnorm_kernel,
        grid=(n // tr,),
        in_specs=[row, row, pl.BlockSpec((1, d), lambda i: (0, 0)),
                  pl.BlockSpec((None, 6, d), lambda i: ((i * tr) // seq, 0, 0))],
        out_specs=row,
        out_shape=jax.ShapeDtypeStruct((n, d), F32),
        compiler_params=_params("parallel"),
        name="finalnorm",
    )(h1, y, w.reshape(1, d), mod3)


def _gelu(x):
    return 0.5 * x * (1.0 + lax.erf(x * (2.0 ** -0.5)))


def _mm_kernel(*refs, has_gate, has_add, gelu):
    a_ref, b_ref = refs[0], refs[1]
    o_ref = refs[-1]
    acc = jnp.dot(a_ref[...], b_ref[...], preferred_element_type=F32)
    nxt = 2
    if has_gate:
        acc = acc * jax.nn.sigmoid(refs[nxt][...].astype(F32))
        nxt += 1
    if has_add:
        acc = acc + refs[nxt][...].astype(F32)
    if gelu:
        acc = _gelu(acc)
    o_ref[...] = acc.astype(o_ref.dtype)


def _matmul(a, b, out_dtype, gate=None, gate_col=0, add=None, gelu=False):
    m, k = a.shape
    n = b.shape[1]
    bm = min(MM_BM, m)
    bn = min(MM_BN, n)
    in_specs = [pl.BlockSpec((bm, k), lambda i, j: (i, 0)), pl.BlockSpec((k, bn), lambda i, j: (0, j))]
    args = [a, b]
    if gate is not None:
        goff = gate_col // bn
        assert goff * bn == gate_col
        in_specs.append(pl.BlockSpec((bm, bn), lambda i, j: (i, goff + j)))
        args.append(gate)
    if add is not None:
        in_specs.append(pl.BlockSpec((bm, bn), lambda i, j: (i, j)))
        args.append(add)
    return pl.pallas_call(
        functools.partial(_mm_kernel, has_gate=gate is not None, has_add=add is not None, gelu=gelu),
        grid=(m // bm, n // bn),
        in_specs=in_specs,
        out_specs=pl.BlockSpec((bm, bn), lambda i, j: (i, j)),
        out_shape=jax.ShapeDtypeStruct((m, n), out_dtype),
        compiler_params=_params("parallel", "arbitrary"),
        name="matmul",
    )(*args)


def _mmw_kernel(*refs, shifts, has_gate, has_add, row_chunk):
    a_ref, w_ref = refs[0], refs[1]
    nxt = 2
    wn_ref = None
    if shifts is not None:
        wn_ref = refs[nxt]
        nxt += 1
    o_ref, wb_ref = refs[-2], refs[-1]
    j = pl.program_id(0)
    k, bn = wb_ref.shape

    def fill(shift):
        for r0 in range(0, k, row_chunk):
            rs = slice(r0, r0 + row_chunk)
            if shift == 0:
                wb_ref[rs, :] = w_ref[rs, :].astype(BF16)
            else:
                wide = jnp.concatenate([w_ref[rs, :], wn_ref[rs, :]], axis=1)
                wb_ref[rs, :] = wide[:, shift:shift + bn].astype(BF16)

    @pl.when(pl.program_id(1) == 0)
    def _():
        if shifts is None:
            fill(0)
        else:
            for j0, j1, shift in shifts:
                pl.when((j >= j0) & (j < j1))(functools.partial(fill, shift))

    acc = jnp.dot(a_ref[...], wb_ref[...], preferred_element_type=F32)
    if has_gate:
        acc = acc * jax.nn.sigmoid(refs[nxt][...].astype(F32))
        nxt += 1
    if has_add:
        acc = acc + refs[nxt][...].astype(F32)
    o_ref[...] = acc.astype(o_ref.dtype)


def _matmul_w(a, w, out_dtype, *, n_out=None, shifts=None, gate=None, gate_col=0, add=None):
    m, k = a.shape
    n = w.shape[1] if n_out is None else n_out
    bm = min(MM_BM, m)
    bn = min(MMW_BN, n)
    assert n % bn == 0 and m % bm == 0
    in_specs = [pl.BlockSpec((bm, k), lambda j, i: (i, 0)), pl.BlockSpec((k, bn), lambda j, i: (0, j))]
    args = [a, w]
    if shifts is not None:
        per = bn // LANES
        in_specs.append(pl.BlockSpec((k, LANES), lambda j, i: (0, (j + 1) * per)))
        args.append(w)
    if gate is not None:
        goff = gate_col // bn
        assert goff * bn == gate_col
        in_specs.append(pl.BlockSpec((bm, bn), lambda j, i: (i, goff + j)))
        args.append(gate)
    if add is not None:
        in_specs.append(pl.BlockSpec((bm, bn), lambda j, i: (i, j)))
        args.append(add)
    return pl.pallas_call(
        functools.partial(_mmw_kernel, shifts=shifts, has_gate=gate is not None, has_add=add is not None,
                          row_chunk=min(512, k)),
        grid=(n // bn, m // bm),
        in_specs=in_specs,
        out_specs=pl.BlockSpec((bm, bn), lambda j, i: (i, j)),
        out_shape=jax.ShapeDtypeStruct((m, n), out_dtype),
        scratch_shapes=[pltpu.VMEM((k, bn), BF16)],
        compiler_params=_params("parallel", "arbitrary"),
        name="matmul_w",
    )(*args)


def _inproj_kernel(a_ref, wt_ref, wtn_ref, o_ref, wb_ref, *, shifts, col_chunk):
    j = pl.program_id(0)
    k, bn = wb_ref.shape

    def fill(shift):
        for c0 in range(0, k, col_chunk):
            cs = slice(c0, c0 + col_chunk)
            if shift == 0:
                blk = wt_ref[:, cs]
            else:
                tall = jnp.concatenate([wt_ref[:, cs], wtn_ref[:, cs]], axis=0)
                blk = tall[shift:shift + bn, :]
            wb_ref[cs, :] = blk.T.astype(BF16)

    @pl.when(pl.program_id(1) == 0)
    def _():
        for j0, j1, shift in shifts:
            pl.when((j >= j0) & (j < j1))(functools.partial(fill, shift))

    acc = jnp.dot(a_ref[...], wb_ref[...], preferred_element_type=F32)
    o_ref[...] = acc.astype(o_ref.dtype)


def _inproj(a, wt, n_out, shifts):
    m, k = a.shape
    bm = min(MM_BM, m)
    bn = MMW_BN
    assert n_out % bn == 0 and m % bm == 0 and all(s % SUBLANES == 0 and s <= LANES for _, _, s in shifts)
    per = bn // LANES
    return pl.pallas_call(
        functools.partial(_inproj_kernel, shifts=shifts, col_chunk=min(512, k)),
        grid=(n_out // bn, m // bm),
        in_specs=[pl.BlockSpec((bm, k), lambda j, i: (i, 0)),
                  pl.BlockSpec((bn, k), lambda j, i: (j, 0)),
                  pl.BlockSpec((LANES, k), lambda j, i: ((j + 1) * per, 0))],
        out_specs=pl.BlockSpec((bm, bn), lambda j, i: (i, j)),
        out_shape=jax.ShapeDtypeStruct((m, n_out), BF16),
        scratch_shapes=[pltpu.VMEM((k, bn), BF16)],
        compiler_params=_params("parallel", "arbitrary"),
        name="inproj",
    )(a, wt, wt)


def _gate_proj_kernel(a_ref, wa_ref, wb_ref, o_ref, *, n_a, off_b, n_b):
    k = wa_ref.shape[1]
    rows = jnp.concatenate([wa_ref[0:n_a, :], wb_ref[off_b:off_b + n_b, :],
                            jnp.zeros((LANES - n_a - n_b, k), F32)], axis=0).astype(BF16)
    o_ref[...] = lax.dot_general(a_ref[...], rows, _NT, preferred_element_type=F32)


def _gate_proj(a, wt, row_a, n_a, row_b, n_b):
    m, k = a.shape
    bm = min(MM_BM, m)
    off_b = row_b % LANES
    assert row_a % LANES == 0 and n_a % SUBLANES == 0 and off_b % SUBLANES == 0 and n_b % SUBLANES == 0
    assert off_b + n_b <= LANES and n_a + n_b <= LANES
    return pl.pallas_call(
        functools.partial(_gate_proj_kernel, n_a=n_a, off_b=off_b, n_b=n_b),
        grid=(m // bm,),
        in_specs=[pl.BlockSpec((bm, k), lambda i: (i, 0)),
                  pl.BlockSpec((LANES, k), lambda i: (row_a // LANES, 0)),
                  pl.BlockSpec((LANES, k), lambda i: (row_b // LANES, 0))],
        out_specs=pl.BlockSpec((bm, LANES), lambda i: (i, 0)),
        out_shape=jax.ShapeDtypeStruct((m, LANES), F32),
        compiler_params=_params("parallel"),
        name="gate_proj",
    )(a, wt, wt)


def _mma_kernel(a_ref, b_ref, o_ref, ab_ref, *, gelu, row_chunk):
    @pl.when(pl.program_id(1) == 0)
    def _():
        for r0 in range(0, ab_ref.shape[0], row_chunk):
            ab_ref[r0:r0 + row_chunk, :] = a_ref[r0:r0 + row_chunk, :].astype(BF16)

    acc = jnp.dot(ab_ref[...], b_ref[...], preferred_element_type=F32)
    if gelu:
        acc = _gelu(acc)
    o_ref[...] = acc.astype(o_ref.dtype)


def _matmul_a(a, b, out_dtype, gelu=False):
    m, k = a.shape
    n = b.shape[1]
    bm = min(MMA_BM, m)
    bn = min(MM_BN, n)
    return pl.pallas_call(
        functools.partial(_mma_kernel, gelu=gelu, row_chunk=min(128, bm)),
        grid=(m // bm, n // bn),
        in_specs=[pl.BlockSpec((bm, k), lambda i, j: (i, 0)), pl.BlockSpec((k, bn), lambda i, j: (0, j))],
        out_specs=pl.BlockSpec((bm, bn), lambda i, j: (i, j)),
        out_shape=jax.ShapeDtypeStruct((m, n), out_dtype),
        scratch_shapes=[pltpu.VMEM((bm, k), BF16)],
        compiler_params=_params("parallel", "arbitrary"),
        name="matmul_a",
    )(a, b)


def _pos_columns(pos, width):
    lane = lax.broadcasted_iota(jnp.int32, (pos.shape[0], width), 1)
    coarse = (pos >> 6) << 6
    vals = jnp.where(lane < 3, coarse, jnp.where(lane < 6, pos & 63, 0))
    return vals.astype(F32).astype(BF16)


def _nsa_kernel(slopes_ref, q_ref, kcm_ref, vcm_ref, ks_ref, vs_ref, kw_ref, vw_ref, gz_ref, wck_ref, wcv_ref,
                pos_ref, o_ref, kc_ref, vc_ref, pext_ref, xs_ref, acc_ref, qext_ref, *, seq, tq, tk,
                heads_per_group):
    R = heads_per_group
    dk = NSA_HEAD_DIM
    g = pl.program_id(1)
    qi = pl.program_id(2)
    t0 = qi * tq
    n_cmp = (seq - CMP_BLOCK) // CMP_STRIDE + 1
    n_slc = seq // SLC_BLOCK
    n_sel = min(SLC_TOPN, n_slc)
    half = CMP_STRIDE * dk
    scale = dk ** -0.5
    c_exp = scale * math.log2(math.e)
    rows = R * tq

    @pl.when(qi == 0)
    def _():
        pext_ref[...] = _pos_columns(lax.broadcasted_iota(jnp.int32, (seq, 1), 0), LANES)
        cpos = lax.broadcasted_iota(jnp.int32, (LANES, 1), 0) * CMP_STRIDE + (CMP_BLOCK - 1)
        for src_ref, w_ref, dst, pi in ((kcm_ref, wck_ref, kc_ref, 0), (vcm_ref, wcv_ref, vc_ref, 1)):
            xs_ref[...] = src_ref[...].astype(F32)
            x = jnp.concatenate([xs_ref[pl.ds(l, LANES, stride=CMP_STRIDE), :].astype(BF16)
                                 for l in range(CMP_STRIDE)], axis=1)
            ya = jnp.dot(x, w_ref[0:half, :], preferred_element_type=F32)
            yb = jnp.dot(x, w_ref[half:2 * half, :], preferred_element_type=F32)
            pos = jnp.broadcast_to(pos_ref[pi], (SUBLANES, 2 * half)).astype(BF16)
            pb = jnp.dot(pos, w_ref[...], preferred_element_type=F32)[0:1, :]
            comp = (ya + pltpu.roll(yb, LANES - 1, 0) + pb).astype(BF16)
            dst[...] = jnp.concatenate([comp, _pos_columns(cpos, LANES)], axis=1) if pi == 0 else comp

        head_of_row = lax.broadcasted_iota(jnp.int32, (rows, LANES), 0) >> _log2(tq)
        lane6 = lax.broadcasted_iota(jnp.int32, (rows, LANES), 1)
        term = jnp.where(lane6 < 3, lane6, lane6 - 3)
        q_ext = jnp.zeros((rows, LANES), F32)
        for r in range(R):
            for k in range(3):
                q_ext = jnp.where((head_of_row == r) & (term == k) & (lane6 < 6), slopes_ref[g, r, k], q_ext)
        qext_ref[...] = q_ext.astype(BF16)

    row_id = lax.broadcasted_iota(jnp.int32, (rows, 1), 0)
    t_col = t0 + (row_id & (tq - 1))
    q = q_ref[...]
    qs = jnp.concatenate([q[:, r * dk:(r + 1) * dk] for r in range(R)], axis=0)
    qa = jnp.concatenate([qs, qext_ref[...]], axis=1)

    n_idx = lax.broadcasted_iota(jnp.int32, (rows, LANES), 1)
    cvalid = (t_col >= n_idx * CMP_STRIDE + CMP_BLOCK - 1) & (n_idx < n_cmp)
    sc = lax.dot_general(qa, kc_ref[...], _NT, preferred_element_type=F32)
    sc = jnp.where(cvalid, sc, NEG_INF)
    mc = jnp.max(sc, axis=-1, keepdims=True)
    p_cmp = jnp.where(cvalid, jnp.exp2((sc - mc) * c_exp), 0.0)
    p_cmp = p_cmp * (1.0 / jnp.maximum(jnp.sum(p_cmp, axis=-1, keepdims=True), TINY))
    o_cmp = jnp.dot(p_cmp.astype(BF16), vc_ref[...], preferred_element_type=F32)

    psum = p_cmp[0:tq]
    for r in range(1, R):
        psum = psum + p_cmp[r * tq:(r + 1) * tq]
    jj = lax.broadcasted_iota(jnp.int32, (LANES, LANES), 0)
    nn = lax.broadcasted_iota(jnp.int32, (LANES, LANES), 1)
    ovl = _ind((nn * CMP_STRIDE < jj * SLC_BLOCK + SLC_BLOCK) & (nn * CMP_STRIDE + CMP_BLOCK > jj * SLC_BLOCK)
               & (jj < n_slc) & (nn < n_cmp), BF16)
    imp_t = jnp.zeros((LANES, tq), F32)
    for part in _split3(psum):
        imp_t = imp_t + lax.dot_general(ovl, part, _NT, preferred_element_type=F32)
    imp_t = imp_t[0:n_slc]
    j_i = lax.broadcasted_iota(jnp.int32, (n_slc, tq), 0)
    t_i = t0 + lax.broadcasted_iota(jnp.int32, (n_slc, tq), 1)
    cur = t_i >> _log2(SLC_BLOCK)
    causal = j_i * SLC_BLOCK <= t_i
    forced = (j_i == 0) | (j_i == cur) | (j_i == cur - 1)
    score = jnp.where(causal, jnp.where(forced, BIG, imp_t), -BIG)
    rank = jnp.zeros((n_slc, tq), jnp.int32)
    for jp in range(n_slc):
        row = score[jp:jp + 1, :]
        beats = (row > score) | ((row == score) & (j_i > jp))
        rank = rank + _ind(beats, jnp.int32)
    sel_t = jnp.where(rank < n_sel, 0.0, NEG_INF)
    sel_t = jnp.concatenate([sel_t, jnp.zeros((LANES - n_slc, tq), F32)], axis=0)
    sel_neg = sel_t.T.astype(BF16)
    tq_col = t0 + lax.broadcasted_iota(jnp.int32, (tq, 1), 0)

    def attend(s, v):
        p = jnp.exp2((s - jnp.max(s, axis=-1, keepdims=True)) * c_exp).astype(BF16)
        ones = _ind(lax.broadcasted_iota(jnp.int32, (v.shape[0], LANES), 1) == 0, BF16)
        pv = jnp.dot(p, jnp.concatenate([v, ones], axis=1), preferred_element_type=F32)
        return pv[:, 0:dk] * (1.0 / jnp.maximum(pv[:, dk:dk + 1], TINY))

    def slc_variant(nk):
        ka = jnp.concatenate([ks_ref[0:nk, :], pext_ref[0:nk, :]], axis=1)
        ej = lax.broadcasted_iota(jnp.int32, (LANES, nk), 0)
        ec = lax.broadcasted_iota(jnp.int32, (LANES, nk), 1)
        expand = _ind(ej == (ec >> _log2(SLC_BLOCK)), BF16)
        madd = jnp.dot(sel_neg, expand, preferred_element_type=F32)
        madd = madd + jnp.where(lax.broadcasted_iota(jnp.int32, (tq, nk), 1) <= tq_col, 0.0, NEG_INF)
        s = lax.dot_general(qa, ka, _NT, preferred_element_type=F32) + jnp.concatenate([madd] * R, axis=0)
        acc_ref[...] = attend(s, vs_ref[0:nk, :])

    variant = (t0 + tq - 1) // tk
    for vi in range(seq // tk):
        pl.when(variant == vi)(functools.partial(slc_variant, (vi + 1) * tk))
    o_slc = acc_ref[...]

    span = WINDOW + tq
    ws = pl.multiple_of(jnp.maximum(t0 - WINDOW, 0), tq)
    kwa = jnp.concatenate([kw_ref[pl.ds(ws, span), :], pext_ref[pl.ds(ws, span), :]], axis=1)
    dw_i = tq_col - (ws + lax.broadcasted_iota(jnp.int32, (tq, span), 1))
    wadd = jnp.where((dw_i >= 0) & (dw_i < WINDOW), 0.0, NEG_INF)
    sw = lax.dot_general(qa, kwa, _NT, preferred_element_type=F32) + jnp.concatenate([wadd] * R, axis=0)
    o_win = attend(sw, vw_ref[pl.ds(ws, span), :])

    gz = gz_ref[...]
    gsel = jnp.zeros((tq, 3 * R), F32)
    for gg in range(NSA_KV_GROUPS):
        gsel = jnp.where(g == gg, gz[:, 3 * R * gg:3 * R * (gg + 1)], gsel)
    gts = jax.nn.sigmoid(gsel)
    for r in range(R):
        sl = slice(r * tq, (r + 1) * tq)
        o = (gts[:, 3 * r:3 * r + 1] * o_cmp[sl] + gts[:, 3 * r + 1:3 * r + 2] * o_slc[sl]
             + gts[:, 3 * r + 2:3 * r + 3] * o_win[sl])
        o_ref[:, r * dk:(r + 1) * dk] = o.astype(o_ref.dtype)


def _nsa(slopes, z, gz, wck, wcv, pos, *, batch, seq, kv_col):
    G = NSA_KV_GROUPS
    R = NSA_HEADS // G
    dk = NSA_HEAD_DIM
    tq = NSA_TQ
    tk = NSA_TK
    nq = seq // tq
    assert seq // CMP_STRIDE == LANES and CMP_BLOCK == 2 * CMP_STRIDE and seq >= WINDOW + tq and seq % tk == 0
    kv0 = kv_col // dk

    def kv_spec(i):
        return pl.BlockSpec((seq, dk), lambda b, g, qi: (b, kv0 + i * G + g))

    w_spec = pl.BlockSpec((CMP_BLOCK * dk, dk), lambda b, g, qi: (0, 0))
    rows = R * tq
    assert dk == LANES and 3 * NSA_HEADS <= LANES
    return pl.pallas_call(
        functools.partial(_nsa_kernel, seq=seq, tq=tq, tk=tk, heads_per_group=R),
        grid=(batch, G, nq),
        in_specs=[
            pl.BlockSpec(memory_space=pltpu.SMEM),
            pl.BlockSpec((tq, R * dk), lambda b, g, qi: (b * nq + qi, g)),
            kv_spec(0), kv_spec(1), kv_spec(2), kv_spec(3), kv_spec(4), kv_spec(5),
            pl.BlockSpec((tq, LANES), lambda b, g, qi: (b * nq + qi, 0)),
            w_spec, w_spec,
            pl.BlockSpec((2, 1, CMP_BLOCK * dk), lambda b, g, qi: (0, 0, 0)),
        ],
        out_specs=pl.BlockSpec((tq, R * dk), lambda b, g, qi: (b * nq + qi, g)),
        out_shape=jax.ShapeDtypeStruct((batch * seq, NSA_HEADS * dk), BF16),
        scratch_shapes=[
            pltpu.VMEM((LANES, 2 * dk), BF16), pltpu.VMEM((LANES, dk), BF16),
            pltpu.VMEM((seq, LANES), BF16), pltpu.VMEM((seq, dk), F32), pltpu.VMEM((rows, dk), F32),
            pltpu.VMEM((rows, LANES), BF16),
        ],
        compiler_params=_params("parallel", "parallel", "arbitrary"),
        name="nsa",
    )(slopes, z, z, z, z, z, z, z, gz, wck, wcv, pos)


def _log_sigmoid(x):
    return jnp.minimum(x, 0.0) - jnp.log1p(jnp.exp(-jnp.abs(x)))


def _mlstm_kernel(bias_ref, q_ref, k_ref, v_ref, og_ref, gz_ref, nw_ref, o_ref, c_ref, *, seq, L, gate_lane, hps):
    dqk = ML_QK_DIM
    dv = ML_V_DIM
    nc = seq // L
    lane = lax.broadcasted_iota(jnp.int32, (L, LANES), 1)

    def gate_forms(blk, which, bias):
        col = jnp.sum(jnp.where(lane == which, blk, 0.0), axis=1, keepdims=True) + bias
        return col, jnp.broadcast_to(col, (L, LANES)).T[0:1, :]

    ext = dv + LANES
    ii = lax.broadcasted_iota(jnp.int32, (L, L), 0)
    kk = lax.broadcasted_iota(jnp.int32, (L, L), 1)
    causal = kk <= ii
    tri = _ind(causal, BF16)
    tri_t = _ind(ii <= kk, BF16)
    ones_slab = _ind(lax.broadcasted_iota(jnp.int32, (L, LANES), 1) == 0, BF16)
    c_ref[...] = jnp.zeros((hps, dqk, ext), F32)

    def cumsum_mats(lf_row, lf_col):
        lr = jnp.broadcast_to(lf_row, (L, L))
        lc = jnp.broadcast_to(lf_col, (L, L))
        b_row = jnp.zeros((L, L), F32)
        b_col = jnp.zeros((L, L), F32)
        for part in _split3(lr)[:2]:
            b_row = b_row + jnp.dot(part, tri_t, preferred_element_type=F32)
        for part in _split3(lc)[:2]:
            b_col = b_col + jnp.dot(tri, part, preferred_element_type=F32)
        return b_row, b_col

    def head_chunk(hh, r0, gblk, m):
        h_id = pl.program_id(1) * hps + hh
        i_lane = gate_lane + h_id
        f_lane = i_lane + ML_HEADS
        qc = q_ref[pl.ds(r0, L), hh * dqk:(hh + 1) * dqk]
        kc = k_ref[pl.ds(r0, L), hh * dqk:(hh + 1) * dqk] * (dqk ** -0.5)
        v_ext = jnp.concatenate([v_ref[pl.ds(r0, L), hh * dv:(hh + 1) * dv], ones_slab], axis=1)
        i_col, i_row = gate_forms(gblk, i_lane, bias_ref[0, h_id])
        f_col, f_row = gate_forms(gblk, f_lane, bias_ref[1, h_id])
        lf_col = _log_sigmoid(f_col)
        lf_row = _log_sigmoid(f_row)
        b_row, b_col = cumsum_mats(lf_row, lf_col)
        log_d = jnp.where(causal, b_col - b_row + i_row, NEG_INF)
        b_c = b_col[:, 0:1]
        m_inter = b_c + m
        m_t = jnp.maximum(m_inter, jnp.max(log_d, axis=-1, keepdims=True))
        d = jnp.exp(log_d - m_t)
        s = lax.dot_general(qc, kc, _NT, preferred_element_type=F32) * d
        w_inter = jnp.exp(m_inter - m_t)
        inter = jnp.dot(qc, c_ref[hh].astype(BF16), preferred_element_type=F32)
        intra = jnp.dot(s.astype(BF16), v_ext, preferred_element_type=F32)
        numden = w_inter * inter + intra
        den = numden[:, dv:dv + 1]
        h = numden[:, 0:dv] / jnp.maximum(jnp.abs(den), jnp.exp(-m_t))
        g_tot = b_row[0:1, L - 1:L]
        a_row = g_tot - b_row[0:1, :] + i_row
        a_col = g_tot - b_c + i_col
        m_new = jnp.maximum(g_tot + m, jnp.max(a_row, axis=-1, keepdims=True))
        decay = jnp.exp(g_tot + m - m_new)
        w_tok = jnp.exp(a_col - m_new)
        kw_t = (kc.astype(F32) * w_tok).T.astype(BF16)
        c_ref[hh] = decay * c_ref[hh] + jnp.dot(kw_t, v_ext, preferred_element_type=F32)
        hn = _rms(h, nw_ref[hh]) * jax.nn.sigmoid(og_ref[pl.ds(r0, L), hh * dv:(hh + 1) * dv].astype(F32))
        o_ref[pl.ds(r0, L), hh * dv:(hh + 1) * dv] = hn.astype(o_ref.dtype)
        return m_new

    def body(c, ms):
        r0 = pl.multiple_of(c * L, L)
        gblk = gz_ref[pl.ds(r0, L), :]
        return tuple(head_chunk(hh, r0, gblk, ms[hh]) for hh in range(hps))

    lax.fori_loop(0, nc, body, tuple(jnp.zeros((1, 1), F32) for _ in range(hps)))


def _mlstm(gate_bias, z, gz, norm_w, *, batch, seq, q_col, k_col, v_col, o_col, gate_lane):
    H = ML_HEADS
    dqk = ML_QK_DIM
    dv = ML_V_DIM
    L = min(ML_CHUNK_LEN, seq)
    hps = ML_HEADS_PER_STEP
    qw, vw = hps * dqk, hps * dv
    assert L == LANES and gate_lane + 2 * H <= LANES and H % hps == 0
    assert q_col % qw == 0 and k_col % qw == 0 and v_col % vw == 0 and o_col % vw == 0
    return pl.pallas_call(
        functools.partial(_mlstm_kernel, seq=seq, L=L, gate_lane=gate_lane, hps=hps),
        grid=(batch, H // hps),
        in_specs=[
            pl.BlockSpec(memory_space=pltpu.SMEM),
            pl.BlockSpec((seq, qw), lambda b, h: (b, q_col // qw + h)),
            pl.BlockSpec((seq, qw), lambda b, h: (b, k_col // qw + h)),
            pl.BlockSpec((seq, vw), lambda b, h: (b, v_col // vw + h)),
            pl.BlockSpec((seq, vw), lambda b, h: (b, o_col // vw + h)),
            pl.BlockSpec((seq, LANES), lambda b, h: (b, 0)),
            pl.BlockSpec((hps, 1, dv), lambda b, h: (h, 0, 0)),
        ],
        out_specs=pl.BlockSpec((seq, vw), lambda b, h: (b, h)),
        out_shape=jax.ShapeDtypeStruct((batch * seq, H * dv), BF16),
        scratch_shapes=[pltpu.VMEM((hps, dqk, dv + LANES), F32)],
        compiler_params=_params("parallel", "parallel"),
        name="mlstm",
    )(gate_bias, z, z, z, z, gz, norm_w.reshape(H, 1, dv))


def _peer_pairs():
    return [(a, (PEER_TOPK + 1) // (a + 1)) for a in range(PEER_TOPK + 1)]


def _topk_multiset(s, k, with_rank=False):
    tt = s.shape[1]
    riota = lax.broadcasted_iota(jnp.int32, (k, tt), 0).astype(F32)
    v = jnp.full((k, tt), -jnp.inf, F32)
    taken = jnp.zeros((1, tt), F32)
    rank = jnp.full(s.shape, float(s.shape[0]), F32)
    rem = s
    for _ in range(k):
        m = jnp.max(rem, axis=0, keepdims=True)
        eq = rem == m
        cnt = jnp.sum(_ind(eq, F32), axis=0, keepdims=True)
        v = jnp.where((riota >= taken) & (riota < taken + cnt), m, v)
        if with_rank:
            rank = jnp.where(eq, taken, rank)
        taken = taken + cnt
        rem = jnp.where(eq, -jnp.inf, rem)
    return (v, rank) if with_rank else v


def _peer_topk_kernel(q_ref, khi_ref, klo_ref, cnt_ref, rank_ref, e1_ref, e2_ref, cand_ref):
    H = PEER_HEADS
    half = PEER_KEY_DIM // 2
    K = PEER_TOPK
    pairs = _peer_pairs()
    n_cand = sum(nb for _, nb in pairs)
    tt = q_ref.shape[0]
    for h in range(H):
        s_parts = []
        for side in range(2):
            qf = q_ref[:, (2 * h + side) * half:(2 * h + side + 1) * half]
            q_hi = qf.astype(BF16)
            q_lo = (qf - q_hi.astype(F32)).astype(BF16)
            k_hi = khi_ref[side, h]
            k_lo = klo_ref[side, h]
            s = (lax.dot_general(k_hi, q_hi, _NT, preferred_element_type=F32)
                 + lax.dot_general(k_hi, q_lo, _NT, preferred_element_type=F32)
                 + lax.dot_general(k_lo, q_hi, _NT, preferred_element_type=F32))
            s_parts.append(s)
        s1, s2 = s_parts
        v1 = _topk_multiset(s1, K + 1)
        v2, rank2 = _topk_multiset(s2, K + 1, with_rank=True)
        off = 0
        for a, nb in pairs:
            cand_ref[off:off + nb, :] = v1[a:a + 1, :] + v2[0:nb, :]
            off += nb
        cand_ref[n_cand:, :] = jnp.full((cand_ref.shape[0] - n_cand, tt), -jnp.inf, F32)
        rem = cand_ref[...]
        taken = jnp.zeros((1, tt), F32)
        tau16 = jnp.zeros((1, tt), F32)
        tau17 = jnp.zeros((1, tt), F32)
        zsum = jnp.zeros((1, tt), F32)
        cmax = v1[0:1, :] + v2[0:1, :]
        for _ in range(K + 1):
            m = jnp.max(rem, axis=0, keepdims=True)
            eq = rem == m
            cnt = jnp.sum(_ind(eq, F32), axis=0, keepdims=True)
            take = jnp.minimum(cnt, jnp.maximum(K - taken, 0.0))
            zsum = zsum + take * jnp.exp(m - cmax)
            after = taken + cnt
            tau16 = jnp.where((taken < K) & (after >= K), m, tau16)
            tau17 = jnp.where((taken <= K) & (after > K), m, tau17)
            taken = after
            rem = jnp.where(eq, -jnp.inf, rem)
        theta = 0.5 * (tau16 + tau17)
        r1 = theta - s1
        cnt1 = jnp.zeros(r1.shape, F32)
        for b in range(K + 1):
            cnt1 = cnt1 + _ind(v2[b:b + 1, :] >= r1, F32)
        cnt_ref[h] = cnt1
        rank_ref[h] = rank2.astype(rank_ref.dtype)
        e1_ref[h] = jnp.exp(s1 - v1[0:1, :])
        e2_ref[h] = (jnp.exp(s2 - v2[0:1, :]) / zsum).astype(e2_ref.dtype)


def _peer_topk(pq, k_hi, k_lo):
    n = pq.shape[0]
    H = PEER_HEADS
    nk = PEER_N_KEYS
    tt = min(PEER_TOPK_TT, n)
    n_cand = sum(nb for _, nb in _peer_pairs())
    cand_rows = -(-n_cand // SUBLANES) * SUBLANES
    key_spec = pl.BlockSpec((2, H, nk, PEER_KEY_DIM // 2), lambda i: (0, 0, 0, 0))
    hkt = pl.BlockSpec((H, nk, tt), lambda i: (0, 0, i))
    wide = jax.ShapeDtypeStruct((H, nk, n), F32)
    narrow = jax.ShapeDtypeStruct((H, nk, n), BF16)
    return pl.pallas_call(
        _peer_topk_kernel,
        grid=(n // tt,),
        in_specs=[pl.BlockSpec((tt, pq.shape[1]), lambda i: (i, 0)), key_spec, key_spec],
        out_specs=[hkt, hkt, hkt, hkt],
        out_shape=[wide, narrow, wide, narrow],
        scratch_shapes=[pltpu.VMEM((cand_rows, tt), F32)],
        compiler_params=_params("parallel"),
        name="peer_topk",
    )(pq, k_hi, k_lo)


def _peer_mix_kernel(act_ref, vt_ref, cnt_ref, e1_ref, rank_ref, e2_ref, o_ref, acc_ref, a_ref, *, chunk):
    H = PEER_HEADS
    nk = PEER_N_KEYS
    te, tt = a_ref.shape
    e = pl.program_id(1)
    pack = 2 * SUBLANES

    @pl.when(e == 0)
    def _():
        acc_ref[...] = jnp.zeros(acc_ref.shape, F32)

    for i in range(te // nk):
        cnt_rows = [cnt_ref[h, i:i + 1, :] for h in range(H)]
        e1_rows = [e1_ref[h, i:i + 1, :] for h in range(H)]
        for lt in range(tt // LANES):
            ls = slice(lt * LANES, (lt + 1) * LANES)
            cntb = [jnp.broadcast_to(cnt_rows[h][:, ls], (pack, LANES)).astype(BF16) for h in range(H)]
            e1b = [jnp.broadcast_to(e1_rows[h][:, ls], (pack, LANES)).astype(BF16) for h in range(H)]
            for sg in range(nk // pack):
                rs = slice(sg * pack, (sg + 1) * pack)
                w = jnp.zeros((pack, LANES), BF16)
                for h in range(H):
                    hit = rank_ref[h, rs, ls] < cntb[h]
                    w = w + e1b[h] * jnp.where(hit, e2_ref[h, rs, ls], jnp.zeros((), BF16))
                rr = slice(i * nk + sg * pack, i * nk + (sg + 1) * pack)
                a_ref[rr, ls] = act_ref[rr, ls] * w
        done = (i + 1) * nk
        if done % chunk == 0:
            ks = slice(done - chunk, done)
            acc_ref[...] += jnp.dot(vt_ref[:, ks], a_ref[ks, :], preferred_element_type=F32)

    @pl.when(e == pl.num_programs(1) - 1)
    def _():
        o_ref[...] = acc_ref[...].T.astype(o_ref.dtype)


def _peer_mix(act_t, pvt, cnt1, rank2, e1, e2):
    n_exp, n = act_t.shape
    d = pvt.shape[0]
    H = PEER_HEADS
    nk = PEER_N_KEYS
    tt = min(PEER_TT, n)
    te = PEER_TE
    assert te == SUBLANES * nk and n_exp % te == 0

    def i1_spec():
        return pl.BlockSpec((H, SUBLANES, tt), lambda i, e: (0, e, i))

    def i2_spec():
        return pl.BlockSpec((H, nk, tt), lambda i, e: (0, 0, i))

    return pl.pallas_call(
        functools.partial(_peer_mix_kernel, chunk=PEER_DOT_CHUNK),
        grid=(n // tt, n_exp // te),
        in_specs=[
            pl.BlockSpec((te, tt), lambda i, e: (e, i)),
            pl.BlockSpec((d, te), lambda i, e: (0, e)),
            i1_spec(), i1_spec(), i2_spec(), i2_spec(),
        ],
        out_specs=pl.BlockSpec((tt, d), lambda i, e: (i, 0)),
        out_shape=jax.ShapeDtypeStruct((n, d), BF16),
        scratch_shapes=[pltpu.VMEM((d, tt), F32), pltpu.VMEM((te, tt), BF16)],
        compiler_params=_params("parallel", "arbitrary"),
        name="peer_mix",
    )(act_t, pvt, cnt1, e1, rank2, e2)


def _cast_transpose_kernel(x_ref, o_ref):
    o_ref[...] = x_ref[...].T.astype(o_ref.dtype)


def _cast_transpose(x, dtype):
    r, c = x.shape
    tr = min(MMA_BM, r)
    return pl.pallas_call(
        _cast_transpose_kernel,
        grid=(r // tr,),
        in_specs=[pl.BlockSpec((tr, c), lambda i: (i, 0))],
        out_specs=pl.BlockSpec((c, tr), lambda i: (0, i)),
        out_shape=jax.ShapeDtypeStruct((c, r), dtype),
        compiler_params=_params("parallel"),
        name="cast_transpose",
    )(x)


def _layer(h, mod3, norm_pre_mix, norm_post_mix, norm_pre_ffn, norm_post_ffn, w_in, cmp_pos, w_cmp_k, w_cmp_v,
           gate_bias, ml_norm_w, w_up_nsa, w_up_mlstm, w_out, peer_w_q, sub_keys, peer_u, peer_v, *, batch, seq):
    d = h.shape[1]
    G = NSA_KV_GROUPS
    R = NSA_HEADS // G
    dk = NSA_HEAD_DIM
    H = ML_HEADS
    nsa_w = NSA_HEADS * dk
    kv_w = 6 * G * dk
    mq_w = H * ML_QK_DIM
    mv_w = H * ML_V_DIM
    c_kv = nsa_w
    c_gn = c_kv + kv_w
    c_mq = c_gn + 3 * NSA_HEADS
    c_mk = c_mq + mq_w
    c_mv = c_mk + mq_w
    c_mo = c_mv + mv_w
    c_if = c_mo + mv_w
    c_mg = c_if + 2 * H
    n_gate = 3 * NSA_HEADS + 2 * H
    z_kv = nsa_w
    z_mq = z_kv + kv_w
    z_mk = z_mq + mq_w
    z_mv = z_mk + mq_w
    z_mo = z_mv + mv_w
    z_mg = z_mo + mv_w
    z_w = z_mg + 2 * d
    assert z_mq % MMW_BN == 0 and z_mg % MMW_BN == 0 and z_w % MMW_BN == 0
    z_shifts = [(0, z_mq // MMW_BN, 0), (z_mq // MMW_BN, z_mg // MMW_BN, 3 * NSA_HEADS),
                (z_mg // MMW_BN, z_w // MMW_BN, n_gate)]

    u = _prenorm(h, norm_pre_mix, mod3, seq)
    w_in_t = w_in.T
    z = _inproj(u, w_in_t, z_w, z_shifts)
    gz = _gate_proj(u, w_in_t, c_gn, 3 * NSA_HEADS, c_if, 2 * H)

    slopes = jnp.exp2(-8.0 * jnp.arange(1, NSA_HEADS + 1, dtype=F32) / NSA_HEADS) / (dk ** -0.5)
    slopes = jnp.stack([t.astype(F32) for t in _split3(slopes)], axis=-1).reshape(G, R, 3)
    o_nsa = _nsa(slopes, z, gz, w_cmp_k.astype(BF16), w_cmp_v.astype(BF16),
                 cmp_pos.reshape(2, 1, CMP_BLOCK * dk), batch=batch, seq=seq, kv_col=z_kv)

    h_ml = _mlstm(gate_bias, z, gz, ml_norm_w, batch=batch, seq=seq, q_col=z_mq, k_col=z_mk, v_col=z_mv,
                  o_col=z_mo, gate_lane=3 * NSA_HEADS)

    t1 = _matmul_w(o_nsa, w_up_nsa, BF16, gate=z, gate_col=z_mg)
    mix = _matmul_w(h_ml, w_up_mlstm, BF16, gate=z, gate_col=z_mg + d, add=t1)
    y = _matmul_w(mix, w_out, BF16)
    h1, u2, u2_t = _midnorm(h, y, norm_post_mix, norm_pre_ffn, mod3, seq)

    pq = _matmul_w(u2, peer_w_q, F32)
    k_hi = sub_keys.astype(BF16)
    k_lo = (sub_keys - k_hi.astype(F32)).astype(BF16)
    cnt1, rank2, e1, e2 = _peer_topk(pq, k_hi, k_lo)
    act_t = _matmul_a(peer_u, u2_t, BF16, gelu=True)
    y2 = _peer_mix(act_t, _cast_transpose(peer_v, BF16), cnt1, rank2, e1, e2)
    return _finalnorm(h1, y2, norm_post_ffn, mod3, seq)


def kernel(x, c, w_ada, b_ada, norm_pre_mix, norm_post_mix, norm_pre_ffn, norm_post_ffn, w_in, nsa_cmp_pos, nsa_w_cmp_k, nsa_w_cmp_v, mlstm_gate_bias, mlstm_norm_w, w_up_nsa, w_up_mlstm, w_out, peer_w_q, peer_sub_keys, peer_u, peer_v):
    batch, seq, d = x.shape
    depth = w_ada.shape[0]
    h = x.reshape(batch * seq, d)
    c_pad = jnp.concatenate([c, jnp.zeros((SUBLANES - batch % SUBLANES, d), c.dtype)], axis=0)
    for l in range(depth):
        mod = _adaln(c_pad, w_ada[l], b_ada[l])[:batch]
        mod3 = mod.reshape(batch, 6, d)
        h = _layer(h, mod3, norm_pre_mix[l], norm_post_mix[l], norm_pre_ffn[l], norm_post_ffn[l], w_in[l],
                   nsa_cmp_pos[l], nsa_w_cmp_k[l], nsa_w_cmp_v[l], mlstm_gate_bias[l], mlstm_norm_w[l],
                   w_up_nsa[l], w_up_mlstm[l], w_out[l], peer_w_q[l], peer_sub_keys[l], peer_u[l], peer_v[l],
                   batch=batch, seq=seq)
    return h.reshape(batch, seq, d).astype(x.dtype)
```

```python
import functools
import math

import jax
import jax.numpy as jnp
from jax import lax
from jax.experimental import pallas as pl
from jax.experimental.pallas import tpu as pltpu

D_MODEL = 4096
BATCH = 4
SEQ = 2048
NSA_HEADS = 16
NSA_KV_GROUPS = 4
NSA_HEAD_DIM = 128
CMP_BLOCK = 32
CMP_STRIDE = 16
SLC_BLOCK = 64
SLC_TOPN = 16
WINDOW = 512
ML_HEADS = 8
ML_QK_DIM = 256
ML_V_DIM = 512
PEER_HEADS = 8
PEER_KEY_DIM = 256
PEER_N_KEYS = 128
PEER_TOPK = 16

NEG_INF = -1e30
BIG = 1e9
TINY = 1e-30
EPS = 1e-6

F32 = jnp.float32
BF16 = jnp.bfloat16

V7X_VMEM_LIMIT_BYTES = 56 * 1024 * 1024
LANES = 128
SUBLANES = 8

ROW_TILE = 256
MM_BM = 1024
MM_BN = 1024
MMW_BN = 512
MMA_BM = 512
ADA_BN = 512
NSA_TQ = 256
NSA_TK = 512
ML_CHUNK_LEN = 128
ML_HEADS_PER_STEP = 2
PEER_TOPK_TT = 256
PEER_TT = 512
PEER_TE = 1024
PEER_DOT_CHUNK = 256

_NT = (((1,), (1,)), ((), ()))


def _params(*sem):
    return pltpu.CompilerParams(dimension_semantics=sem, vmem_limit_bytes=V7X_VMEM_LIMIT_BYTES)


def _ind(cond, dtype):
    wide = jnp.int32 if jnp.issubdtype(dtype, jnp.integer) else F32
    return jnp.where(cond, jnp.ones((), wide), jnp.zeros((), wide)).astype(dtype)


def _log2(n):
    k = int(math.log2(n))
    assert 1 << k == n
    return k


def _split3(x):
    hi = x.astype(BF16)
    r1 = x - hi.astype(F32)
    mid = r1.astype(BF16)
    lo = (r1 - mid.astype(F32)).astype(BF16)
    return hi, mid, lo


def _adaln_kernel(c_ref, w_ref, b_ref, o_ref):
    c = c_ref[...]
    cond = (c * jax.nn.sigmoid(c)).astype(BF16)
    o_ref[...] = jnp.dot(cond, w_ref[...].astype(BF16), preferred_element_type=F32) + b_ref[...]


def _adaln(c_pad, w_ada, b_ada):
    rows, d = c_pad.shape
    n = w_ada.shape[1]
    bn = min(ADA_BN, n)
    return pl.pallas_call(
        _adaln_kernel,
        grid=(n // bn,),
        in_specs=[
            pl.BlockSpec((rows, d), lambda j: (0, 0)),
            pl.BlockSpec((d, bn), lambda j: (0, j)),
            pl.BlockSpec((1, bn), lambda j: (0, j)),
        ],
        out_specs=pl.BlockSpec((rows, bn), lambda j: (0, j)),
        out_shape=jax.ShapeDtypeStruct((rows, n), F32),
        compiler_params=_params("parallel"),
        name="adaln",
    )(c_pad, w_ada, b_ada.reshape(1, n))


def _rms(x, w):
    return x * lax.rsqrt(jnp.mean(x * x, axis=-1, keepdims=True) + EPS) * w


def _prenorm_kernel(x_ref, w_ref, mod_ref, u_ref):
    y = _rms(x_ref[...], w_ref[...])
    u_ref[...] = (y * (1.0 + mod_ref[1:2, :]) + mod_ref[0:1, :]).astype(u_ref.dtype)


def _prenorm(x2, w, mod3, seq):
    n, d = x2.shape
    tr = min(ROW_TILE, seq)
    return pl.pallas_call(
        _prenorm_kernel,
        grid=(n // tr,),
        in_specs=[
            pl.BlockSpec((tr, d), lambda i: (i, 0)),
            pl.BlockSpec((1, d), lambda i: (0, 0)),
            pl.BlockSpec((None, 6, d), lambda i: ((i * tr) // seq, 0, 0)),
        ],
        out_specs=pl.BlockSpec((tr, d), lambda i: (i, 0)),
        out_shape=jax.ShapeDtypeStruct((n, d), BF16),
        compiler_params=_params("parallel"),
        name="prenorm",
    )(x2, w.reshape(1, d), mod3)


def _midnorm_kernel(x_ref, y_ref, w1_ref, w2_ref, mod_ref, h_ref, u_ref, ut_ref):
    h = x_ref[...] + mod_ref[2:3, :] * _rms(y_ref[...].astype(F32), w1_ref[...])
    h_ref[...] = h
    u = _rms(h, w2_ref[...]) * (1.0 + mod_ref[4:5, :]) + mod_ref[3:4, :]
    u_ref[...] = u.astype(u_ref.dtype)
    ut_ref[...] = u.T.astype(ut_ref.dtype)


def _midnorm(x2, y, w_post, w_pre, mod3, seq):
    n, d = x2.shape
    tr = min(ROW_TILE, seq)
    row = pl.BlockSpec((tr, d), lambda i: (i, 0))
    vec = pl.BlockSpec((1, d), lambda i: (0, 0))
    return pl.pallas_call(
        _midnorm_kernel,
        grid=(n // tr,),
        in_specs=[row, row, vec, vec, pl.BlockSpec((None, 6, d), lambda i: ((i * tr) // seq, 0, 0))],
        out_specs=[row, row, pl.BlockSpec((d, tr), lambda i: (0, i))],
        out_shape=[jax.ShapeDtypeStruct((n, d), F32), jax.ShapeDtypeStruct((n, d), BF16),
                   jax.ShapeDtypeStruct((d, n), BF16)],
        compiler_params=_params("parallel"),
        name="midnorm",
    )(x2, y, w_post.reshape(1, d), w_pre.reshape(1, d), mod3)


def _finalnorm_kernel(h_ref, y_ref, w_ref, mod_ref, o_ref):
    o_ref[...] = h_ref[...] + mod_ref[5:6, :] * _rms(y_ref[...].astype(F32), w_ref[...])


def _finalnorm(h1, y, w, mod3, seq):
    n, d = h1.shape
    tr = min(ROW_TILE, seq)
    row = pl.BlockSpec((tr, d), lambda i: (i, 0))
    return pl.pallas_call(
        _finalnorm_kernel,
        grid=(n // tr,),
        in_specs=[row, row, pl.BlockSpec((1, d), lambda i: (0, 0)),
                  pl.BlockSpec((None, 6, d), lambda i: ((i * tr) // seq, 0, 0))],
        out_specs=row,
        out_shape=jax.ShapeDtypeStruct((n, d), F32),
        compiler_params=_params("parallel"),
        name="finalnorm",
    )(h1, y, w.reshape(1, d), mod3)


def _gelu(x):
    return 0.5 * x * (1.0 + lax.erf(x * (2.0 ** -0.5)))


def _mm_kernel(*refs, has_gate, has_add, gelu):
    a_ref, b_ref = refs[0], refs[1]
    o_ref = refs[-1]
    acc = jnp.dot(a_ref[...], b_ref[...], preferred_element_type=F32)
    nxt = 2
    if has_gate:
        acc = acc * jax.nn.sigmoid(refs[nxt][...].astype(F32))
        nxt += 1
    if has_add:
        acc = acc + refs[nxt][...].astype(F32)
    if gelu:
        acc = _gelu(acc)
    o_ref[...] = acc.astype(o_ref.dtype)


def _matmul(a, b, out_dtype, gate=None, gate_col=0, add=None, gelu=False):
    m, k = a.shape
    n = b.shape[1]
    bm = min(MM_BM, m)
    bn = min(MM_BN, n)
    in_specs = [pl.BlockSpec((bm, k), lambda i, j: (i, 0)), pl.BlockSpec((k, bn), lambda i, j: (0, j))]
    args = [a, b]
    if gate is not None:
        goff = gate_col // bn
        assert goff * bn == gate_col
        in_specs.append(pl.BlockSpec((bm, bn), lambda i, j: (i, goff + j)))
        args.append(gate)
    if add is not None:
        in_specs.append(pl.BlockSpec((bm, bn), lambda i, j: (i, j)))
        args.append(add)
    return pl.pallas_call(
        functools.partial(_mm_kernel, has_gate=gate is not None, has_add=add is not None, gelu=gelu),
        grid=(m // bm, n // bn),
        in_specs=in_specs,
        out_specs=pl.BlockSpec((bm, bn), lambda i, j: (i, j)),
        out_shape=jax.ShapeDtypeStruct((m, n), out_dtype),
        compiler_params=_params("parallel", "arbitrary"),
        name="matmul",
    )(*args)


def _mmw_kernel(*refs, shifts, has_gate, has_add, row_chunk):
    a_ref, w_ref = refs[0], refs[1]
    nxt = 2
    wn_ref = None
    if shifts is not None:
        wn_ref = refs[nxt]
        nxt += 1
    o_ref, wb_ref = refs[-2], refs[-1]
    j = pl.program_id(0)
    k, bn = wb_ref.shape

    def fill(shift):
        for r0 in range(0, k, row_chunk):
            rs = slice(r0, r0 + row_chunk)
            if shift == 0:
                wb_ref[rs, :] = w_ref[rs, :].astype(BF16)
            else:
                wide = jnp.concatenate([w_ref[rs, :], wn_ref[rs, :]], axis=1)
                wb_ref[rs, :] = wide[:, shift:shift + bn].astype(BF16)

    @pl.when(pl.program_id(1) == 0)
    def _():
        if shifts is None:
            fill(0)
        else:
            for j0, j1, shift in shifts:
                pl.when((j >= j0) & (j < j1))(functools.partial(fill, shift))

    acc = jnp.dot(a_ref[...], wb_ref[...], preferred_element_type=F32)
    if has_gate:
        acc = acc * jax.nn.sigmoid(refs[nxt][...].astype(F32))
        nxt += 1
    if has_add:
        acc = acc + refs[nxt][...].astype(F32)
    o_ref[...] = acc.astype(o_ref.dtype)


def _matmul_w(a, w, out_dtype, *, n_out=None, shifts=None, gate=None, gate_col=0, add=None):
    m, k = a.shape
    n = w.shape[1] if n_out is None else n_out
    bm = min(MM_BM, m)
    bn = min(MMW_BN, n)
    assert n % bn == 0 and m % bm == 0
    in_specs = [pl.BlockSpec((bm, k), lambda j, i: (i, 0)), pl.BlockSpec((k, bn), lambda j, i: (0, j))]
    args = [a, w]
    if shifts is not None:
        per = bn // LANES
        in_specs.append(pl.BlockSpec((k, LANES), lambda j, i: (0, (j + 1) * per)))
        args.append(w)
    if gate is not None:
        goff = gate_col // bn
        assert goff * bn == gate_col
        in_specs.append(pl.BlockSpec((bm, bn), lambda j, i: (i, goff + j)))
        args.append(gate)
    if add is not None:
        in_specs.append(pl.BlockSpec((bm, bn), lambda j, i: (i, j)))
        args.append(add)
    return pl.pallas_call(
        functools.partial(_mmw_kernel, shifts=shifts, has_gate=gate is not None, has_add=add is not None,
                          row_chunk=min(512, k)),
        grid=(n // bn, m // bm),
        in_specs=in_specs,
        out_specs=pl.BlockSpec((bm, bn), lambda j, i: (i, j)),
        out_shape=jax.ShapeDtypeStruct((m, n), out_dtype),
        scratch_shapes=[pltpu.VMEM((k, bn), BF16)],
        compiler_params=_params("parallel", "arbitrary"),
        name="matmul_w",
    )(*args)


def _inproj_kernel(a_ref, wt_ref, wtn_ref, o_ref, wb_ref, *, shifts, col_chunk):
    j = pl.program_id(0)
    k, bn = wb_ref.shape

    def fill(shift):
        for c0 in range(0, k, col_chunk):
            cs = slice(c0, c0 + col_chunk)
            if shift == 0:
                blk = wt_ref[:, cs]
            else:
                tall = jnp.concatenate([wt_ref[:, cs], wtn_ref[:, cs]], axis=0)
                blk = tall[shift:shift + bn, :]
            wb_ref[cs, :] = blk.T.astype(BF16)

    @pl.when(pl.program_id(1) == 0)
    def _():
        for j0, j1, shift in shifts:
            pl.when((j >= j0) & (j < j1))(functools.partial(fill, shift))

    acc = jnp.dot(a_ref[...], wb_ref[...], preferred_element_type=F32)
    o_ref[...] = acc.astype(o_ref.dtype)


def _inproj(a, wt, n_out, shifts):
    m, k = a.shape
    bm = min(MM_BM, m)
    bn = MMW_BN
    assert n_out % bn == 0 and m % bm == 0 and all(s % SUBLANES == 0 and s <= LANES for _, _, s in shifts)
    per = bn // LANES
    return pl.pallas_call(
        functools.partial(_inproj_kernel, shifts=shifts, col_chunk=min(512, k)),
        grid=(n_out // bn, m // bm),
        in_specs=[pl.BlockSpec((bm, k), lambda j, i: (i, 0)),
                  pl.BlockSpec((bn, k), lambda j, i: (j, 0)),
                  pl.BlockSpec((LANES, k), lambda j, i: ((j + 1) * per, 0))],
        out_specs=pl.BlockSpec((bm, bn), lambda j, i: (i, j)),
        out_shape=jax.ShapeDtypeStruct((m, n_out), BF16),
        scratch_shapes=[pltpu.VMEM((k, bn), BF16)],
        compiler_params=_params("parallel", "arbitrary"),
        name="inproj",
    )(a, wt, wt)


def _gate_proj_kernel(a_ref, wa_ref, wb_ref, o_ref, *, n_a, off_b, n_b):
    k = wa_ref.shape[1]
    rows = jnp.concatenate([wa_ref[0:n_a, :], wb_ref[off_b:off_b + n_b, :],
                            jnp.zeros((LANES - n_a - n_b, k), F32)], axis=0).astype(BF16)
    o_ref[...] = lax.dot_general(a_ref[...], rows, _NT, preferred_element_type=F32)


def _gate_proj(a, wt, row_a, n_a, row_b, n_b):
    m, k = a.shape
    bm = min(MM_BM, m)
    off_b = row_b % LANES
    assert row_a % LANES == 0 and n_a % SUBLANES == 0 and off_b % SUBLANES == 0 and n_b % SUBLANES == 0
    assert off_b + n_b <= LANES and n_a + n_b <= LANES
    return pl.pallas_call(
        functools.partial(_gate_proj_kernel, n_a=n_a, off_b=off_b, n_b=n_b),
        grid=(m // bm,),
        in_specs=[pl.BlockSpec((bm, k), lambda i: (i, 0)),
                  pl.BlockSpec((LANES, k), lambda i: (row_a // LANES, 0)),
                  pl.BlockSpec((LANES, k), lambda i: (row_b // LANES, 0))],
        out_specs=pl.BlockSpec((bm, LANES), lambda i: (i, 0)),
        out_shape=jax.ShapeDtypeStruct((m, LANES), F32),
        compiler_params=_params("parallel"),
        name="gate_proj",
    )(a, wt, wt)


def _mma_kernel(a_ref, b_ref, o_ref, ab_ref, *, gelu, row_chunk):
    @pl.when(pl.program_id(1) == 0)
    def _():
        for r0 in range(0, ab_ref.shape[0], row_chunk):
            ab_ref[r0:r0 + row_chunk, :] = a_ref[r0:r0 + row_chunk, :].astype(BF16)

    acc = jnp.dot(ab_ref[...], b_ref[...], preferred_element_type=F32)
    if gelu:
        acc = _gelu(acc)
    o_ref[...] = acc.astype(o_ref.dtype)


def _matmul_a(a, b, out_dtype, gelu=False):
    m, k = a.shape
    n = b.shape[1]
    bm = min(MMA_BM, m)
    bn = min(MM_BN, n)
    return pl.pallas_call(
        functools.partial(_mma_kernel, gelu=gelu, row_chunk=min(128, bm)),
        grid=(m // bm, n // bn),
        in_specs=[pl.BlockSpec((bm, k), lambda i, j: (i, 0)), pl.BlockSpec((k, bn), lambda i, j: (0, j))],
        out_specs=pl.BlockSpec((bm, bn), lambda i, j: (i, j)),
        out_shape=jax.ShapeDtypeStruct((m, n), out_dtype),
        scratch_shapes=[pltpu.VMEM((bm, k), BF16)],
        compiler_params=_params("parallel", "arbitrary"),
        name="matmul_a",
    )(a, b)


def _pos_columns(pos, width):
    lane = lax.broadcasted_iota(jnp.int32, (pos.shape[0], width), 1)
    coarse = (pos >> 6) << 6
    vals = jnp.where(lane < 3, coarse, jnp.where(lane < 6, pos & 63, 0))
    return vals.astype(F32).astype(BF16)


def _nsa_kernel(slopes_ref, q_ref, kcm_ref, vcm_ref, ks_ref, vs_ref, kw_ref, vw_ref, gz_ref, wck_ref, wcv_ref,
                pos_ref, o_ref, kc_ref, vc_ref, pext_ref, xs_ref, acc_ref, qext_ref, *, seq, tq, tk,
                heads_per_group):
    R = heads_per_group
    dk = NSA_HEAD_DIM
    g = pl.program_id(1)
    qi = pl.program_id(2)
    t0 = qi * tq
    n_cmp = (seq - CMP_BLOCK) // CMP_STRIDE + 1
    n_slc = seq // SLC_BLOCK
    n_sel = min(SLC_TOPN, n_slc)
    half = CMP_STRIDE * dk
    scale = dk ** -0.5
    c_exp = scale * math.log2(math.e)
    rows = R * tq

    @pl.when(qi == 0)
    def _():
        pext_ref[...] = _pos_columns(lax.broadcasted_iota(jnp.int32, (seq, 1), 0), LANES)
        cpos = lax.broadcasted_iota(jnp.int32, (LANES, 1), 0) * CMP_STRIDE + (CMP_BLOCK - 1)
        for src_ref, w_ref, dst, pi in ((kcm_ref, wck_ref, kc_ref, 0), (vcm_ref, wcv_ref, vc_ref, 1)):
            xs_ref[...] = src_ref[...].astype(F32)
            x = jnp.concatenate([xs_ref[pl.ds(l, LANES, stride=CMP_STRIDE), :].astype(BF16)
                                 for l in range(CMP_STRIDE)], axis=1)
            ya = jnp.dot(x, w_ref[0:half, :], preferred_element_type=F32)
            yb = jnp.dot(x, w_ref[half:2 * half, :], preferred_element_type=F32)
            pos = jnp.broadcast_to(pos_ref[pi], (SUBLANES, 2 * half)).astype(BF16)
            pb = jnp.dot(pos, w_ref[...], preferred_element_type=F32)[0:1, :]
            comp = (ya + pltpu.roll(yb, LANES - 1, 0) + pb).astype(BF16)
            dst[...] = jnp.concatenate([comp, _pos_columns(cpos, LANES)], axis=1) if pi == 0 else comp

        head_of_row = lax.broadcasted_iota(jnp.int32, (rows, LANES), 0) >> _log2(tq)
        lane6 = lax.broadcasted_iota(jnp.int32, (rows, LANES), 1)
        term = jnp.where(lane6 < 3, lane6, lane6 - 3)
        q_ext = jnp.zeros((rows, LANES), F32)
        for r in range(R):
            for k in range(3):
                q_ext = jnp.where((head_of_row == r) & (term == k) & (lane6 < 6), slopes_ref[g, r, k], q_ext)
        qext_ref[...] = q_ext.astype(BF16)

    row_id = lax.broadcasted_iota(jnp.int32, (rows, 1), 0)
    t_col = t0 + (row_id & (tq - 1))
    q = q_ref[...]
    qs = jnp.concatenate([q[:, r * dk:(r + 1) * dk] for r in range(R)], axis=0)
    qa = jnp.concatenate([qs, qext_ref[...]], axis=1)

    n_idx = lax.broadcasted_iota(jnp.int32, (rows, LANES), 1)
    cvalid = (t_col >= n_idx * CMP_STRIDE + CMP_BLOCK - 1) & (n_idx < n_cmp)
    sc = lax.dot_general(qa, kc_ref[...], _NT, preferred_element_type=F32)
    sc = jnp.where(cvalid, sc, NEG_INF)
    mc = jnp.max(sc, axis=-1, keepdims=True)
    p_cmp = jnp.where(cvalid, jnp.exp2((sc - mc) * c_exp), 0.0)
    p_cmp = p_cmp * (1.0 / jnp.maximum(jnp.sum(p_cmp, axis=-1, keepdims=True), TINY))
    o_cmp = jnp.dot(p_cmp.astype(BF16), vc_ref[...], preferred_element_type=F32)

    psum = p_cmp[0:tq]
    for r in range(1, R):
        psum = psum + p_cmp[r * tq:(r + 1) * tq]
    jj = lax.broadcasted_iota(jnp.int32, (LANES, LANES), 0)
    nn = lax.broadcasted_iota(jnp.int32, (LANES, LANES), 1)
    ovl = _ind((nn * CMP_STRIDE < jj * SLC_BLOCK + SLC_BLOCK) & (nn * CMP_STRIDE + CMP_BLOCK > jj * SLC_BLOCK)
               & (jj < n_slc) & (nn < n_cmp), BF16)
    imp_t = jnp.zeros((LANES, tq), F32)
    for part in _split3(psum):
        imp_t = imp_t + lax.dot_general(ovl, part, _NT, preferred_element_type=F32)
    imp_t = imp_t[0:n_slc]
    j_i = lax.broadcasted_iota(jnp.int32, (n_slc, tq), 0)
    t_i = t0 + lax.broadcasted_iota(jnp.int32, (n_slc, tq), 1)
    cur = t_i >> _log2(SLC_BLOCK)
    causal = j_i * SLC_BLOCK <= t_i
    forced = (j_i == 0) | (j_i == cur) | (j_i == cur - 1)
    score = jnp.where(causal, jnp.where(forced, BIG, imp_t), -BIG)
    rank = jnp.zeros((n_slc, tq), jnp.int32)
    for jp in range(n_slc):
        row = score[jp:jp + 1, :]
        beats = (row > score) | ((row == score) & (j_i > jp))
        rank = rank + _ind(beats, jnp.int32)
    sel_t = jnp.where(rank < n_sel, 0.0, NEG_INF)
    sel_t = jnp.concatenate([sel_t, jnp.zeros((LANES - n_slc, tq), F32)], axis=0)
    sel_neg = sel_t.T.astype(BF16)
    tq_col = t0 + lax.broadcasted_iota(jnp.int32, (tq, 1), 0)

    def attend(s, v):
        p = jnp.exp2((s - jnp.max(s, axis=-1, keepdims=True)) * c_exp).astype(BF16)
        ones = _ind(lax.broadcasted_iota(jnp.int32, (v.shape[0], LANES), 1) == 0, BF16)
        pv = jnp.dot(p, jnp.concatenate([v, ones], axis=1), preferred_element_type=F32)
        return pv[:, 0:dk] * (1.0 / jnp.maximum(pv[:, dk:dk + 1], TINY))

    def slc_variant(nk):
        ka = jnp.concatenate([ks_ref[0:nk, :], pext_ref[0:nk, :]], axis=1)
        ej = lax.broadcasted_iota(jnp.int32, (LANES, nk), 0)
        ec = lax.broadcasted_iota(jnp.int32, (LANES, nk), 1)
        expand = _ind(ej == (ec >> _log2(SLC_BLOCK)), BF16)
        madd = jnp.dot(sel_neg, expand, preferred_element_type=F32)
        madd = madd + jnp.where(lax.broadcasted_iota(jnp.int32, (tq, nk), 1) <= tq_col, 0.0, NEG_INF)
        s = lax.dot_general(qa, ka, _NT, preferred_element_type=F32) + jnp.concatenate([madd] * R, axis=0)
        acc_ref[...] = attend(s, vs_ref[0:nk, :])

    variant = (t0 + tq - 1) // tk
    for vi in range(seq // tk):
        pl.when(variant == vi)(functools.partial(slc_variant, (vi + 1) * tk))
    o_slc = acc_ref[...]

    span = WINDOW + tq
    ws = pl.multiple_of(jnp.maximum(t0 - WINDOW, 0), tq)
    kwa = jnp.concatenate([kw_ref[pl.ds(ws, span), :], pext_ref[pl.ds(ws, span), :]], axis=1)
    dw_i = tq_col - (ws + lax.broadcasted_iota(jnp.int32, (tq, span), 1))
    wadd = jnp.where((dw_i >= 0) & (dw_i < WINDOW), 0.0, NEG_INF)
    sw = lax.dot_general(qa, kwa, _NT, preferred_element_type=F32) + jnp.concatenate([wadd] * R, axis=0)
    o_win = attend(sw, vw_ref[pl.ds(ws, span), :])

    gz = gz_ref[...]
    gsel = jnp.zeros((tq, 3 * R), F32)
    for gg in range(NSA_KV_GROUPS):
        gsel = jnp.where(g == gg, gz[:, 3 * R * gg:3 * R * (gg + 1)], gsel)
    gts = jax.nn.sigmoid(gsel)
    for r in range(R):
        sl = slice(r * tq, (r + 1) * tq)
        o = (gts[:, 3 * r:3 * r + 1] * o_cmp[sl] + gts[:, 3 * r + 1:3 * r + 2] * o_slc[sl]
             + gts[:, 3 * r + 2:3 * r + 3] * o_win[sl])
        o_ref[:, r * dk:(r + 1) * dk] = o.astype(o_ref.dtype)


def _nsa(slopes, z, gz, wck, wcv, pos, *, batch, seq, kv_col):
    G = NSA_KV_GROUPS
    R = NSA_HEADS // G
    dk = NSA_HEAD_DIM
    tq = NSA_TQ
    tk = NSA_TK
    nq = seq // tq
    assert seq // CMP_STRIDE == LANES and CMP_BLOCK == 2 * CMP_STRIDE and seq >= WINDOW + tq and seq % tk == 0
    kv0 = kv_col // dk

    def kv_spec(i):
        return pl.BlockSpec((seq, dk), lambda b, g, qi: (b, kv0 + i * G + g))

    w_spec = pl.BlockSpec((CMP_BLOCK * dk, dk), lambda b, g, qi: (0, 0))
    rows = R * tq
    assert dk == LANES and 3 * NSA_HEADS <= LANES
    return pl.pallas_call(
        functools.partial(_nsa_kernel, seq=seq, tq=tq, tk=tk, heads_per_group=R),
        grid=(batch, G, nq),
        in_specs=[
            pl.BlockSpec(memory_space=pltpu.SMEM),
            pl.BlockSpec((tq, R * dk), lambda b, g, qi: (b * nq + qi, g)),
            kv_spec(0), kv_spec(1), kv_spec(2), kv_spec(3), kv_spec(4), kv_spec(5),
            pl.BlockSpec((tq, LANES), lambda b, g, qi: (b * nq + qi, 0)),
            w_spec, w_spec,
            pl.BlockSpec((2, 1, CMP_BLOCK * dk), lambda b, g, qi: (0, 0, 0)),
        ],
        out_specs=pl.BlockSpec((tq, R * dk), lambda b, g, qi: (b * nq + qi, g)),
        out_shape=jax.ShapeDtypeStruct((batch * seq, NSA_HEADS * dk), BF16),
        scratch_shapes=[
            pltpu.VMEM((LANES, 2 * dk), BF16), pltpu.VMEM((LANES, dk), BF16),
            pltpu.VMEM((seq, LANES), BF16), pltpu.VMEM((seq, dk), F32), pltpu.VMEM((rows, dk), F32),
            pltpu.VMEM((rows, LANES), BF16),
        ],
        compiler_params=_params("parallel", "parallel", "arbitrary"),
        name="nsa",
    )(slopes, z, z, z, z, z, z, z, gz, wck, wcv, pos)


def _log_sigmoid(x):
    return jnp.minimum(x, 0.0) - jnp.log1p(jnp.exp(-jnp.abs(x)))


def _mlstm_kernel(bias_ref, q_ref, k_ref, v_ref, og_ref, gz_ref, nw_ref, o_ref, c_ref, *, seq, L, gate_lane, hps):
    dqk = ML_QK_DIM
    dv = ML_V_DIM
    nc = seq // L
    lane = lax.broadcasted_iota(jnp.int32, (L, LANES), 1)

    def gate_forms(blk, which, bias):
        col = jnp.sum(jnp.where(lane == which, blk, 0.0), axis=1, keepdims=True) + bias
        return col, jnp.broadcast_to(col, (L, LANES)).T[0:1, :]

    ext = dv + LANES
    ii = lax.broadcasted_iota(jnp.int32, (L, L), 0)
    kk = lax.broadcasted_iota(jnp.int32, (L, L), 1)
    causal = kk <= ii
    tri = _ind(causal, BF16)
    tri_t = _ind(ii <= kk, BF16)
    ones_slab = _ind(lax.broadcasted_iota(jnp.int32, (L, LANES), 1) == 0, BF16)
    c_ref[...] = jnp.zeros((hps, dqk, ext), F32)

    def cumsum_mats(lf_row, lf_col):
        lr = jnp.broadcast_to(lf_row, (L, L))
        lc = jnp.broadcast_to(lf_col, (L, L))
        b_row = jnp.zeros((L, L), F32)
        b_col = jnp.zeros((L, L), F32)
        for part in _split3(lr)[:2]:
            b_row = b_row + jnp.dot(part, tri_t, preferred_element_type=F32)
        for part in _split3(lc)[:2]:
            b_col = b_col + jnp.dot(tri, part, preferred_element_type=F32)
        return b_row, b_col

    def head_chunk(hh, r0, gblk, m):
        h_id = pl.program_id(1) * hps + hh
        i_lane = gate_lane + h_id
        f_lane = i_lane + ML_HEADS
        qc = q_ref[pl.ds(r0, L), hh * dqk:(hh + 1) * dqk]
        kc = k_ref[pl.ds(r0, L), hh * dqk:(hh + 1) * dqk] * (dqk ** -0.5)
        v_ext = jnp.concatenate([v_ref[pl.ds(r0, L), hh * dv:(hh + 1) * dv], ones_slab], axis=1)
        i_col, i_row = gate_forms(gblk, i_lane, bias_ref[0, h_id])
        f_col, f_row = gate_forms(gblk, f_lane, bias_ref[1, h_id])
        lf_col = _log_sigmoid(f_col)
        lf_row = _log_sigmoid(f_row)
        b_row, b_col = cumsum_mats(lf_row, lf_col)
        log_d = jnp.where(causal, b_col - b_row + i_row, NEG_INF)
        b_c = b_col[:, 0:1]
        m_inter = b_c + m
        m_t = jnp.maximum(m_inter, jnp.max(log_d, axis=-1, keepdims=True))
        d = jnp.exp(log_d - m_t)
        s = lax.dot_general(qc, kc, _NT, preferred_element_type=F32) * d
        w_inter = jnp.exp(m_inter - m_t)
        inter = jnp.dot(qc, c_ref[hh].astype(BF16), preferred_element_type=F32)
        intra = jnp.dot(s.astype(BF16), v_ext, preferred_element_type=F32)
        numden = w_inter * inter + intra
        den = numden[:, dv:dv + 1]
        h = numden[:, 0:dv] / jnp.maximum(jnp.abs(den), jnp.exp(-m_t))
        g_tot = b_row[0:1, L - 1:L]
        a_row = g_tot - b_row[0:1, :] + i_row
        a_col = g_tot - b_c + i_col
        m_new = jnp.maximum(g_tot + m, jnp.max(a_row, axis=-1, keepdims=True))
        decay = jnp.exp(g_tot + m - m_new)
        w_tok = jnp.exp(a_col - m_new)
        kw_t = (kc.astype(F32) * w_tok).T.astype(BF16)
        c_ref[hh] = decay * c_ref[hh] + jnp.dot(kw_t, v_ext, preferred_element_type=F32)
        hn = _rms(h, nw_ref[hh]) * jax.nn.sigmoid(og_ref[pl.ds(r0, L), hh * dv:(hh + 1) * dv].astype(F32))
        o_ref[pl.ds(r0, L), hh * dv:(hh + 1) * dv] = hn.astype(o_ref.dtype)
        return m_new

    def body(c, ms):
        r0 = pl.multiple_of(c * L, L)
        gblk = gz_ref[pl.ds(r0, L), :]
        return tuple(head_chunk(hh, r0, gblk, ms[hh]) for hh in range(hps))

    lax.fori_loop(0, nc, body, tuple(jnp.zeros((1, 1), F32) for _ in range(hps)))


def _mlstm(gate_bias, z, gz, norm_w, *, batch, seq, q_col, k_col, v_col, o_col, gate_lane):
    H = ML_HEADS
    dqk = ML_QK_DIM
    dv = ML_V_DIM
    L = min(ML_CHUNK_LEN, seq)
    hps = ML_HEADS_PER_STEP
    qw, vw = hps * dqk, hps * dv
    assert L == LANES and gate_lane + 2 * H <= LANES and H % hps == 0
    assert q_col % qw == 0 and k_col % qw == 0 and v_col % vw == 0 and o_col % vw == 0
    return pl.pallas_call(
        functools.partial(_mlstm_kernel, seq=seq, L=L, gate_lane=gate_lane, hps=hps),
        grid=(batch, H // hps),
        in_specs=[
            pl.BlockSpec(memory_space=pltpu.SMEM),
            pl.BlockSpec((seq, qw), lambda b, h: (b, q_col // qw + h)),
            pl.BlockSpec((seq, qw), lambda b, h: (b, k_col // qw + h)),
            pl.BlockSpec((seq, vw), lambda b, h: (b, v_col // vw + h)),
            pl.BlockSpec((seq, vw), lambda b, h: (b, o_col // vw + h)),
            pl.BlockSpec((seq, LANES), lambda b, h: (b, 0)),
            pl.BlockSpec((hps, 1, dv), lambda b, h: (h, 0, 0)),
        ],
        out_specs=pl.BlockSpec((seq, vw), lambda b, h: (b, h)),
        out_shape=jax.ShapeDtypeStruct((batch * seq, H * dv), BF16),
        scratch_shapes=[pltpu.VMEM((hps, dqk, dv + LANES), F32)],
        compiler_params=_params("parallel", "parallel"),
        name="mlstm",
    )(gate_bias, z, z, z, z, gz, norm_w.reshape(H, 1, dv))


def _peer_pairs():
    return [(a, (PEER_TOPK + 1) // (a + 1)) for a in range(PEER_TOPK + 1)]


def _topk_multiset(s, k, with_rank=False):
    tt = s.shape[1]
    riota = lax.broadcasted_iota(jnp.int32, (k, tt), 0).astype(F32)
    v = jnp.full((k, tt), -jnp.inf, F32)
    taken = jnp.zeros((1, tt), F32)
    rank = jnp.full(s.shape, float(s.shape[0]), F32)
    rem = s
    for _ in range(k):
        m = jnp.max(rem, axis=0, keepdims=True)
        eq = rem == m
        cnt = jnp.sum(_ind(eq, F32), axis=0, keepdims=True)
        v = jnp.where((riota >= taken) & (riota < taken + cnt), m, v)
        if with_rank:
            rank = jnp.where(eq, taken, rank)
        taken = taken + cnt
        rem = jnp.where(eq, -jnp.inf, rem)
    return (v, rank) if with_rank else v


def _peer_topk_kernel(q_ref, khi_ref, klo_ref, cnt_ref, rank_ref, e1_ref, e2_ref, cand_ref):
    H = PEER_HEADS
    half = PEER_KEY_DIM // 2
    K = PEER_TOPK
    pairs = _peer_pairs()
    n_cand = sum(nb for _, nb in pairs)
    tt = q_ref.shape[0]
    for h in range(H):
        s_parts = []
        for side in range(2):
            qf = q_ref[:, (2 * h + side) * half:(2 * h + side + 1) * half]
            q_hi = qf.astype(BF16)
            q_lo = (qf - q_hi.astype(F32)).astype(BF16)
            k_hi = khi_ref[side, h]
            k_lo = klo_ref[side, h]
            s = (lax.dot_general(k_hi, q_hi, _NT, preferred_element_type=F32)
                 + lax.dot_general(k_hi, q_lo, _NT, preferred_element_type=F32)
                 + lax.dot_general(k_lo, q_hi, _NT, preferred_element_type=F32))
            s_parts.append(s)
        s1, s2 = s_parts
        v1 = _topk_multiset(s1, K + 1)
        v2, rank2 = _topk_multiset(s2, K + 1, with_rank=True)
        off = 0
        for a, nb in pairs:
            cand_ref[off:off + nb, :] = v1[a:a + 1, :] + v2[0:nb, :]
            off += nb
        cand_ref[n_cand:, :] = jnp.full((cand_ref.shape[0] - n_cand, tt), -jnp.inf, F32)
        rem = cand_ref[...]
        taken = jnp.zeros((1, tt), F32)
        tau16 = jnp.zeros((1, tt), F32)
        tau17 = jnp.zeros((1, tt), F32)
        zsum = jnp.zeros((1, tt), F32)
        cmax = v1[0:1, :] + v2[0:1, :]
        for _ in range(K + 1):
            m = jnp.max(rem, axis=0, keepdims=True)
            eq = rem == m
            cnt = jnp.sum(_ind(eq, F32), axis=0, keepdims=True)
            take = jnp.minimum(cnt, jnp.maximum(K - taken, 0.0))
            zsum = zsum + take * jnp.exp(m - cmax)
            after = taken + cnt
            tau16 = jnp.where((taken < K) & (after >= K), m, tau16)
            tau17 = jnp.where((taken <= K) & (after > K), m, tau17)
            taken = after
            rem = jnp.where(eq, -jnp.inf, rem)
        theta = 0.5 * (tau16 + tau17)
        r1 = theta - s1
        cnt1 = jnp.zeros(r1.shape, F32)
        for b in range(K + 1):
            cnt1 = cnt1 + _ind(v2[b:b + 1, :] >= r1, F32)
        cnt_ref[h] = cnt1
        rank_ref[h] = rank2.astype(rank_ref.dtype)
        e1_ref[h] = jnp.exp(s1 - v1[0:1, :])
        e2_ref[h] = (jnp.exp(s2 - v2[0:1, :]) / zsum).astype(e2_ref.dtype)


def _peer_topk(pq, k_hi, k_lo):
    n = pq.shape[0]
    H = PEER_HEADS
    nk = PEER_N_KEYS
    tt = min(PEER_TOPK_TT, n)
    n_cand = sum(nb for _, nb in _peer_pairs())
    cand_rows = -(-n_cand // SUBLANES) * SUBLANES
    key_spec = pl.BlockSpec((2, H, nk, PEER_KEY_DIM // 2), lambda i: (0, 0, 0, 0))
    hkt = pl.BlockSpec((H, nk, tt), lambda i: (0, 0, i))
    wide = jax.ShapeDtypeStruct((H, nk, n), F32)
    narrow = jax.ShapeDtypeStruct((H, nk, n), BF16)
    return pl.pallas_call(
        _peer_topk_kernel,
        grid=(n // tt,),
        in_specs=[pl.BlockSpec((tt, pq.shape[1]), lambda i: (i, 0)), key_spec, key_spec],
        out_specs=[hkt, hkt, hkt, hkt],
        out_shape=[wide, narrow, wide, narrow],
        scratch_shapes=[pltpu.VMEM((cand_rows, tt), F32)],
        compiler_params=_params("parallel"),
        name="peer_topk",
    )(pq, k_hi, k_lo)


def _bcast_packed(row):
    half = jnp.broadcast_to(row, (SUBLANES, LANES))
    return jnp.concatenate([half, half], axis=0).astype(BF16)


def _peer_mix_kernel(act_ref, vt_ref, cnt_ref, e1_ref, rank_ref, e2_ref, o_ref, acc_ref, a_ref, *, chunk):
    H = PEER_HEADS
    nk = PEER_N_KEYS
    te, tt = a_ref.shape
    e = pl.program_id(1)
    pack = 2 * SUBLANES

    @pl.when(e == 0)
    def _():
        acc_ref[...] = jnp.zeros(acc_ref.shape, F32)

    for i in range(te // nk):
        cnt_rows = [cnt_ref[h, i:i + 1, :] for h in range(H)]
        e1_rows = [e1_ref[h, i:i + 1, :] for h in range(H)]
        for lt in range(tt // LANES):
            ls = slice(lt * LANES, (lt + 1) * LANES)
            cntb = [_bcast_packed(cnt_rows[h][:, ls]) for h in range(H)]
            e1b = [_bcast_packed(e1_rows[h][:, ls]) for h in range(H)]
            for sg in range(nk // pack):
                rs = slice(sg * pack, (sg + 1) * pack)
                w = jnp.zeros((pack, LANES), BF16)
                for h in range(H):
                    hit = rank_ref[h, rs, ls] < cntb[h]
                    w = w + e1b[h] * jnp.where(hit, e2_ref[h, rs, ls], jnp.zeros((), BF16))
                rr = slice(i * nk + sg * pack, i * nk + (sg + 1) * pack)
                a_ref[rr, ls] = act_ref[rr, ls] * w
        done = (i + 1) * nk
        if done % chunk == 0:
            ks = slice(done - chunk, done)
            acc_ref[...] += jnp.dot(vt_ref[:, ks], a_ref[ks, :], preferred_element_type=F32)

    @pl.when(e == pl.num_programs(1) - 1)
    def _():
        o_ref[...] = acc_ref[...].T.astype(o_ref.dtype)


def _peer_mix(act_t, pvt, cnt1, rank2, e1, e2):
    n_exp, n = act_t.shape
    d = pvt.shape[0]
    H = PEER_HEADS
    nk = PEER_N_KEYS
    tt = min(PEER_TT, n)
    te = PEER_TE
    assert te == SUBLANES * nk and n_exp % te == 0

    def i1_spec():
        return pl.BlockSpec((H, SUBLANES, tt), lambda i, e: (0, e, i))

    def i2_spec():
        return pl.BlockSpec((H, nk, tt), lambda i, e: (0, 0, i))

    return pl.pallas_call(
        functools.partial(_peer_mix_kernel, chunk=PEER_DOT_CHUNK),
        grid=(n // tt, n_exp // te),
        in_specs=[
            pl.BlockSpec((te, tt), lambda i, e: (e, i)),
            pl.BlockSpec((d, te), lambda i, e: (0, e)),
            i1_spec(), i1_spec(), i2_spec(), i2_spec(),
        ],
        out_specs=pl.BlockSpec((tt, d), lambda i, e: (i, 0)),
        out_shape=jax.ShapeDtypeStruct((n, d), BF16),
        scratch_shapes=[pltpu.VMEM((d, tt), F32), pltpu.VMEM((te, tt), BF16)],
        compiler_params=_params("parallel", "arbitrary"),
        name="peer_mix",
    )(act_t, pvt, cnt1, e1, rank2, e2)


def _cast_transpose_kernel(x_ref, o_ref):
    o_ref[...] = x_ref[...].T.astype(o_ref.dtype)


def _cast_transpose(x, dtype):
    r, c = x.shape
    tr = min(MMA_BM, r)
    return pl.pallas_call(
        _cast_transpose_kernel,
        grid=(r // tr,),
        in_specs=[pl.BlockSpec((tr, c), lambda i: (i, 0))],
        out_specs=pl.BlockSpec((c, tr), lambda i: (0, i)),
        out_shape=jax.ShapeDtypeStruct((c, r), dtype),
        compiler_params=_params("parallel"),
        name="cast_transpose",
    )(x)


def _layer(h, mod3, norm_pre_mix, norm_post_mix, norm_pre_ffn, norm_post_ffn, w_in, cmp_pos, w_cmp_k, w_cmp_v,
           gate_bias, ml_norm_w, w_up_nsa, w_up_mlstm, w_out, peer_w_q, sub_keys, peer_u, peer_v, *, batch, seq):
    d = h.shape[1]
    G = NSA_KV_GROUPS
    R = NSA_HEADS // G
    dk = NSA_HEAD_DIM
    H = ML_HEADS
    nsa_w = NSA_HEADS * dk
    kv_w = 6 * G * dk
    mq_w = H * ML_QK_DIM
    mv_w = H * ML_V_DIM
    c_kv = nsa_w
    c_gn = c_kv + kv_w
    c_mq = c_gn + 3 * NSA_HEADS
    c_mk = c_mq + mq_w
    c_mv = c_mk + mq_w
    c_mo = c_mv + mv_w
    c_if = c_mo + mv_w
    c_mg = c_if + 2 * H
    n_gate = 3 * NSA_HEADS + 2 * H
    z_kv = nsa_w
    z_mq = z_kv + kv_w
    z_mk = z_mq + mq_w
    z_mv = z_mk + mq_w
    z_mo = z_mv + mv_w
    z_mg = z_mo + mv_w
    z_w = z_mg + 2 * d
    assert z_mq % MMW_BN == 0 and z_mg % MMW_BN == 0 and z_w % MMW_BN == 0
    z_shifts = [(0, z_mq // MMW_BN, 0), (z_mq // MMW_BN, z_mg // MMW_BN, 3 * NSA_HEADS),
                (z_mg // MMW_BN, z_w // MMW_BN, n_gate)]

    u = _prenorm(h, norm_pre_mix, mod3, seq)
    w_in_t = w_in.T
    z = _inproj(u, w_in_t, z_w, z_shifts)
    gz = _gate_proj(u, w_in_t, c_gn, 3 * NSA_HEADS, c_if, 2 * H)

    slopes = jnp.exp2(-8.0 * jnp.arange(1, NSA_HEADS + 1, dtype=F32) / NSA_HEADS) / (dk ** -0.5)
    slopes = jnp.stack([t.astype(F32) for t in _split3(slopes)], axis=-1).reshape(G, R, 3)
    o_nsa = _nsa(slopes, z, gz, w_cmp_k.astype(BF16), w_cmp_v.astype(BF16),
                 cmp_pos.reshape(2, 1, CMP_BLOCK * dk), batch=batch, seq=seq, kv_col=z_kv)

    h_ml = _mlstm(gate_bias, z, gz, ml_norm_w, batch=batch, seq=seq, q_col=z_mq, k_col=z_mk, v_col=z_mv,
                  o_col=z_mo, gate_lane=3 * NSA_HEADS)

    t1 = _matmul_w(o_nsa, w_up_nsa, BF16, gate=z, gate_col=z_mg)
    mix = _matmul_w(h_ml, w_up_mlstm, BF16, gate=z, gate_col=z_mg + d, add=t1)
    y = _matmul_w(mix, w_out, BF16)
    h1, u2, u2_t = _midnorm(h, y, norm_post_mix, norm_pre_ffn, mod3, seq)

    pq = _matmul_w(u2, peer_w_q, F32)
    k_hi = sub_keys.astype(BF16)
    k_lo = (sub_keys - k_hi.astype(F32)).astype(BF16)
    cnt1, rank2, e1, e2 = _peer_topk(pq, k_hi, k_lo)
    act_t = _matmul_a(peer_u, u2_t, BF16, gelu=True)
    y2 = _peer_mix(act_t, _cast_transpose(peer_v, BF16), cnt1, rank2, e1, e2)
    return _finalnorm(h1, y2, norm_post_ffn, mod3, seq)


def kernel(x, c, w_ada, b_ada, norm_pre_mix, norm_post_mix, norm_pre_ffn, norm_post_ffn, w_in, nsa_cmp_pos, nsa_w_cmp_k, nsa_w_cmp_v, mlstm_gate_bias, mlstm_norm_w, w_up_nsa, w_up_mlstm, w_out, peer_w_q, peer_sub_keys, peer_u, peer_v):
    batch, seq, d = x.shape
    depth = w_ada.shape[0]
    h = x.reshape(batch * seq, d)
    c_pad = jnp.concatenate([c, jnp.zeros((SUBLANES - batch % SUBLANES, d), c.dtype)], axis=0)
    for l in range(depth):
        mod = _adaln(c_pad, w_ada[l], b_ada[l])[:batch]
        mod3 = mod.reshape(batch, 6, d)
        h = _layer(h, mod3, norm_pre_mix[l], norm_post_mix[l], norm_pre_ffn[l], norm_post_ffn[l], w_in[l],
                   nsa_cmp_pos[l], nsa_w_cmp_k[l], nsa_w_cmp_v[l], mlstm_gate_bias[l], mlstm_norm_w[l],
                   w_up_nsa[l], w_up_mlstm[l], w_out[l], peer_w_q[l], peer_sub_keys[l], peer_u[l], peer_v[l],
                   batch=batch, seq=seq)
    return h.reshape(batch, seq, d).astype(x.dtype)
```

```python
import functools
import math

import jax
import jax.numpy as jnp
from jax import lax
from jax.experimental import pallas as pl
from jax.experimental.pallas import tpu as pltpu

D_MODEL = 4096
BATCH = 4
SEQ = 2048
NSA_HEADS = 16
NSA_KV_GROUPS = 4
NSA_HEAD_DIM = 128
CMP_BLOCK = 32
CMP_STRIDE = 16
SLC_BLOCK = 64
SLC_TOPN = 16
WINDOW = 512
ML_HEADS = 8
ML_QK_DIM = 256
ML_V_DIM = 512
PEER_HEADS = 8
PEER_KEY_DIM = 256
PEER_N_KEYS = 128
PEER_TOPK = 16

NEG_INF = -1e30
BIG = 1e9
TINY = 1e-30
EPS = 1e-6

F32 = jnp.float32
BF16 = jnp.bfloat16

V7X_VMEM_LIMIT_BYTES = 56 * 1024 * 1024
LANES = 128
SUBLANES = 8

ROW_TILE = 256
MM_BM = 1024
MM_BN = 1024
MMW_BN = 512
MMA_BM = 512
ADA_BN = 512
NSA_TQ = 256
NSA_TK = 512
ML_CHUNK_LEN = 256
ML_HEADS_PER_STEP = 2
PEER_TOPK_TT = 256
PEER_TT = 512
PEER_TE = 1024
PEER_DOT_CHUNK = 256

_NT = (((1,), (1,)), ((), ()))


def _params(*sem):
    return pltpu.CompilerParams(dimension_semantics=sem, vmem_limit_bytes=V7X_VMEM_LIMIT_BYTES)


def _ind(cond, dtype):
    wide = jnp.int32 if jnp.issubdtype(dtype, jnp.integer) else F32
    return jnp.where(cond, jnp.ones((), wide), jnp.zeros((), wide)).astype(dtype)


def _log2(n):
    k = int(math.log2(n))
    assert 1 << k == n
    return k


def _split3(x):
    hi = x.astype(BF16)
    r1 = x - hi.astype(F32)
    mid = r1.astype(BF16)
    lo = (r1 - mid.astype(F32)).astype(BF16)
    return hi, mid, lo


def _adaln_kernel(c_ref, w_ref, b_ref, o_ref):
    c = c_ref[...]
    cond = (c * jax.nn.sigmoid(c)).astype(BF16)
    o_ref[...] = jnp.dot(cond, w_ref[...].astype(BF16), preferred_element_type=F32) + b_ref[...]


def _adaln(c_pad, w_ada, b_ada):
    rows, d = c_pad.shape
    n = w_ada.shape[1]
    bn = min(ADA_BN, n)
    return pl.pallas_call(
        _adaln_kernel,
        grid=(n // bn,),
        in_specs=[
            pl.BlockSpec((rows, d), lambda j: (0, 0)),
            pl.BlockSpec((d, bn), lambda j: (0, j)),
            pl.BlockSpec((1, bn), lambda j: (0, j)),
        ],
        out_specs=pl.BlockSpec((rows, bn), lambda j: (0, j)),
        out_shape=jax.ShapeDtypeStruct((rows, n), F32),
        compiler_params=_params("parallel"),
        name="adaln",
    )(c_pad, w_ada, b_ada.reshape(1, n))


def _rms(x, w):
    return x * lax.rsqrt(jnp.mean(x * x, axis=-1, keepdims=True) + EPS) * w


def _prenorm_kernel(x_ref, w_ref, mod_ref, u_ref):
    y = _rms(x_ref[...], w_ref[...])
    u_ref[...] = (y * (1.0 + mod_ref[1:2, :]) + mod_ref[0:1, :]).astype(u_ref.dtype)


def _prenorm(x2, w, mod3, seq):
    n, d = x2.shape
    tr = min(ROW_TILE, seq)
    return pl.pallas_call(
        _prenorm_kernel,
        grid=(n // tr,),
        in_specs=[
            pl.BlockSpec((tr, d), lambda i: (i, 0)),
            pl.BlockSpec((1, d), lambda i: (0, 0)),
            pl.BlockSpec((None, 6, d), lambda i: ((i * tr) // seq, 0, 0)),
        ],
        out_specs=pl.BlockSpec((tr, d), lambda i: (i, 0)),
        out_shape=jax.ShapeDtypeStruct((n, d), BF16),
        compiler_params=_params("parallel"),
        name="prenorm",
    )(x2, w.reshape(1, d), mod3)


def _midnorm_kernel(x_ref, y_ref, w1_ref, w2_ref, mod_ref, h_ref, u_ref, ut_ref):
    h = x_ref[...] + mod_ref[2:3, :] * _rms(y_ref[...].astype(F32), w1_ref[...])
    h_ref[...] = h
    u = _rms(h, w2_ref[...]) * (1.0 + mod_ref[4:5, :]) + mod_ref[3:4, :]
    u_ref[...] = u.astype(u_ref.dtype)
    ut_ref[...] = u.T.astype(ut_ref.dtype)


def _midnorm(x2, y, w_post, w_pre, mod3, seq):
    n, d = x2.shape
    tr = min(ROW_TILE, seq)
    row = pl.BlockSpec((tr, d), lambda i: (i, 0))
    vec = pl.BlockSpec((1, d), lambda i: (0, 0))
    return pl.pallas_call(
        _midnorm_kernel,
        grid=(n // tr,),
        in_specs=[row, row, vec, vec, pl.BlockSpec((None, 6, d), lambda i: ((i * tr) // seq, 0, 0))],
        out_specs=[row, row, pl.BlockSpec((d, tr), lambda i: (0, i))],
        out_shape=[jax.ShapeDtypeStruct((n, d), F32), jax.ShapeDtypeStruct((n, d), BF16),
                   jax.ShapeDtypeStruct((d, n), BF16)],
        compiler_params=_params("parallel"),
        name="midnorm",
    )(x2, y, w_post.reshape(1, d), w_pre.reshape(1, d), mod3)


def _finalnorm_kernel(h_ref, y_ref, w_ref, mod_ref, o_ref):
    o_ref[...] = h_ref[...] + mod_ref[5:6, :] * _rms(y_ref[...].astype(F32), w_ref[...])


def _finalnorm(h1, y, w, mod3, seq):
    n, d = h1.shape
    tr = min(ROW_TILE, seq)
    row = pl.BlockSpec((tr, d), lambda i: (i, 0))
    return pl.pallas_call(
        _finalnorm_kernel,
        grid=(n // tr,),
        in_specs=[row, row, pl.BlockSpec((1, d), lambda i: (0, 0)),
                  pl.BlockSpec((None, 6, d), lambda i: ((i * tr) // seq, 0, 0))],
        out_specs=row,
        out_shape=jax.ShapeDtypeStruct((n, d), F32),
        compiler_params=_params("parallel"),
        name="finalnorm",
    )(h1, y, w.reshape(1, d), mod3)


def _gelu(x):
    return 0.5 * x * (1.0 + lax.erf(x * (2.0 ** -0.5)))


def _mmw_kernel(*refs, has_gate, has_add, row_chunk):
    a_ref, w_ref = refs[0], refs[1]
    nxt = 2
    o_ref, wb_ref = refs[-2], refs[-1]

    @pl.when(pl.program_id(1) == 0)
    def _():
        for r0 in range(0, wb_ref.shape[0], row_chunk):
            wb_ref[r0:r0 + row_chunk, :] = w_ref[r0:r0 + row_chunk, :].astype(BF16)

    acc = jnp.dot(a_ref[...], wb_ref[...], preferred_element_type=F32)
    if has_gate:
        acc = acc * jax.nn.sigmoid(refs[nxt][...].astype(F32))
        nxt += 1
    if has_add:
        acc = acc + refs[nxt][...].astype(F32)
    o_ref[...] = acc.astype(o_ref.dtype)


def _matmul_w(a, w, out_dtype, *, gate=None, gate_col=0, add=None):
    m, k = a.shape
    n = w.shape[1]
    bm = min(MM_BM, m)
    bn = min(MMW_BN, n)
    assert n % bn == 0 and m % bm == 0
    in_specs = [pl.BlockSpec((bm, k), lambda j, i: (i, 0)), pl.BlockSpec((k, bn), lambda j, i: (0, j))]
    args = [a, w]
    if gate is not None:
        goff = gate_col // bn
        assert goff * bn == gate_col
        in_specs.append(pl.BlockSpec((bm, bn), lambda j, i: (i, goff + j)))
        args.append(gate)
    if add is not None:
        in_specs.append(pl.BlockSpec((bm, bn), lambda j, i: (i, j)))
        args.append(add)
    return pl.pallas_call(
        functools.partial(_mmw_kernel, has_gate=gate is not None, has_add=add is not None, row_chunk=min(512, k)),
        grid=(n // bn, m // bm),
        in_specs=in_specs,
        out_specs=pl.BlockSpec((bm, bn), lambda j, i: (i, j)),
        out_shape=jax.ShapeDtypeStruct((m, n), out_dtype),
        scratch_shapes=[pltpu.VMEM((k, bn), BF16)],
        compiler_params=_params("parallel", "arbitrary"),
        name="matmul_w",
    )(*args)


def _inproj_kernel(a_ref, wt_ref, wtn_ref, o_ref, wb_ref, *, shifts, col_chunk):
    j = pl.program_id(0)
    k, bn = wb_ref.shape

    def fill(shift):
        for c0 in range(0, k, col_chunk):
            cs = slice(c0, c0 + col_chunk)
            if shift == 0:
                blk = wt_ref[:, cs]
            else:
                tall = jnp.concatenate([wt_ref[:, cs], wtn_ref[:, cs]], axis=0)
                blk = tall[shift:shift + bn, :]
            wb_ref[cs, :] = blk.T.astype(BF16)

    @pl.when(pl.program_id(1) == 0)
    def _():
        for j0, j1, shift in shifts:
            pl.when((j >= j0) & (j < j1))(functools.partial(fill, shift))

    acc = jnp.dot(a_ref[...], wb_ref[...], preferred_element_type=F32)
    o_ref[...] = acc.astype(o_ref.dtype)


def _inproj(a, wt, n_out, shifts):
    m, k = a.shape
    bm = min(MM_BM, m)
    bn = MMW_BN
    assert n_out % bn == 0 and m % bm == 0 and all(s % SUBLANES == 0 and s <= LANES for _, _, s in shifts)
    per = bn // LANES
    return pl.pallas_call(
        functools.partial(_inproj_kernel, shifts=shifts, col_chunk=min(512, k)),
        grid=(n_out // bn, m // bm),
        in_specs=[pl.BlockSpec((bm, k), lambda j, i: (i, 0)),
                  pl.BlockSpec((bn, k), lambda j, i: (j, 0)),
                  pl.BlockSpec((LANES, k), lambda j, i: ((j + 1) * per, 0))],
        out_specs=pl.BlockSpec((bm, bn), lambda j, i: (i, j)),
        out_shape=jax.ShapeDtypeStruct((m, n_out), BF16),
        scratch_shapes=[pltpu.VMEM((k, bn), BF16)],
        compiler_params=_params("parallel", "arbitrary"),
        name="inproj",
    )(a, wt, wt)


def _gate_proj_kernel(a_ref, wa_ref, wb_ref, o_ref, *, n_a, off_b, n_b):
    k = wa_ref.shape[1]
    rows = jnp.concatenate([wa_ref[0:n_a, :], wb_ref[off_b:off_b + n_b, :],
                            jnp.zeros((LANES - n_a - n_b, k), F32)], axis=0).astype(BF16)
    o_ref[...] = lax.dot_general(a_ref[...], rows, _NT, preferred_element_type=F32)


def _gate_proj(a, wt, row_a, n_a, row_b, n_b):
    m, k = a.shape
    bm = min(MM_BM, m)
    off_b = row_b % LANES
    assert row_a % LANES == 0 and n_a % SUBLANES == 0 and off_b % SUBLANES == 0 and n_b % SUBLANES == 0
    assert off_b + n_b <= LANES and n_a + n_b <= LANES
    return pl.pallas_call(
        functools.partial(_gate_proj_kernel, n_a=n_a, off_b=off_b, n_b=n_b),
        grid=(m // bm,),
        in_specs=[pl.BlockSpec((bm, k), lambda i: (i, 0)),
                  pl.BlockSpec((LANES, k), lambda i: (row_a // LANES, 0)),
                  pl.BlockSpec((LANES, k), lambda i: (row_b // LANES, 0))],
        out_specs=pl.BlockSpec((bm, LANES), lambda i: (i, 0)),
        out_shape=jax.ShapeDtypeStruct((m, LANES), F32),
        compiler_params=_params("parallel"),
        name="gate_proj",
    )(a, wt, wt)


def _mma_kernel(a_ref, b_ref, o_ref, ab_ref, *, gelu, row_chunk):
    @pl.when(pl.program_id(1) == 0)
    def _():
        for r0 in range(0, ab_ref.shape[0], row_chunk):
            ab_ref[r0:r0 + row_chunk, :] = a_ref[r0:r0 + row_chunk, :].astype(BF16)

    acc = jnp.dot(ab_ref[...], b_ref[...], preferred_element_type=F32)
    if gelu:
        acc = _gelu(acc)
    o_ref[...] = acc.astype(o_ref.dtype)


def _matmul_a(a, b, out_dtype, gelu=False):
    m, k = a.shape
    n = b.shape[1]
    bm = min(MMA_BM, m)
    bn = min(MM_BN, n)
    return pl.pallas_call(
        functools.partial(_mma_kernel, gelu=gelu, row_chunk=min(128, bm)),
        grid=(m // bm, n // bn),
        in_specs=[pl.BlockSpec((bm, k), lambda i, j: (i, 0)), pl.BlockSpec((k, bn), lambda i, j: (0, j))],
        out_specs=pl.BlockSpec((bm, bn), lambda i, j: (i, j)),
        out_shape=jax.ShapeDtypeStruct((m, n), out_dtype),
        scratch_shapes=[pltpu.VMEM((bm, k), BF16)],
        compiler_params=_params("parallel", "arbitrary"),
        name="matmul_a",
    )(a, b)


def _pos_columns(pos, width):
    lane = lax.broadcasted_iota(jnp.int32, (pos.shape[0], width), 1)
    coarse = (pos >> 6) << 6
    vals = jnp.where(lane < 3, coarse, jnp.where(lane < 6, pos & 63, 0))
    return vals.astype(F32).astype(BF16)


def _nsa_kernel(slopes_ref, q_ref, kcm_ref, vcm_ref, ks_ref, vs_ref, kw_ref, vw_ref, gz_ref, wck_ref, wcv_ref,
                pos_ref, o_ref, kc_ref, vc_ref, pext_ref, xs_ref, acc_ref, qext_ref, *, seq, tq, tk,
                heads_per_group):
    R = heads_per_group
    dk = NSA_HEAD_DIM
    g = pl.program_id(1)
    qi = pl.program_id(2)
    t0 = qi * tq
    n_cmp = (seq - CMP_BLOCK) // CMP_STRIDE + 1
    n_slc = seq // SLC_BLOCK
    n_sel = min(SLC_TOPN, n_slc)
    half = CMP_STRIDE * dk
    scale = dk ** -0.5
    c_exp = scale * math.log2(math.e)
    rows = R * tq

    @pl.when(qi == 0)
    def _():
        pext_ref[...] = _pos_columns(lax.broadcasted_iota(jnp.int32, (seq, 1), 0), LANES)
        cpos = lax.broadcasted_iota(jnp.int32, (LANES, 1), 0) * CMP_STRIDE + (CMP_BLOCK - 1)
        for src_ref, w_ref, dst, pi in ((kcm_ref, wck_ref, kc_ref, 0), (vcm_ref, wcv_ref, vc_ref, 1)):
            xs_ref[...] = src_ref[...].astype(F32)
            x = jnp.concatenate([xs_ref[pl.ds(l, LANES, stride=CMP_STRIDE), :].astype(BF16)
                                 for l in range(CMP_STRIDE)], axis=1)
            ya = jnp.dot(x, w_ref[0:half, :], preferred_element_type=F32)
            yb = jnp.dot(x, w_ref[half:2 * half, :], preferred_element_type=F32)
            pos = jnp.broadcast_to(pos_ref[pi], (SUBLANES, 2 * half)).astype(BF16)
            pb = jnp.dot(pos, w_ref[...], preferred_element_type=F32)[0:1, :]
            comp = (ya + pltpu.roll(yb, LANES - 1, 0) + pb).astype(BF16)
            dst[...] = jnp.concatenate([comp, _pos_columns(cpos, LANES)], axis=1) if pi == 0 else comp

        head_of_row = lax.broadcasted_iota(jnp.int32, (rows, LANES), 0) >> _log2(tq)
        lane6 = lax.broadcasted_iota(jnp.int32, (rows, LANES), 1)
        term = jnp.where(lane6 < 3, lane6, lane6 - 3)
        q_ext = jnp.zeros((rows, LANES), F32)
        for r in range(R):
            for k in range(3):
                q_ext = jnp.where((head_of_row == r) & (term == k) & (lane6 < 6), slopes_ref[g, r, k], q_ext)
        qext_ref[...] = q_ext.astype(BF16)

    row_id = lax.broadcasted_iota(jnp.int32, (rows, 1), 0)
    t_col = t0 + (row_id & (tq - 1))
    q = q_ref[...]
    qs = jnp.concatenate([q[:, r * dk:(r + 1) * dk] for r in range(R)], axis=0)
    qa = jnp.concatenate([qs, qext_ref[...]], axis=1)

    n_idx = lax.broadcasted_iota(jnp.int32, (rows, LANES), 1)
    cvalid = (t_col >= n_idx * CMP_STRIDE + CMP_BLOCK - 1) & (n_idx < n_cmp)
    sc = lax.dot_general(qa, kc_ref[...], _NT, preferred_element_type=F32)
    sc = jnp.where(cvalid, sc, NEG_INF)
    mc = jnp.max(sc, axis=-1, keepdims=True)
    p_cmp = jnp.where(cvalid, jnp.exp2((sc - mc) * c_exp), 0.0)
    p_cmp = p_cmp * (1.0 / jnp.maximum(jnp.sum(p_cmp, axis=-1, keepdims=True), TINY))
    o_cmp = jnp.dot(p_cmp.astype(BF16), vc_ref[...], preferred_element_type=F32)

    psum = p_cmp[0:tq]
    for r in range(1, R):
        psum = psum + p_cmp[r * tq:(r + 1) * tq]
    jj = lax.broadcasted_iota(jnp.int32, (LANES, LANES), 0)
    nn = lax.broadcasted_iota(jnp.int32, (LANES, LANES), 1)
    ovl = _ind((nn * CMP_STRIDE < jj * SLC_BLOCK + SLC_BLOCK) & (nn * CMP_STRIDE + CMP_BLOCK > jj * SLC_BLOCK)
               & (jj < n_slc) & (nn < n_cmp), BF16)
    imp_t = jnp.zeros((LANES, tq), F32)
    for part in _split3(psum):
        imp_t = imp_t + lax.dot_general(ovl, part, _NT, preferred_element_type=F32)
    imp_t = imp_t[0:n_slc]
    j_i = lax.broadcasted_iota(jnp.int32, (n_slc, tq), 0)
    t_i = t0 + lax.broadcasted_iota(jnp.int32, (n_slc, tq), 1)
    cur = t_i >> _log2(SLC_BLOCK)
    causal = j_i * SLC_BLOCK <= t_i
    forced = (j_i == 0) | (j_i == cur) | (j_i == cur - 1)
    score = jnp.where(causal, jnp.where(forced, BIG, imp_t), -BIG)
    rank = jnp.zeros((n_slc, tq), jnp.int32)
    for jp in range(n_slc):
        row = score[jp:jp + 1, :]
        beats = (row > score) | ((row == score) & (j_i > jp))
        rank = rank + _ind(beats, jnp.int32)
    sel_t = jnp.where(rank < n_sel, 0.0, NEG_INF)
    sel_t = jnp.concatenate([sel_t, jnp.zeros((LANES - n_slc, tq), F32)], axis=0)
    sel_neg = sel_t.T.astype(BF16)
    tq_col = t0 + lax.broadcasted_iota(jnp.int32, (tq, 1), 0)

    def attend(s, v):
        p = jnp.exp2((s - jnp.max(s, axis=-1, keepdims=True)) * c_exp).astype(BF16)
        ones = _ind(lax.broadcasted_iota(jnp.int32, (v.shape[0], LANES), 1) == 0, BF16)
        pv = jnp.dot(p, jnp.concatenate([v, ones], axis=1), preferred_element_type=F32)
        return pv[:, 0:dk] * (1.0 / jnp.maximum(pv[:, dk:dk + 1], TINY))

    def slc_variant(nk):
        ka = jnp.concatenate([ks_ref[0:nk, :], pext_ref[0:nk, :]], axis=1)
        ej = lax.broadcasted_iota(jnp.int32, (LANES, nk), 0)
        ec = lax.broadcasted_iota(jnp.int32, (LANES, nk), 1)
        expand = _ind(ej == (ec >> _log2(SLC_BLOCK)), BF16)
        madd = jnp.dot(sel_neg, expand, preferred_element_type=F32)
        madd = madd + jnp.where(lax.broadcasted_iota(jnp.int32, (tq, nk), 1) <= tq_col, 0.0, NEG_INF)
        s = lax.dot_general(qa, ka, _NT, preferred_element_type=F32) + jnp.concatenate([madd] * R, axis=0)
        acc_ref[...] = attend(s, vs_ref[0:nk, :])

    variant = (t0 + tq - 1) // tk
    for vi in range(seq // tk):
        pl.when(variant == vi)(functools.partial(slc_variant, (vi + 1) * tk))
    o_slc = acc_ref[...]

    span = WINDOW + tq
    ws = pl.multiple_of(jnp.maximum(t0 - WINDOW, 0), tq)
    kwa = jnp.concatenate([kw_ref[pl.ds(ws, span), :], pext_ref[pl.ds(ws, span), :]], axis=1)
    dw_i = tq_col - (ws + lax.broadcasted_iota(jnp.int32, (tq, span), 1))
    wadd = jnp.where((dw_i >= 0) & (dw_i < WINDOW), 0.0, NEG_INF)
    sw = lax.dot_general(qa, kwa, _NT, preferred_element_type=F32) + jnp.concatenate([wadd] * R, axis=0)
    o_win = attend(sw, vw_ref[pl.ds(ws, span), :])

    gz = gz_ref[...]
    gsel = jnp.zeros((tq, 3 * R), F32)
    for gg in range(NSA_KV_GROUPS):
        gsel = jnp.where(g == gg, gz[:, 3 * R * gg:3 * R * (gg + 1)], gsel)
    gts = jax.nn.sigmoid(gsel)
    for r in range(R):
        sl = slice(r * tq, (r + 1) * tq)
        o = (gts[:, 3 * r:3 * r + 1] * o_cmp[sl] + gts[:, 3 * r + 1:3 * r + 2] * o_slc[sl]
             + gts[:, 3 * r + 2:3 * r + 3] * o_win[sl])
        o_ref[:, r * dk:(r + 1) * dk] = o.astype(o_ref.dtype)


def _nsa(slopes, z, gz, wck, wcv, pos, *, batch, seq, kv_col):
    G = NSA_KV_GROUPS
    R = NSA_HEADS // G
    dk = NSA_HEAD_DIM
    tq = NSA_TQ
    tk = NSA_TK
    nq = seq // tq
    assert seq // CMP_STRIDE == LANES and CMP_BLOCK == 2 * CMP_STRIDE and seq >= WINDOW + tq and seq % tk == 0
    kv0 = kv_col // dk

    def kv_spec(i):
        return pl.BlockSpec((seq, dk), lambda b, g, qi: (b, kv0 + i * G + g))

    w_spec = pl.BlockSpec((CMP_BLOCK * dk, dk), lambda b, g, qi: (0, 0))
    rows = R * tq
    assert dk == LANES and 3 * NSA_HEADS <= LANES
    return pl.pallas_call(
        functools.partial(_nsa_kernel, seq=seq, tq=tq, tk=tk, heads_per_group=R),
        grid=(batch, G, nq),
        in_specs=[
            pl.BlockSpec(memory_space=pltpu.SMEM),
            pl.BlockSpec((tq, R * dk), lambda b, g, qi: (b * nq + qi, g)),
            kv_spec(0), kv_spec(1), kv_spec(2), kv_spec(3), kv_spec(4), kv_spec(5),
            pl.BlockSpec((tq, LANES), lambda b, g, qi: (b * nq + qi, 0)),
            w_spec, w_spec,
            pl.BlockSpec((2, 1, CMP_BLOCK * dk), lambda b, g, qi: (0, 0, 0)),
        ],
        out_specs=pl.BlockSpec((tq, R * dk), lambda b, g, qi: (b * nq + qi, g)),
        out_shape=jax.ShapeDtypeStruct((batch * seq, NSA_HEADS * dk), BF16),
        scratch_shapes=[
            pltpu.VMEM((LANES, 2 * dk), BF16), pltpu.VMEM((LANES, dk), BF16),
            pltpu.VMEM((seq, LANES), BF16), pltpu.VMEM((seq, dk), F32), pltpu.VMEM((rows, dk), F32),
            pltpu.VMEM((rows, LANES), BF16),
        ],
        compiler_params=_params("parallel", "parallel", "arbitrary"),
        name="nsa",
    )(slopes, z, z, z, z, z, z, z, gz, wck, wcv, pos)


def _log_sigmoid(x):
    return jnp.minimum(x, 0.0) - jnp.log1p(jnp.exp(-jnp.abs(x)))


def _mlstm_kernel(bias_ref, q_ref, k_ref, v_ref, og_ref, gz_ref, nw_ref, o_ref, c_ref, *, seq, L, gate_lane, hps):
    dqk = ML_QK_DIM
    dv = ML_V_DIM
    nc = seq // L
    lane = lax.broadcasted_iota(jnp.int32, (L, LANES), 1)

    def gate_forms(blk, which, bias):
        col = jnp.sum(jnp.where(lane == which, blk, 0.0), axis=1, keepdims=True) + bias
        return col, jnp.broadcast_to(col, (L, LANES)).T[0:1, :]

    ext = dv + LANES
    ii = lax.broadcasted_iota(jnp.int32, (L, L), 0)
    kk = lax.broadcasted_iota(jnp.int32, (L, L), 1)
    causal = kk <= ii
    tri = _ind(causal, BF16)
    tri_t = _ind(ii <= kk, BF16)
    ones_slab = _ind(lax.broadcasted_iota(jnp.int32, (L, LANES), 1) == 0, BF16)
    c_ref[...] = jnp.zeros((hps, dqk, ext), F32)

    def cumsum_mats(lf_row, lf_col):
        lr = jnp.broadcast_to(lf_row, (L, L))
        lc = jnp.broadcast_to(lf_col, (L, L))
        b_row = jnp.zeros((L, L), F32)
        b_col = jnp.zeros((L, L), F32)
        for part in _split3(lr)[:2]:
            b_row = b_row + jnp.dot(part, tri_t, preferred_element_type=F32)
        for part in _split3(lc)[:2]:
            b_col = b_col + jnp.dot(tri, part, preferred_element_type=F32)
        return b_row, b_col

    def head_chunk(hh, r0, gblk, m):
        h_id = pl.program_id(1) * hps + hh
        i_lane = gate_lane + h_id
        f_lane = i_lane + ML_HEADS
        qc = q_ref[pl.ds(r0, L), hh * dqk:(hh + 1) * dqk]
        kc = k_ref[pl.ds(r0, L), hh * dqk:(hh + 1) * dqk] * (dqk ** -0.5)
        v_ext = jnp.concatenate([v_ref[pl.ds(r0, L), hh * dv:(hh + 1) * dv], ones_slab], axis=1)
        i_col, i_row = gate_forms(gblk, i_lane, bias_ref[0, h_id])
        f_col, f_row = gate_forms(gblk, f_lane, bias_ref[1, h_id])
        lf_col = _log_sigmoid(f_col)
        lf_row = _log_sigmoid(f_row)
        b_row, b_col = cumsum_mats(lf_row, lf_col)
        log_d = jnp.where(causal, b_col - b_row + i_row, NEG_INF)
        b_c = b_col[:, 0:1]
        m_inter = b_c + m
        m_t = jnp.maximum(m_inter, jnp.max(log_d, axis=-1, keepdims=True))
        d = jnp.exp(log_d - m_t)
        s = lax.dot_general(qc, kc, _NT, preferred_element_type=F32) * d
        w_inter = jnp.exp(m_inter - m_t)
        inter = jnp.dot(qc, c_ref[hh].astype(BF16), preferred_element_type=F32)
        intra = jnp.dot(s.astype(BF16), v_ext, preferred_element_type=F32)
        numden = w_inter * inter + intra
        den = numden[:, dv:dv + 1]
        h = numden[:, 0:dv] / jnp.maximum(jnp.abs(den), jnp.exp(-m_t))
        g_tot = b_row[0:1, L - 1:L]
        a_row = g_tot - b_row[0:1, :] + i_row
        a_col = g_tot - b_c + i_col
        m_new = jnp.maximum(g_tot + m, jnp.max(a_row, axis=-1, keepdims=True))
        decay = jnp.exp(g_tot + m - m_new)
        w_tok = jnp.exp(a_col - m_new)
        kw_t = (kc.astype(F32) * w_tok).T.astype(BF16)
        c_ref[hh] = decay * c_ref[hh] + jnp.dot(kw_t, v_ext, preferred_element_type=F32)
        hn = _rms(h, nw_ref[hh]) * jax.nn.sigmoid(og_ref[pl.ds(r0, L), hh * dv:(hh + 1) * dv].astype(F32))
        o_ref[pl.ds(r0, L), hh * dv:(hh + 1) * dv] = hn.astype(o_ref.dtype)
        return m_new

    def body(c, ms):
        r0 = pl.multiple_of(c * L, L)
        gblk = gz_ref[pl.ds(r0, L), :]
        return tuple(head_chunk(hh, r0, gblk, ms[hh]) for hh in range(hps))

    lax.fori_loop(0, nc, body, tuple(jnp.zeros((1, 1), F32) for _ in range(hps)))


def _mlstm(gate_bias, z, gz, norm_w, *, batch, seq, q_col, k_col, v_col, o_col, gate_lane):
    H = ML_HEADS
    dqk = ML_QK_DIM
    dv = ML_V_DIM
    L = min(ML_CHUNK_LEN, seq)
    hps = ML_HEADS_PER_STEP
    qw, vw = hps * dqk, hps * dv
    assert L % LANES == 0 and seq % L == 0 and gate_lane + 2 * H <= LANES and H % hps == 0
    assert q_col % qw == 0 and k_col % qw == 0 and v_col % vw == 0 and o_col % vw == 0
    return pl.pallas_call(
        functools.partial(_mlstm_kernel, seq=seq, L=L, gate_lane=gate_lane, hps=hps),
        grid=(batch, H // hps),
        in_specs=[
            pl.BlockSpec(memory_space=pltpu.SMEM),
            pl.BlockSpec((seq, qw), lambda b, h: (b, q_col // qw + h)),
            pl.BlockSpec((seq, qw), lambda b, h: (b, k_col // qw + h)),
            pl.BlockSpec((seq, vw), lambda b, h: (b, v_col // vw + h)),
            pl.BlockSpec((seq, vw), lambda b, h: (b, o_col // vw + h)),
            pl.BlockSpec((seq, LANES), lambda b, h: (b, 0)),
            pl.BlockSpec((hps, 1, dv), lambda b, h: (h, 0, 0)),
        ],
        out_specs=pl.BlockSpec((seq, vw), lambda b, h: (b, h)),
        out_shape=jax.ShapeDtypeStruct((batch * seq, H * dv), BF16),
        scratch_shapes=[pltpu.VMEM((hps, dqk, dv + LANES), F32)],
        compiler_params=_params("parallel", "parallel"),
        name="mlstm",
    )(gate_bias, z, z, z, z, gz, norm_w.reshape(H, 1, dv))


def _peer_pairs():
    return [(a, (PEER_TOPK + 1) // (a + 1)) for a in range(PEER_TOPK + 1)]


def _topk_multiset(s, k, with_rank=False):
    tt = s.shape[1]
    riota = lax.broadcasted_iota(jnp.int32, (k, tt), 0).astype(F32)
    v = jnp.full((k, tt), -jnp.inf, F32)
    taken = jnp.zeros((1, tt), F32)
    rank = jnp.full(s.shape, float(s.shape[0]), F32)
    rem = s
    for _ in range(k):
        m = jnp.max(rem, axis=0, keepdims=True)
        eq = rem == m
        cnt = jnp.sum(_ind(eq, F32), axis=0, keepdims=True)
        v = jnp.where((riota >= taken) & (riota < taken + cnt), m, v)
        if with_rank:
            rank = jnp.where(eq, taken, rank)
        taken = taken + cnt
        rem = jnp.where(eq, -jnp.inf, rem)
    return (v, rank) if with_rank else v


def _peer_topk_kernel(q_ref, khi_ref, klo_ref, cnt_ref, rank_ref, e1_ref, e2_ref, cand_ref):
    H = PEER_HEADS
    half = PEER_KEY_DIM // 2
    K = PEER_TOPK
    pairs = _peer_pairs()
    n_cand = sum(nb for _, nb in pairs)
    tt = q_ref.shape[0]
    for h in range(H):
        s_parts = []
        for side in range(2):
            qf = q_ref[:, (2 * h + side) * half:(2 * h + side + 1) * half]
            q_hi = qf.astype(BF16)
            q_lo = (qf - q_hi.astype(F32)).astype(BF16)
            k_hi = khi_ref[side, h]
            k_lo = klo_ref[side, h]
            s = (lax.dot_general(k_hi, q_hi, _NT, preferred_element_type=F32)
                 + lax.dot_general(k_hi, q_lo, _NT, preferred_element_type=F32)
                 + lax.dot_general(k_lo, q_hi, _NT, preferred_element_type=F32))
            s_parts.append(s)
        s1, s2 = s_parts
        v1 = _topk_multiset(s1, K + 1)
        v2, rank2 = _topk_multiset(s2, K + 1, with_rank=True)
        off = 0
        for a, nb in pairs:
            cand_ref[off:off + nb, :] = v1[a:a + 1, :] + v2[0:nb, :]
            off += nb
        cand_ref[n_cand:, :] = jnp.full((cand_ref.shape[0] - n_cand, tt), -jnp.inf, F32)
        rem = cand_ref[...]
        taken = jnp.zeros((1, tt), F32)
        tau16 = jnp.zeros((1, tt), F32)
        tau17 = jnp.zeros((1, tt), F32)
        zsum = jnp.zeros((1, tt), F32)
        cmax = v1[0:1, :] + v2[0:1, :]
        for _ in range(K + 1):
            m = jnp.max(rem, axis=0, keepdims=True)
            eq = rem == m
            cnt = jnp.sum(_ind(eq, F32), axis=0, keepdims=True)
            take = jnp.minimum(cnt, jnp.maximum(K - taken, 0.0))
            zsum = zsum + take * jnp.exp(m - cmax)
            after = taken + cnt
            tau16 = jnp.where((taken < K) & (after >= K), m, tau16)
            tau17 = jnp.where((taken <= K) & (after > K), m, tau17)
            taken = after
            rem = jnp.where(eq, -jnp.inf, rem)
        theta = 0.5 * (tau16 + tau17)
        r1 = theta - s1
        cnt1 = jnp.zeros(r1.shape, F32)
        for b in range(K + 1):
            cnt1 = cnt1 + _ind(v2[b:b + 1, :] >= r1, F32)
        cnt_ref[h] = cnt1
        rank_ref[h] = rank2.astype(rank_ref.dtype)
        e1_ref[h] = jnp.exp(s1 - v1[0:1, :])
        e2_ref[h] = (jnp.exp(s2 - v2[0:1, :]) / zsum).astype(e2_ref.dtype)


def _peer_topk(pq, k_hi, k_lo):
    n = pq.shape[0]
    H = PEER_HEADS
    nk = PEER_N_KEYS
    tt = min(PEER_TOPK_TT, n)
    n_cand = sum(nb for _, nb in _peer_pairs())
    cand_rows = -(-n_cand // SUBLANES) * SUBLANES
    key_spec = pl.BlockSpec((2, H, nk, PEER_KEY_DIM // 2), lambda i: (0, 0, 0, 0))
    hkt = pl.BlockSpec((H, nk, tt), lambda i: (0, 0, i))
    wide = jax.ShapeDtypeStruct((H, nk, n), F32)
    narrow = jax.ShapeDtypeStruct((H, nk, n), BF16)
    return pl.pallas_call(
        _peer_topk_kernel,
        grid=(n // tt,),
        in_specs=[pl.BlockSpec((tt, pq.shape[1]), lambda i: (i, 0)), key_spec, key_spec],
        out_specs=[hkt, hkt, hkt, hkt],
        out_shape=[wide, narrow, wide, narrow],
        scratch_shapes=[pltpu.VMEM((cand_rows, tt), F32)],
        compiler_params=_params("parallel"),
        name="peer_topk",
    )(pq, k_hi, k_lo)


def _bcast_packed(row):
    half = jnp.broadcast_to(row, (SUBLANES, LANES))
    return jnp.concatenate([half, half], axis=0).astype(BF16)


def _peer_mix_kernel(act_ref, vt_ref, cnt_ref, e1_ref, rank_ref, e2_ref, o_ref, acc_ref, a_ref, *, chunk):
    H = PEER_HEADS
    nk = PEER_N_KEYS
    te, tt = a_ref.shape
    e = pl.program_id(1)
    pack = 2 * SUBLANES

    @pl.when(e == 0)
    def _():
        acc_ref[...] = jnp.zeros(acc_ref.shape, F32)

    for i in range(te // nk):
        cnt_rows = [cnt_ref[h, i:i + 1, :] for h in range(H)]
        e1_rows = [e1_ref[h, i:i + 1, :] for h in range(H)]
        for lt in range(tt // LANES):
            ls = slice(lt * LANES, (lt + 1) * LANES)
            cntb = [_bcast_packed(cnt_rows[h][:, ls]) for h in range(H)]
            e1b = [_bcast_packed(e1_rows[h][:, ls]) for h in range(H)]
            for sg in range(nk // pack):
                rs = slice(sg * pack, (sg + 1) * pack)
                w = jnp.zeros((pack, LANES), BF16)
                for h in range(H):
                    hit = rank_ref[h, rs, ls] < cntb[h]
                    w = w + e1b[h] * jnp.where(hit, e2_ref[h, rs, ls], jnp.zeros((), BF16))
                rr = slice(i * nk + sg * pack, i * nk + (sg + 1) * pack)
                a_ref[rr, ls] = act_ref[rr, ls] * w
        done = (i + 1) * nk
        if done % chunk == 0:
            ks = slice(done - chunk, done)
            acc_ref[...] += jnp.dot(vt_ref[:, ks], a_ref[ks, :], preferred_element_type=F32)

    @pl.when(e == pl.num_programs(1) - 1)
    def _():
        o_ref[...] = acc_ref[...].T.astype(o_ref.dtype)


def _peer_mix(act_t, pvt, cnt1, rank2, e1, e2):
    n_exp, n = act_t.shape
    d = pvt.shape[0]
    H = PEER_HEADS
    nk = PEER_N_KEYS
    tt = min(PEER_TT, n)
    te = PEER_TE
    assert te == SUBLANES * nk and n_exp % te == 0

    def i1_spec():
        return pl.BlockSpec((H, SUBLANES, tt), lambda i, e: (0, e, i))

    def i2_spec():
        return pl.BlockSpec((H, nk, tt), lambda i, e: (0, 0, i))

    return pl.pallas_call(
        functools.partial(_peer_mix_kernel, chunk=PEER_DOT_CHUNK),
        grid=(n // tt, n_exp // te),
        in_specs=[
            pl.BlockSpec((te, tt), lambda i, e: (e, i)),
            pl.BlockSpec((d, te), lambda i, e: (0, e)),
            i1_spec(), i1_spec(), i2_spec(), i2_spec(),
        ],
        out_specs=pl.BlockSpec((tt, d), lambda i, e: (i, 0)),
        out_shape=jax.ShapeDtypeStruct((n, d), BF16),
        scratch_shapes=[pltpu.VMEM((d, tt), F32), pltpu.VMEM((te, tt), BF16)],
        compiler_params=_params("parallel", "arbitrary"),
        name="peer_mix",
    )(act_t, pvt, cnt1, e1, rank2, e2)


def _cast_transpose_kernel(x_ref, o_ref):
    o_ref[...] = x_ref[...].T.astype(o_ref.dtype)


def _cast_transpose(x, dtype):
    r, c = x.shape
    tr = min(MMA_BM, r)
    return pl.pallas_call(
        _cast_transpose_kernel,
        grid=(r // tr,),
        in_specs=[pl.BlockSpec((tr, c), lambda i: (i, 0))],
        out_specs=pl.BlockSpec((c, tr), lambda i: (0, i)),
        out_shape=jax.ShapeDtypeStruct((c, r), dtype),
        compiler_params=_params("parallel"),
        name="cast_transpose",
    )(x)


def _layer(h, mod3, norm_pre_mix, norm_post_mix, norm_pre_ffn, norm_post_ffn, w_in, cmp_pos, w_cmp_k, w_cmp_v,
           gate_bias, ml_norm_w, w_up_nsa, w_up_mlstm, w_out, peer_w_q, sub_keys, peer_u, peer_v, *, batch, seq):
    d = h.shape[1]
    G = NSA_KV_GROUPS
    R = NSA_HEADS // G
    dk = NSA_HEAD_DIM
    H = ML_HEADS
    nsa_w = NSA_HEADS * dk
    kv_w = 6 * G * dk
    mq_w = H * ML_QK_DIM
    mv_w = H * ML_V_DIM
    c_kv = nsa_w
    c_gn = c_kv + kv_w
    c_mq = c_gn + 3 * NSA_HEADS
    c_mk = c_mq + mq_w
    c_mv = c_mk + mq_w
    c_mo = c_mv + mv_w
    c_if = c_mo + mv_w
    c_mg = c_if + 2 * H
    n_gate = 3 * NSA_HEADS + 2 * H
    z_kv = nsa_w
    z_mq = z_kv + kv_w
    z_mk = z_mq + mq_w
    z_mv = z_mk + mq_w
    z_mo = z_mv + mv_w
    z_mg = z_mo + mv_w
    z_w = z_mg + 2 * d
    assert z_mq % MMW_BN == 0 and z_mg % MMW_BN == 0 and z_w % MMW_BN == 0
    z_shifts = [(0, z_mq // MMW_BN, 0), (z_mq // MMW_BN, z_mg // MMW_BN, 3 * NSA_HEADS),
                (z_mg // MMW_BN, z_w // MMW_BN, n_gate)]

    u = _prenorm(h, norm_pre_mix, mod3, seq)
    w_in_t = w_in.T
    z = _inproj(u, w_in_t, z_w, z_shifts)
    gz = _gate_proj(u, w_in_t, c_gn, 3 * NSA_HEADS, c_if, 2 * H)

    slopes = jnp.exp2(-8.0 * jnp.arange(1, NSA_HEADS + 1, dtype=F32) / NSA_HEADS) / (dk ** -0.5)
    slopes = jnp.stack([t.astype(F32) for t in _split3(slopes)], axis=-1).reshape(G, R, 3)
    o_nsa = _nsa(slopes, z, gz, w_cmp_k.astype(BF16), w_cmp_v.astype(BF16),
                 cmp_pos.reshape(2, 1, CMP_BLOCK * dk), batch=batch, seq=seq, kv_col=z_kv)

    h_ml = _mlstm(gate_bias, z, gz, ml_norm_w, batch=batch, seq=seq, q_col=z_mq, k_col=z_mk, v_col=z_mv,
                  o_col=z_mo, gate_lane=3 * NSA_HEADS)

    t1 = _matmul_w(o_nsa, w_up_nsa, BF16, gate=z, gate_col=z_mg)
    mix = _matmul_w(h_ml, w_up_mlstm, BF16, gate=z, gate_col=z_mg + d, add=t1)
    y = _matmul_w(mix, w_out, BF16)
    h1, u2, u2_t = _midnorm(h, y, norm_post_mix, norm_pre_ffn, mod3, seq)

    pq = _matmul_w(u2, peer_w_q, F32)
    k_hi = sub_keys.astype(BF16)
    k_lo = (sub_keys - k_hi.astype(F32)).astype(BF16)
    cnt1, rank2, e1, e2 = _peer_topk(pq, k_hi, k_lo)
    act_t = _matmul_a(peer_u, u2_t, BF16, gelu=True)
    y2 = _peer_mix(act_t, _cast_transpose(peer_v, BF16), cnt1, rank2, e1, e2)
    return _finalnorm(h1, y2, norm_post_ffn, mod3, seq)


def kernel(x, c, w_ada, b_ada, norm_pre_mix, norm_post_mix, norm_pre_ffn, norm_post_ffn, w_in, nsa_cmp_pos, nsa_w_cmp_k, nsa_w_cmp_v, mlstm_gate_bias, mlstm_norm_w, w_up_nsa, w_up_mlstm, w_out, peer_w_q, peer_sub_keys, peer_u, peer_v):
    batch, seq, d = x.shape
    depth = w_ada.shape[0]
    h = x.reshape(batch * seq, d)
    c_pad = jnp.concatenate([c, jnp.zeros((SUBLANES - batch % SUBLANES, d), c.dtype)], axis=0)
    for l in range(depth):
        mod = _adaln(c_pad, w_ada[l], b_ada[l])[:batch]
        mod3 = mod.reshape(batch, 6, d)
        h = _layer(h, mod3, norm_pre_mix[l], norm_post_mix[l], norm_pre_ffn[l], norm_post_ffn[l], w_in[l],
                   nsa_cmp_pos[l], nsa_w_cmp_k[l], nsa_w_cmp_v[l], mlstm_gate_bias[l], mlstm_norm_w[l],
                   w_up_nsa[l], w_up_mlstm[l], w_out[l], peer_w_q[l], peer_sub_keys[l], peer_u[l], peer_v[l],
                   batch=batch, seq=seq)
    return h.reshape(batch, seq, d).astype(x.dtype)
```

```python
import functools
import math

import jax
import jax.numpy as jnp
from jax import lax
from jax.experimental import pallas as pl
from jax.experimental.pallas import tpu as pltpu

D_MODEL = 4096
BATCH = 4
SEQ = 2048
NSA_HEADS = 16
NSA_KV_GROUPS = 4
NSA_HEAD_DIM = 128
CMP_BLOCK = 32
CMP_STRIDE = 16
SLC_BLOCK = 64
SLC_TOPN = 16
WINDOW = 512
ML_HEADS = 8
ML_QK_DIM = 256
ML_V_DIM = 512
PEER_HEADS = 8
PEER_KEY_DIM = 256
PEER_N_KEYS = 128
PEER_TOPK = 16

NEG_INF = -1e30
BIG = 1e9
TINY = 1e-30
EPS = 1e-6

F32 = jnp.float32
BF16 = jnp.bfloat16

V7X_VMEM_LIMIT_BYTES = 56 * 1024 * 1024
LANES = 128
SUBLANES = 8

ROW_TILE = 256
MM_BM = 1024
MM_BN = 1024
MMW_BN = 512
INPROJ_BN = 1024
MMW_VMEM_BUDGET_BYTES = 50 * 1024 * 1024
MMA_BM = 1024
CAST_T_ROWS = 512
ADA_BN = 512
NSA_TQ = 256
NSA_TK = 512
ML_CHUNK_LEN = 256
ML_HEADS_PER_STEP = 2
PEER_TOPK_TT = 256
PEER_TT = 512
PEER_TE = 1024
PEER_DOT_CHUNK = 256

_NT = (((1,), (1,)), ((), ()))


def _params(*sem):
    return pltpu.CompilerParams(dimension_semantics=sem, vmem_limit_bytes=V7X_VMEM_LIMIT_BYTES)


def _ind(cond, dtype):
    wide = jnp.int32 if jnp.issubdtype(dtype, jnp.integer) else F32
    return jnp.where(cond, jnp.ones((), wide), jnp.zeros((), wide)).astype(dtype)


def _log2(n):
    k = int(math.log2(n))
    assert 1 << k == n
    return k


def _split3(x):
    hi = x.astype(BF16)
    r1 = x - hi.astype(F32)
    mid = r1.astype(BF16)
    lo = (r1 - mid.astype(F32)).astype(BF16)
    return hi, mid, lo


def _adaln_kernel(c_ref, w_ref, b_ref, o_ref):
    c = c_ref[...]
    cond = (c * jax.nn.sigmoid(c)).astype(BF16)
    o_ref[...] = jnp.dot(cond, w_ref[...].astype(BF16), preferred_element_type=F32) + b_ref[...]


def _adaln(c_pad, w_ada, b_ada):
    rows, d = c_pad.shape
    n = w_ada.shape[1]
    bn = min(ADA_BN, n)
    return pl.pallas_call(
        _adaln_kernel,
        grid=(n // bn,),
        in_specs=[
            pl.BlockSpec((rows, d), lambda j: (0, 0)),
            pl.BlockSpec((d, bn), lambda j: (0, j)),
            pl.BlockSpec((1, bn), lambda j: (0, j)),
        ],
        out_specs=pl.BlockSpec((rows, bn), lambda j: (0, j)),
        out_shape=jax.ShapeDtypeStruct((rows, n), F32),
        compiler_params=_params("parallel"),
        name="adaln",
    )(c_pad, w_ada, b_ada.reshape(1, n))


def _rms(x, w):
    return x * lax.rsqrt(jnp.mean(x * x, axis=-1, keepdims=True) + EPS) * w


def _prenorm_kernel(x_ref, w_ref, mod_ref, u_ref):
    y = _rms(x_ref[...], w_ref[...])
    u_ref[...] = (y * (1.0 + mod_ref[1:2, :]) + mod_ref[0:1, :]).astype(u_ref.dtype)


def _prenorm(x2, w, mod3, seq):
    n, d = x2.shape
    tr = min(ROW_TILE, seq)
    return pl.pallas_call(
        _prenorm_kernel,
        grid=(n // tr,),
        in_specs=[
            pl.BlockSpec((tr, d), lambda i: (i, 0)),
            pl.BlockSpec((1, d), lambda i: (0, 0)),
            pl.BlockSpec((None, 6, d), lambda i: ((i * tr) // seq, 0, 0)),
        ],
        out_specs=pl.BlockSpec((tr, d), lambda i: (i, 0)),
        out_shape=jax.ShapeDtypeStruct((n, d), BF16),
        compiler_params=_params("parallel"),
        name="prenorm",
    )(x2, w.reshape(1, d), mod3)


def _midnorm_kernel(x_ref, y_ref, w1_ref, w2_ref, mod_ref, h_ref, u_ref, ut_ref):
    h = x_ref[...] + mod_ref[2:3, :] * _rms(y_ref[...].astype(F32), w1_ref[...])
    h_ref[...] = h
    u = _rms(h, w2_ref[...]) * (1.0 + mod_ref[4:5, :]) + mod_ref[3:4, :]
    u_ref[...] = u.astype(u_ref.dtype)
    ut_ref[...] = u.T.astype(ut_ref.dtype)


def _midnorm(x2, y, w_post, w_pre, mod3, seq):
    n, d = x2.shape
    tr = min(ROW_TILE, seq)
    row = pl.BlockSpec((tr, d), lambda i: (i, 0))
    vec = pl.BlockSpec((1, d), lambda i: (0, 0))
    return pl.pallas_call(
        _midnorm_kernel,
        grid=(n // tr,),
        in_specs=[row, row, vec, vec, pl.BlockSpec((None, 6, d), lambda i: ((i * tr) // seq, 0, 0))],
        out_specs=[row, row, pl.BlockSpec((d, tr), lambda i: (0, i))],
        out_shape=[jax.ShapeDtypeStruct((n, d), F32), jax.ShapeDtypeStruct((n, d), BF16),
                   jax.ShapeDtypeStruct((d, n), BF16)],
        compiler_params=_params("parallel"),
        name="midnorm",
    )(x2, y, w_post.reshape(1, d), w_pre.reshape(1, d), mod3)


def _finalnorm_kernel(h_ref, y_ref, w_ref, mod_ref, o_ref):
    o_ref[...] = h_ref[...] + mod_ref[5:6, :] * _rms(y_ref[...].astype(F32), w_ref[...])


def _finalnorm(h1, y, w, mod3, seq):
    n, d = h1.shape
    tr = min(ROW_TILE, seq)
    row = pl.BlockSpec((tr, d), lambda i: (i, 0))
    return pl.pallas_call(
        _finalnorm_kernel,
        grid=(n // tr,),
        in_specs=[row, row, pl.BlockSpec((1, d), lambda i: (0, 0)),
                  pl.BlockSpec((None, 6, d), lambda i: ((i * tr) // seq, 0, 0))],
        out_specs=row,
        out_shape=jax.ShapeDtypeStruct((n, d), F32),
        compiler_params=_params("parallel"),
        name="finalnorm",
    )(h1, y, w.reshape(1, d), mod3)


def _gelu(x):
    return 0.5 * x * (1.0 + lax.erf(x * (2.0 ** -0.5)))


def _mmw_kernel(*refs, has_gate, has_add, row_chunk):
    a_ref, w_ref = refs[0], refs[1]
    nxt = 2
    o_ref, wb_ref = refs[-2], refs[-1]

    @pl.when(pl.program_id(1) == 0)
    def _():
        for r0 in range(0, wb_ref.shape[0], row_chunk):
            wb_ref[r0:r0 + row_chunk, :] = w_ref[r0:r0 + row_chunk, :].astype(BF16)

    acc = jnp.dot(a_ref[...], wb_ref[...], preferred_element_type=F32)
    if has_gate:
        acc = acc * jax.nn.sigmoid(refs[nxt][...].astype(F32))
        nxt += 1
    if has_add:
        acc = acc + refs[nxt][...].astype(F32)
    o_ref[...] = acc.astype(o_ref.dtype)


def _matmul_w(a, w, out_dtype, *, gate=None, gate_col=0, add=None):
    m, k = a.shape
    n = w.shape[1]
    bm = min(MM_BM, m)
    n_side = (gate is not None) + (add is not None)
    out_bytes = jnp.dtype(out_dtype).itemsize

    def vmem_bytes(bn, w_bufs):
        return (w_bufs * k * bn * 4 + k * bn * 2 + 2 * bm * k * 2 + 2 * bm * bn * (out_bytes + 2 * n_side)
                + bm * bn * 4)

    if n % MM_BN == 0 and vmem_bytes(MM_BN, 1) <= MMW_VMEM_BUDGET_BYTES:
        bn, w_mode = MM_BN, pl.Buffered(1)
    else:
        bn, w_mode = min(MMW_BN, n), None
    assert n % bn == 0 and m % bm == 0
    in_specs = [pl.BlockSpec((bm, k), lambda j, i: (i, 0)),
                pl.BlockSpec((k, bn), lambda j, i: (0, j), pipeline_mode=w_mode)]
    args = [a, w]
    if gate is not None:
        goff = gate_col // bn
        assert goff * bn == gate_col
        in_specs.append(pl.BlockSpec((bm, bn), lambda j, i: (i, goff + j)))
        args.append(gate)
    if add is not None:
        in_specs.append(pl.BlockSpec((bm, bn), lambda j, i: (i, j)))
        args.append(add)
    return pl.pallas_call(
        functools.partial(_mmw_kernel, has_gate=gate is not None, has_add=add is not None, row_chunk=min(512, k)),
        grid=(n // bn, m // bm),
        in_specs=in_specs,
        out_specs=pl.BlockSpec((bm, bn), lambda j, i: (i, j)),
        out_shape=jax.ShapeDtypeStruct((m, n), out_dtype),
        scratch_shapes=[pltpu.VMEM((k, bn), BF16)],
        compiler_params=_params("parallel", "arbitrary"),
        name="matmul_w",
    )(*args)


def _inproj_kernel(a_ref, wt_ref, wtn_ref, o_ref, wb_ref, *, shifts, col_chunk):
    j = pl.program_id(0)
    k, bn = wb_ref.shape

    def fill(shift):
        for c0 in range(0, k, col_chunk):
            cs = slice(c0, c0 + col_chunk)
            if shift == 0:
                blk = wt_ref[:, cs]
            else:
                tall = jnp.concatenate([wt_ref[:, cs], wtn_ref[:, cs]], axis=0)
                blk = tall[shift:shift + bn, :]
            wb_ref[cs, :] = blk.T.astype(BF16)

    @pl.when(pl.program_id(1) == 0)
    def _():
        for j0, j1, shift in shifts:
            pl.when((j >= j0) & (j < j1))(functools.partial(fill, shift))

    acc = jnp.dot(a_ref[...], wb_ref[...], preferred_element_type=F32)
    o_ref[...] = acc.astype(o_ref.dtype)


def _inproj(a, wt, n_out, shifts):
    m, k = a.shape
    bm = min(MM_BM, m)
    bn = INPROJ_BN
    assert n_out % bn == 0 and m % bm == 0 and all(s % SUBLANES == 0 and s <= LANES for _, _, s in shifts)
    per = bn // LANES
    once = pl.Buffered(1)
    return pl.pallas_call(
        functools.partial(_inproj_kernel, shifts=shifts, col_chunk=min(512, k)),
        grid=(n_out // bn, m // bm),
        in_specs=[pl.BlockSpec((bm, k), lambda j, i: (i, 0)),
                  pl.BlockSpec((bn, k), lambda j, i: (j, 0), pipeline_mode=once),
                  pl.BlockSpec((LANES, k), lambda j, i: ((j + 1) * per, 0), pipeline_mode=once)],
        out_specs=pl.BlockSpec((bm, bn), lambda j, i: (i, j)),
        out_shape=jax.ShapeDtypeStruct((m, n_out), BF16),
        scratch_shapes=[pltpu.VMEM((k, bn), BF16)],
        compiler_params=_params("parallel", "arbitrary"),
        name="inproj",
    )(a, wt, wt)


def _gate_proj_kernel(a_ref, wa_ref, wb_ref, o_ref, *, n_a, off_b, n_b):
    k = wa_ref.shape[1]
    rows = jnp.concatenate([wa_ref[0:n_a, :], wb_ref[off_b:off_b + n_b, :],
                            jnp.zeros((LANES - n_a - n_b, k), F32)], axis=0).astype(BF16)
    o_ref[...] = lax.dot_general(a_ref[...], rows, _NT, preferred_element_type=F32)


def _gate_proj(a, wt, row_a, n_a, row_b, n_b):
    m, k = a.shape
    bm = min(MM_BM, m)
    off_b = row_b % LANES
    assert row_a % LANES == 0 and n_a % SUBLANES == 0 and off_b % SUBLANES == 0 and n_b % SUBLANES == 0
    assert off_b + n_b <= LANES and n_a + n_b <= LANES
    return pl.pallas_call(
        functools.partial(_gate_proj_kernel, n_a=n_a, off_b=off_b, n_b=n_b),
        grid=(m // bm,),
        in_specs=[pl.BlockSpec((bm, k), lambda i: (i, 0)),
                  pl.BlockSpec((LANES, k), lambda i: (row_a // LANES, 0)),
                  pl.BlockSpec((LANES, k), lambda i: (row_b // LANES, 0))],
        out_specs=pl.BlockSpec((bm, LANES), lambda i: (i, 0)),
        out_shape=jax.ShapeDtypeStruct((m, LANES), F32),
        compiler_params=_params("parallel"),
        name="gate_proj",
    )(a, wt, wt)


def _mma_kernel(a_ref, b_ref, o_ref, ab_ref, *, gelu, row_chunk):
    @pl.when(pl.program_id(1) == 0)
    def _():
        for r0 in range(0, ab_ref.shape[0], row_chunk):
            ab_ref[r0:r0 + row_chunk, :] = a_ref[r0:r0 + row_chunk, :].astype(BF16)

    acc = jnp.dot(ab_ref[...], b_ref[...], preferred_element_type=F32)
    if gelu:
        acc = _gelu(acc)
    o_ref[...] = acc.astype(o_ref.dtype)


def _matmul_a(a, b, out_dtype, gelu=False):
    m, k = a.shape
    n = b.shape[1]
    bm = min(MMA_BM, m)
    bn = min(MM_BN, n)
    return pl.pallas_call(
        functools.partial(_mma_kernel, gelu=gelu, row_chunk=min(128, bm)),
        grid=(m // bm, n // bn),
        in_specs=[pl.BlockSpec((bm, k), lambda i, j: (i, 0), pipeline_mode=pl.Buffered(1)),
                  pl.BlockSpec((k, bn), lambda i, j: (0, j))],
        out_specs=pl.BlockSpec((bm, bn), lambda i, j: (i, j)),
        out_shape=jax.ShapeDtypeStruct((m, n), out_dtype),
        scratch_shapes=[pltpu.VMEM((bm, k), BF16)],
        compiler_params=_params("parallel", "arbitrary"),
        name="matmul_a",
    )(a, b)


def _pos_columns(pos, width):
    lane = lax.broadcasted_iota(jnp.int32, (pos.shape[0], width), 1)
    coarse = (pos >> 6) << 6
    vals = jnp.where(lane < 3, coarse, jnp.where(lane < 6, pos & 63, 0))
    return vals.astype(F32).astype(BF16)


def _nsa_kernel(slopes_ref, q_ref, kcm_ref, vcm_ref, ks_ref, vs_ref, kw_ref, vw_ref, gz_ref, wck_ref, wcv_ref,
                pos_ref, o_ref, kc_ref, vc_ref, pext_ref, xs_ref, acc_ref, qext_ref, *, seq, tq, tk,
                heads_per_group):
    R = heads_per_group
    dk = NSA_HEAD_DIM
    g = pl.program_id(1)
    qi = pl.program_id(2)
    t0 = qi * tq
    n_cmp = (seq - CMP_BLOCK) // CMP_STRIDE + 1
    n_slc = seq // SLC_BLOCK
    n_sel = min(SLC_TOPN, n_slc)
    half = CMP_STRIDE * dk
    scale = dk ** -0.5
    c_exp = scale * math.log2(math.e)
    rows = R * tq

    @pl.when(qi == 0)
    def _():
        pext_ref[...] = _pos_columns(lax.broadcasted_iota(jnp.int32, (seq, 1), 0), LANES)
        cpos = lax.broadcasted_iota(jnp.int32, (LANES, 1), 0) * CMP_STRIDE + (CMP_BLOCK - 1)
        for src_ref, w_ref, dst, pi in ((kcm_ref, wck_ref, kc_ref, 0), (vcm_ref, wcv_ref, vc_ref, 1)):
            xs_ref[...] = src_ref[...].astype(F32)
            x = jnp.concatenate([xs_ref[pl.ds(l, LANES, stride=CMP_STRIDE), :].astype(BF16)
                                 for l in range(CMP_STRIDE)], axis=1)
            ya = jnp.dot(x, w_ref[0:half, :], preferred_element_type=F32)
            yb = jnp.dot(x, w_ref[half:2 * half, :], preferred_element_type=F32)
            pos = jnp.broadcast_to(pos_ref[pi], (SUBLANES, 2 * half)).astype(BF16)
            pb = jnp.dot(pos, w_ref[...], preferred_element_type=F32)[0:1, :]
            comp = (ya + pltpu.roll(yb, LANES - 1, 0) + pb).astype(BF16)
            dst[...] = jnp.concatenate([comp, _pos_columns(cpos, LANES)], axis=1) if pi == 0 else comp

        head_of_row = lax.broadcasted_iota(jnp.int32, (rows, LANES), 0) >> _log2(tq)
        lane6 = lax.broadcasted_iota(jnp.int32, (rows, LANES), 1)
        term = jnp.where(lane6 < 3, lane6, lane6 - 3)
        q_ext = jnp.zeros((rows, LANES), F32)
        for r in range(R):
            for k in range(3):
                q_ext = jnp.where((head_of_row == r) & (term == k) & (lane6 < 6), slopes_ref[g, r, k], q_ext)
        qext_ref[...] = q_ext.astype(BF16)

    row_id = lax.broadcasted_iota(jnp.int32, (rows, 1), 0)
    t_col = t0 + (row_id & (tq - 1))
    q = q_ref[...]
    qs = jnp.concatenate([q[:, r * dk:(r + 1) * dk] for r in range(R)], axis=0)
    qa = jnp.concatenate([qs, qext_ref[...]], axis=1)

    n_idx = lax.broadcasted_iota(jnp.int32, (rows, LANES), 1)
    cvalid = (t_col >= n_idx * CMP_STRIDE + CMP_BLOCK - 1) & (n_idx < n_cmp)
    sc = lax.dot_general(qa, kc_ref[...], _NT, preferred_element_type=F32)
    sc = jnp.where(cvalid, sc, NEG_INF)
    mc = jnp.max(sc, axis=-1, keepdims=True)
    p_cmp = jnp.where(cvalid, jnp.exp2((sc - mc) * c_exp), 0.0)
    p_cmp = p_cmp * (1.0 / jnp.maximum(jnp.sum(p_cmp, axis=-1, keepdims=True), TINY))
    o_cmp = jnp.dot(p_cmp.astype(BF16), vc_ref[...], preferred_element_type=F32)

    psum = p_cmp[0:tq]
    for r in range(1, R):
        psum = psum + p_cmp[r * tq:(r + 1) * tq]
    jj = lax.broadcasted_iota(jnp.int32, (LANES, LANES), 0)
    nn = lax.broadcasted_iota(jnp.int32, (LANES, LANES), 1)
    ovl = _ind((nn * CMP_STRIDE < jj * SLC_BLOCK + SLC_BLOCK) & (nn * CMP_STRIDE + CMP_BLOCK > jj * SLC_BLOCK)
               & (jj < n_slc) & (nn < n_cmp), BF16)
    imp_t = jnp.zeros((LANES, tq), F32)
    for part in _split3(psum):
        imp_t = imp_t + lax.dot_general(ovl, part, _NT, preferred_element_type=F32)
    imp_t = imp_t[0:n_slc]
    j_i = lax.broadcasted_iota(jnp.int32, (n_slc, tq), 0)
    t_i = t0 + lax.broadcasted_iota(jnp.int32, (n_slc, tq), 1)
    cur = t_i >> _log2(SLC_BLOCK)
    causal = j_i * SLC_BLOCK <= t_i
    forced = (j_i == 0) | (j_i == cur) | (j_i == cur - 1)
    score = jnp.where(causal, jnp.where(forced, BIG, imp_t), -BIG)
    rank = jnp.zeros((n_slc, tq), jnp.int32)
    for jp in range(n_slc):
        row = score[jp:jp + 1, :]
        beats = (row > score) | ((row == score) & (j_i > jp))
        rank = rank + _ind(beats, jnp.int32)
    sel_t = jnp.where(rank < n_sel, 0.0, NEG_INF)
    sel_t = jnp.concatenate([sel_t, jnp.zeros((LANES - n_slc, tq), F32)], axis=0)
    sel_neg = sel_t.T.astype(BF16)
    tq_col = t0 + lax.broadcasted_iota(jnp.int32, (tq, 1), 0)

    def attend(s, v):
        p = jnp.exp2((s - jnp.max(s, axis=-1, keepdims=True)) * c_exp).astype(BF16)
        ones = _ind(lax.broadcasted_iota(jnp.int32, (v.shape[0], LANES), 1) == 0, BF16)
        pv = jnp.dot(p, jnp.concatenate([v, ones], axis=1), preferred_element_type=F32)
        return pv[:, 0:dk] * (1.0 / jnp.maximum(pv[:, dk:dk + 1], TINY))

    def slc_variant(nk):
        ka = jnp.concatenate([ks_ref[0:nk, :], pext_ref[0:nk, :]], axis=1)
        ej = lax.broadcasted_iota(jnp.int32, (LANES, nk), 0)
        ec = lax.broadcasted_iota(jnp.int32, (LANES, nk), 1)
        expand = _ind(ej == (ec >> _log2(SLC_BLOCK)), BF16)
        madd = jnp.dot(sel_neg, expand, preferred_element_type=F32)
        madd = madd + jnp.where(lax.broadcasted_iota(jnp.int32, (tq, nk), 1) <= tq_col, 0.0, NEG_INF)
        s = lax.dot_general(qa, ka, _NT, preferred_element_type=F32) + jnp.concatenate([madd] * R, axis=0)
        acc_ref[...] = attend(s, vs_ref[0:nk, :])

    variant = (t0 + tq - 1) // tk
    for vi in range(seq // tk):
        pl.when(variant == vi)(functools.partial(slc_variant, (vi + 1) * tk))
    o_slc = acc_ref[...]

    span = WINDOW + tq
    ws = pl.multiple_of(jnp.maximum(t0 - WINDOW, 0), tq)
    kwa = jnp.concatenate([kw_ref[pl.ds(ws, span), :], pext_ref[pl.ds(ws, span), :]], axis=1)
    dw_i = tq_col - (ws + lax.broadcasted_iota(jnp.int32, (tq, span), 1))
    wadd = jnp.where((dw_i >= 0) & (dw_i < WINDOW), 0.0, NEG_INF)
    sw = lax.dot_general(qa, kwa, _NT, preferred_element_type=F32) + jnp.concatenate([wadd] * R, axis=0)
    o_win = attend(sw, vw_ref[pl.ds(ws, span), :])

    gz = gz_ref[...]
    gsel = jnp.zeros((tq, 3 * R), F32)
    for gg in range(NSA_KV_GROUPS):
        gsel = jnp.where(g == gg, gz[:, 3 * R * gg:3 * R * (gg + 1)], gsel)
    gts = jax.nn.sigmoid(gsel)
    for r in range(R):
        sl = slice(r * tq, (r + 1) * tq)
        o = (gts[:, 3 * r:3 * r + 1] * o_cmp[sl] + gts[:, 3 * r + 1:3 * r + 2] * o_slc[sl]
             + gts[:, 3 * r + 2:3 * r + 3] * o_win[sl])
        o_ref[:, r * dk:(r + 1) * dk] = o.astype(o_ref.dtype)


def _nsa(slopes, z, gz, wck, wcv, pos, *, batch, seq, kv_col):
    G = NSA_KV_GROUPS
    R = NSA_HEADS // G
    dk = NSA_HEAD_DIM
    tq = NSA_TQ
    tk = NSA_TK
    nq = seq // tq
    assert seq // CMP_STRIDE == LANES and CMP_BLOCK == 2 * CMP_STRIDE and seq >= WINDOW + tq and seq % tk == 0
    kv0 = kv_col // dk

    def kv_spec(i):
        return pl.BlockSpec((seq, dk), lambda b, g, qi: (b, kv0 + i * G + g))

    w_spec = pl.BlockSpec((CMP_BLOCK * dk, dk), lambda b, g, qi: (0, 0))
    rows = R * tq
    assert dk == LANES and 3 * NSA_HEADS <= LANES
    return pl.pallas_call(
        functools.partial(_nsa_kernel, seq=seq, tq=tq, tk=tk, heads_per_group=R),
        grid=(batch, G, nq),
        in_specs=[
            pl.BlockSpec(memory_space=pltpu.SMEM),
            pl.BlockSpec((tq, R * dk), lambda b, g, qi: (b * nq + qi, g)),
            kv_spec(0), kv_spec(1), kv_spec(2), kv_spec(3), kv_spec(4), kv_spec(5),
            pl.BlockSpec((tq, LANES), lambda b, g, qi: (b * nq + qi, 0)),
            w_spec, w_spec,
            pl.BlockSpec((2, 1, CMP_BLOCK * dk), lambda b, g, qi: (0, 0, 0)),
        ],
        out_specs=pl.BlockSpec((tq, R * dk), lambda b, g, qi: (b * nq + qi, g)),
        out_shape=jax.ShapeDtypeStruct((batch * seq, NSA_HEADS * dk), BF16),
        scratch_shapes=[
            pltpu.VMEM((LANES, 2 * dk), BF16), pltpu.VMEM((LANES, dk), BF16),
            pltpu.VMEM((seq, LANES), BF16), pltpu.VMEM((seq, dk), F32), pltpu.VMEM((rows, dk), F32),
            pltpu.VMEM((rows, LANES), BF16),
        ],
        compiler_params=_params("parallel", "parallel", "arbitrary"),
        name="nsa",
    )(slopes, z, z, z, z, z, z, z, gz, wck, wcv, pos)


def _log_sigmoid(x):
    return jnp.minimum(x, 0.0) - jnp.log1p(jnp.exp(-jnp.abs(x)))


def _mlstm_kernel(bias_ref, q_ref, k_ref, v_ref, og_ref, gz_ref, nw_ref, o_ref, c_ref, *, seq, L, gate_lane, hps):
    dqk = ML_QK_DIM
    dv = ML_V_DIM
    nc = seq // L
    lane = lax.broadcasted_iota(jnp.int32, (L, LANES), 1)

    def gate_forms(blk, which, bias):
        col = jnp.sum(jnp.where(lane == which, blk, 0.0), axis=1, keepdims=True) + bias
        return col, jnp.broadcast_to(col, (L, LANES)).T[0:1, :]

    ext = dv + LANES
    ii = lax.broadcasted_iota(jnp.int32, (L, L), 0)
    kk = lax.broadcasted_iota(jnp.int32, (L, L), 1)
    causal = kk <= ii
    tri = _ind(causal, BF16)
    tri_t = _ind(ii <= kk, BF16)
    ones_slab = _ind(lax.broadcasted_iota(jnp.int32, (L, LANES), 1) == 0, BF16)
    c_ref[...] = jnp.zeros((hps, dqk, ext), F32)

    def cumsum_mats(lf_row, lf_col):
        lr = jnp.broadcast_to(lf_row, (L, L))
        lc = jnp.broadcast_to(lf_col, (L, L))
        b_row = jnp.zeros((L, L), F32)
        b_col = jnp.zeros((L, L), F32)
        for part in _split3(lr)[:2]:
            b_row = b_row + jnp.dot(part, tri_t, preferred_element_type=F32)
        for part in _split3(lc)[:2]:
            b_col = b_col + jnp.dot(tri, part, preferred_element_type=F32)
        return b_row, b_col

    def head_chunk(hh, r0, gblk, m):
        h_id = pl.program_id(1) * hps + hh
        i_lane = gate_lane + h_id
        f_lane = i_lane + ML_HEADS
        qc = q_ref[pl.ds(r0, L), hh * dqk:(hh + 1) * dqk]
        kc = k_ref[pl.ds(r0, L), hh * dqk:(hh + 1) * dqk] * (dqk ** -0.5)
        v_ext = jnp.concatenate([v_ref[pl.ds(r0, L), hh * dv:(hh + 1) * dv], ones_slab], axis=1)
        i_col, i_row = gate_forms(gblk, i_lane, bias_ref[0, h_id])
        f_col, f_row = gate_forms(gblk, f_lane, bias_ref[1, h_id])
        lf_col = _log_sigmoid(f_col)
        lf_row = _log_sigmoid(f_row)
        b_row, b_col = cumsum_mats(lf_row, lf_col)
        log_d = jnp.where(causal, b_col - b_row + i_row, NEG_INF)
        b_c = b_col[:, 0:1]
        m_inter = b_c + m
        m_t = jnp.maximum(m_inter, jnp.max(log_d, axis=-1, keepdims=True))
        d = jnp.exp(log_d - m_t)
        s = lax.dot_general(qc, kc, _NT, preferred_element_type=F32) * d
        w_inter = jnp.exp(m_inter - m_t)
        inter = jnp.dot(qc, c_ref[hh].astype(BF16), preferred_element_type=F32)
        intra = jnp.dot(s.astype(BF16), v_ext, preferred_element_type=F32)
        numden = w_inter * inter + intra
        den = numden[:, dv:dv + 1]
        h = numden[:, 0:dv] / jnp.maximum(jnp.abs(den), jnp.exp(-m_t))
        g_tot = b_row[0:1, L - 1:L]
        a_row = g_tot - b_row[0:1, :] + i_row
        a_col = g_tot - b_c + i_col
        m_new = jnp.maximum(g_tot + m, jnp.max(a_row, axis=-1, keepdims=True))
        decay = jnp.exp(g_tot + m - m_new)
        w_tok = jnp.exp(a_col - m_new)
        kw_t = (kc.astype(F32) * w_tok).T.astype(BF16)
        c_ref[hh] = decay * c_ref[hh] + jnp.dot(kw_t, v_ext, preferred_element_type=F32)
        hn = _rms(h, nw_ref[hh]) * jax.nn.sigmoid(og_ref[pl.ds(r0, L), hh * dv:(hh + 1) * dv].astype(F32))
        o_ref[pl.ds(r0, L), hh * dv:(hh + 1) * dv] = hn.astype(o_ref.dtype)
        return m_new

    def body(c, ms):
        r0 = pl.multiple_of(c * L, L)
        gblk = gz_ref[pl.ds(r0, L), :]
        return tuple(head_chunk(hh, r0, gblk, ms[hh]) for hh in range(hps))

    lax.fori_loop(0, nc, body, tuple(jnp.zeros((1, 1), F32) for _ in range(hps)))


def _mlstm(gate_bias, z, gz, norm_w, *, batch, seq, q_col, k_col, v_col, o_col, gate_lane):
    H = ML_HEADS
    dqk = ML_QK_DIM
    dv = ML_V_DIM
    L = min(ML_CHUNK_LEN, seq)
    hps = ML_HEADS_PER_STEP
    qw, vw = hps * dqk, hps * dv
    assert L % LANES == 0 and seq % L == 0 and gate_lane + 2 * H <= LANES and H % hps == 0
    assert q_col % qw == 0 and k_col % qw == 0 and v_col % vw == 0 and o_col % vw == 0
    return pl.pallas_call(
        functools.partial(_mlstm_kernel, seq=seq, L=L, gate_lane=gate_lane, hps=hps),
        grid=(batch, H // hps),
        in_specs=[
            pl.BlockSpec(memory_space=pltpu.SMEM),
            pl.BlockSpec((seq, qw), lambda b, h: (b, q_col // qw + h)),
            pl.BlockSpec((seq, qw), lambda b, h: (b, k_col // qw + h)),
            pl.BlockSpec((seq, vw), lambda b, h: (b, v_col // vw + h)),
            pl.BlockSpec((seq, vw), lambda b, h: (b, o_col // vw + h)),
            pl.BlockSpec((seq, LANES), lambda b, h: (b, 0)),
            pl.BlockSpec((hps, 1, dv), lambda b, h: (h, 0, 0)),
        ],
        out_specs=pl.BlockSpec((seq, vw), lambda b, h: (b, h)),
        out_shape=jax.ShapeDtypeStruct((batch * seq, H * dv), BF16),
        scratch_shapes=[pltpu.VMEM((hps, dqk, dv + LANES), F32)],
        compiler_params=_params("parallel", "parallel"),
        name="mlstm",
    )(gate_bias, z, z, z, z, gz, norm_w.reshape(H, 1, dv))


def _peer_pairs():
    return [(a, (PEER_TOPK + 1) // (a + 1)) for a in range(PEER_TOPK + 1)]


def _topk_multiset(s, k, with_rank=False):
    tt = s.shape[1]
    riota = lax.broadcasted_iota(jnp.int32, (k, tt), 0).astype(F32)
    v = jnp.full((k, tt), -jnp.inf, F32)
    taken = jnp.zeros((1, tt), F32)
    rank = jnp.full(s.shape, float(s.shape[0]), F32)
    rem = s
    for _ in range(k):
        m = jnp.max(rem, axis=0, keepdims=True)
        eq = rem == m
        cnt = jnp.sum(_ind(eq, F32), axis=0, keepdims=True)
        v = jnp.where((riota >= taken) & (riota < taken + cnt), m, v)
        if with_rank:
            rank = jnp.where(eq, taken, rank)
        taken = taken + cnt
        rem = jnp.where(eq, -jnp.inf, rem)
    return (v, rank) if with_rank else v


def _peer_topk_kernel(q_ref, khi_ref, klo_ref, cnt_ref, rank_ref, e1_ref, e2_ref, cand_ref):
    H = PEER_HEADS
    half = PEER_KEY_DIM // 2
    K = PEER_TOPK
    pairs = _peer_pairs()
    n_cand = sum(nb for _, nb in pairs)
    tt = q_ref.shape[0]
    for h in range(H):
        s_parts = []
        for side in range(2):
            qf = q_ref[:, (2 * h + side) * half:(2 * h + side + 1) * half]
            q_hi = qf.astype(BF16)
            q_lo = (qf - q_hi.astype(F32)).astype(BF16)
            k_hi = khi_ref[side, h]
            k_lo = klo_ref[side, h]
            s = (lax.dot_general(k_hi, q_hi, _NT, preferred_element_type=F32)
                 + lax.dot_general(k_hi, q_lo, _NT, preferred_element_type=F32)
                 + lax.dot_general(k_lo, q_hi, _NT, preferred_element_type=F32))
            s_parts.append(s)
        s1, s2 = s_parts
        v1 = _topk_multiset(s1, K + 1)
        v2, rank2 = _topk_multiset(s2, K + 1, with_rank=True)
        off = 0
        for a, nb in pairs:
            cand_ref[off:off + nb, :] = v1[a:a + 1, :] + v2[0:nb, :]
            off += nb
        cand_ref[n_cand:, :] = jnp.full((cand_ref.shape[0] - n_cand, tt), -jnp.inf, F32)
        rem = cand_ref[...]
        taken = jnp.zeros((1, tt), F32)
        tau16 = jnp.zeros((1, tt), F32)
        tau17 = jnp.zeros((1, tt), F32)
        zsum = jnp.zeros((1, tt), F32)
        cmax = v1[0:1, :] + v2[0:1, :]
        for _ in range(K + 1):
            m = jnp.max(rem, axis=0, keepdims=True)
            eq = rem == m
            cnt = jnp.sum(_ind(eq, F32), axis=0, keepdims=True)
            take = jnp.minimum(cnt, jnp.maximum(K - taken, 0.0))
            zsum = zsum + take * jnp.exp(m - cmax)
            after = taken + cnt
            tau16 = jnp.where((taken < K) & (after >= K), m, tau16)
            tau17 = jnp.where((taken <= K) & (after > K), m, tau17)
            taken = after
            rem = jnp.where(eq, -jnp.inf, rem)
        theta = 0.5 * (tau16 + tau17)
        r1 = theta - s1
        cnt1 = jnp.zeros(r1.shape, F32)
        for b in range(K + 1):
            cnt1 = cnt1 + _ind(v2[b:b + 1, :] >= r1, F32)
        cnt_ref[h] = cnt1
        rank_ref[h] = rank2.astype(rank_ref.dtype)
        e1_ref[h] = jnp.exp(s1 - v1[0:1, :])
        e2_ref[h] = (jnp.exp(s2 - v2[0:1, :]) / zsum).astype(e2_ref.dtype)


def _peer_topk(pq, k_hi, k_lo):
    n = pq.shape[0]
    H = PEER_HEADS
    nk = PEER_N_KEYS
    tt = min(PEER_TOPK_TT, n)
    n_cand = sum(nb for _, nb in _peer_pairs())
    cand_rows = -(-n_cand // SUBLANES) * SUBLANES
    key_spec = pl.BlockSpec((2, H, nk, PEER_KEY_DIM // 2), lambda i: (0, 0, 0, 0))
    hkt = pl.BlockSpec((H, nk, tt), lambda i: (0, 0, i))
    wide = jax.ShapeDtypeStruct((H, nk, n), F32)
    narrow = jax.ShapeDtypeStruct((H, nk, n), BF16)
    return pl.pallas_call(
        _peer_topk_kernel,
        grid=(n // tt,),
        in_specs=[pl.BlockSpec((tt, pq.shape[1]), lambda i: (i, 0)), key_spec, key_spec],
        out_specs=[hkt, hkt, hkt, hkt],
        out_shape=[wide, narrow, wide, narrow],
        scratch_shapes=[pltpu.VMEM((cand_rows, tt), F32)],
        compiler_params=_params("parallel"),
        name="peer_topk",
    )(pq, k_hi, k_lo)


def _bcast_packed(row):
    half = jnp.broadcast_to(row, (SUBLANES, LANES))
    return jnp.concatenate([half, half], axis=0).astype(BF16)


def _peer_mix_kernel(act_ref, vt_ref, cnt_ref, e1_ref, rank_ref, e2_ref, o_ref, acc_ref, a_ref, *, chunk):
    H = PEER_HEADS
    nk = PEER_N_KEYS
    te, tt = a_ref.shape
    e = pl.program_id(1)
    pack = 2 * SUBLANES

    @pl.when(e == 0)
    def _():
        acc_ref[...] = jnp.zeros(acc_ref.shape, F32)

    for i in range(te // nk):
        cnt_rows = [cnt_ref[h, i:i + 1, :] for h in range(H)]
        e1_rows = [e1_ref[h, i:i + 1, :] for h in range(H)]
        for lt in range(tt // LANES):
            ls = slice(lt * LANES, (lt + 1) * LANES)
            cntb = [_bcast_packed(cnt_rows[h][:, ls]) for h in range(H)]
            e1b = [_bcast_packed(e1_rows[h][:, ls]) for h in range(H)]
            for sg in range(nk // pack):
                rs = slice(sg * pack, (sg + 1) * pack)
                w = jnp.zeros((pack, LANES), BF16)
                for h in range(H):
                    hit = rank_ref[h, rs, ls] < cntb[h]
                    w = w + e1b[h] * jnp.where(hit, e2_ref[h, rs, ls], jnp.zeros((), BF16))
                rr = slice(i * nk + sg * pack, i * nk + (sg + 1) * pack)
                a_ref[rr, ls] = act_ref[rr, ls] * w
        done = (i + 1) * nk
        if done % chunk == 0:
            ks = slice(done - chunk, done)
            acc_ref[...] += jnp.dot(vt_ref[:, ks], a_ref[ks, :], preferred_element_type=F32)

    @pl.when(e == pl.num_programs(1) - 1)
    def _():
        o_ref[...] = acc_ref[...].T.astype(o_ref.dtype)


def _peer_mix(act_t, pvt, cnt1, rank2, e1, e2):
    n_exp, n = act_t.shape
    d = pvt.shape[0]
    H = PEER_HEADS
    nk = PEER_N_KEYS
    tt = min(PEER_TT, n)
    te = PEER_TE
    assert te == SUBLANES * nk and n_exp % te == 0

    def i1_spec():
        return pl.BlockSpec((H, SUBLANES, tt), lambda i, e: (0, e, i))

    def i2_spec():
        return pl.BlockSpec((H, nk, tt), lambda i, e: (0, 0, i))

    return pl.pallas_call(
        functools.partial(_peer_mix_kernel, chunk=PEER_DOT_CHUNK),
        grid=(n // tt, n_exp // te),
        in_specs=[
            pl.BlockSpec((te, tt), lambda i, e: (e, i)),
            pl.BlockSpec((d, te), lambda i, e: (0, e)),
            i1_spec(), i1_spec(), i2_spec(), i2_spec(),
        ],
        out_specs=pl.BlockSpec((tt, d), lambda i, e: (i, 0)),
        out_shape=jax.ShapeDtypeStruct((n, d), BF16),
        scratch_shapes=[pltpu.VMEM((d, tt), F32), pltpu.VMEM((te, tt), BF16)],
        compiler_params=_params("parallel", "arbitrary"),
        name="peer_mix",
    )(act_t, pvt, cnt1, e1, rank2, e2)


def _cast_transpose_kernel(x_ref, o_ref):
    o_ref[...] = x_ref[...].T.astype(o_ref.dtype)


def _cast_transpose(x, dtype):
    r, c = x.shape
    tr = min(CAST_T_ROWS, r)
    return pl.pallas_call(
        _cast_transpose_kernel,
        grid=(r // tr,),
        in_specs=[pl.BlockSpec((tr, c), lambda i: (i, 0))],
        out_specs=pl.BlockSpec((c, tr), lambda i: (0, i)),
        out_shape=jax.ShapeDtypeStruct((c, r), dtype),
        compiler_params=_params("parallel"),
        name="cast_transpose",
    )(x)


def _layer(h, mod3, norm_pre_mix, norm_post_mix, norm_pre_ffn, norm_post_ffn, w_in, cmp_pos, w_cmp_k, w_cmp_v,
           gate_bias, ml_norm_w, w_up_nsa, w_up_mlstm, w_out, peer_w_q, sub_keys, peer_u, peer_v, *, batch, seq):
    d = h.shape[1]
    G = NSA_KV_GROUPS
    R = NSA_HEADS // G
    dk = NSA_HEAD_DIM
    H = ML_HEADS
    nsa_w = NSA_HEADS * dk
    kv_w = 6 * G * dk
    mq_w = H * ML_QK_DIM
    mv_w = H * ML_V_DIM
    c_kv = nsa_w
    c_gn = c_kv + kv_w
    c_mq = c_gn + 3 * NSA_HEADS
    c_mk = c_mq + mq_w
    c_mv = c_mk + mq_w
    c_mo = c_mv + mv_w
    c_if = c_mo + mv_w
    c_mg = c_if + 2 * H
    n_gate = 3 * NSA_HEADS + 2 * H
    z_kv = nsa_w
    z_mq = z_kv + kv_w
    z_mk = z_mq + mq_w
    z_mv = z_mk + mq_w
    z_mo = z_mv + mv_w
    z_mg = z_mo + mv_w
    z_w = z_mg + 2 * d
    assert z_mq % INPROJ_BN == 0 and z_mg % INPROJ_BN == 0 and z_w % INPROJ_BN == 0
    z_shifts = [(0, z_mq // INPROJ_BN, 0), (z_mq // INPROJ_BN, z_mg // INPROJ_BN, 3 * NSA_HEADS),
                (z_mg // INPROJ_BN, z_w // INPROJ_BN, n_gate)]

    u = _prenorm(h, norm_pre_mix, mod3, seq)
    w_in_t = w_in.T
    z = _inproj(u, w_in_t, z_w, z_shifts)
    gz = _gate_proj(u, w_in_t, c_gn, 3 * NSA_HEADS, c_if, 2 * H)

    slopes = jnp.exp2(-8.0 * jnp.arange(1, NSA_HEADS + 1, dtype=F32) / NSA_HEADS) / (dk ** -0.5)
    slopes = jnp.stack([t.astype(F32) for t in _split3(slopes)], axis=-1).reshape(G, R, 3)
    o_nsa = _nsa(slopes, z, gz, w_cmp_k.astype(BF16), w_cmp_v.astype(BF16),
                 cmp_pos.reshape(2, 1, CMP_BLOCK * dk), batch=batch, seq=seq, kv_col=z_kv)

    h_ml = _mlstm(gate_bias, z, gz, ml_norm_w, batch=batch, seq=seq, q_col=z_mq, k_col=z_mk, v_col=z_mv,
                  o_col=z_mo, gate_lane=3 * NSA_HEADS)

    t1 = _matmul_w(o_nsa, w_up_nsa, BF16, gate=z, gate_col=z_mg)
    mix = _matmul_w(h_ml, w_up_mlstm, BF16, gate=z, gate_col=z_mg + d, add=t1)
    y = _matmul_w(mix, w_out, BF16)
    h1, u2, u2_t = _midnorm(h, y, norm_post_mix, norm_pre_ffn, mod3, seq)

    pq = _matmul_w(u2, peer_w_q, F32)
    k_hi = sub_keys.astype(BF16)
    k_lo = (sub_keys - k_hi.astype(F32)).astype(BF16)
    cnt1, rank2, e1, e2 = _peer_topk(pq, k_hi, k_lo)
    act_t = _matmul_a(peer_u, u2_t, BF16, gelu=True)
    y2 = _peer_mix(act_t, _cast_transpose(peer_v, BF16), cnt1, rank2, e1, e2)
    return _finalnorm(h1, y2, norm_post_ffn, mod3, seq)


def kernel(x, c, w_ada, b_ada, norm_pre_mix, norm_post_mix, norm_pre_ffn, norm_post_ffn, w_in, nsa_cmp_pos, nsa_w_cmp_k, nsa_w_cmp_v, mlstm_gate_bias, mlstm_norm_w, w_up_nsa, w_up_mlstm, w_out, peer_w_q, peer_sub_keys, peer_u, peer_v):
    batch, seq, d = x.shape
    depth = w_ada.shape[0]
    h = x.reshape(batch * seq, d)
    c_pad = jnp.concatenate([c, jnp.zeros((SUBLANES - batch % SUBLANES, d), c.dtype)], axis=0)
    for l in range(depth):
        mod = _adaln(c_pad, w_ada[l], b_ada[l])[:batch]
        mod3 = mod.reshape(batch, 6, d)
        h = _layer(h, mod3, norm_pre_mix[l], norm_post_mix[l], norm_pre_ffn[l], norm_post_ffn[l], w_in[l],
                   nsa_cmp_pos[l], nsa_w_cmp_k[l], nsa_w_cmp_v[l], mlstm_gate_bias[l], mlstm_norm_w[l],
                   w_up_nsa[l], w_up_mlstm[l], w_out[l], peer_w_q[l], peer_sub_keys[l], peer_u[l], peer_v[l],
                   batch=batch, seq=seq)
    return h.reshape(batch, seq, d).astype(x.dtype)
```

```python
import functools
import math

import jax
import jax.numpy as jnp
from jax import lax
from jax.experimental import pallas as pl
from jax.experimental.pallas import tpu as pltpu

D_MODEL = 4096
BATCH = 4
SEQ = 2048
NSA_HEADS = 16
NSA_KV_GROUPS = 4
NSA_HEAD_DIM = 128
CMP_BLOCK = 32
CMP_STRIDE = 16
SLC_BLOCK = 64
SLC_TOPN = 16
WINDOW = 512
ML_HEADS = 8
ML_QK_DIM = 256
ML_V_DIM = 512
PEER_HEADS = 8
PEER_KEY_DIM = 256
PEER_N_KEYS = 128
PEER_TOPK = 16

NEG_INF = -1e30
BIG = 1e9
TINY = 1e-30
EPS = 1e-6

F32 = jnp.float32
BF16 = jnp.bfloat16

V7X_VMEM_LIMIT_BYTES = 56 * 1024 * 1024
LANES = 128
SUBLANES = 8

ROW_TILE = 256
MM_BM = 1024
MM_BN = 1024
MMW_BN = 512
INPROJ_BN = 512
MMW_VMEM_BUDGET_BYTES = 50 * 1024 * 1024
MMA_BM = 1024
CAST_T_ROWS = 512
ADA_BN = 512
NSA_TQ = 256
NSA_TK = 512
ML_CHUNK_LEN = 256
ML_HEADS_PER_STEP = 2
PEER_TOPK_TT = 256
PEER_TT = 512
PEER_TE = 1024
PEER_DOT_CHUNK = 256

_NT = (((1,), (1,)), ((), ()))


def _params(*sem):
    return pltpu.CompilerParams(dimension_semantics=sem, vmem_limit_bytes=V7X_VMEM_LIMIT_BYTES)


def _ind(cond, dtype):
    wide = jnp.int32 if jnp.issubdtype(dtype, jnp.integer) else F32
    return jnp.where(cond, jnp.ones((), wide), jnp.zeros((), wide)).astype(dtype)


def _log2(n):
    k = int(math.log2(n))
    assert 1 << k == n
    return k


def _split3(x):
    hi = x.astype(BF16)
    r1 = x - hi.astype(F32)
    mid = r1.astype(BF16)
    lo = (r1 - mid.astype(F32)).astype(BF16)
    return hi, mid, lo


def _adaln_kernel(c_ref, w_ref, b_ref, o_ref):
    c = c_ref[...]
    cond = (c * jax.nn.sigmoid(c)).astype(BF16)
    o_ref[...] = jnp.dot(cond, w_ref[...].astype(BF16), preferred_element_type=F32) + b_ref[...]


def _adaln(c_pad, w_ada, b_ada):
    rows, d = c_pad.shape
    n = w_ada.shape[1]
    bn = min(ADA_BN, n)
    return pl.pallas_call(
        _adaln_kernel,
        grid=(n // bn,),
        in_specs=[
            pl.BlockSpec((rows, d), lambda j: (0, 0)),
            pl.BlockSpec((d, bn), lambda j: (0, j)),
            pl.BlockSpec((1, bn), lambda j: (0, j)),
        ],
        out_specs=pl.BlockSpec((rows, bn), lambda j: (0, j)),
        out_shape=jax.ShapeDtypeStruct((rows, n), F32),
        compiler_params=_params("parallel"),
        name="adaln",
    )(c_pad, w_ada, b_ada.reshape(1, n))


def _rms(x, w):
    return x * lax.rsqrt(jnp.mean(x * x, axis=-1, keepdims=True) + EPS) * w


def _prenorm_kernel(x_ref, w_ref, mod_ref, u_ref):
    y = _rms(x_ref[...], w_ref[...])
    u_ref[...] = (y * (1.0 + mod_ref[1:2, :]) + mod_ref[0:1, :]).astype(u_ref.dtype)


def _prenorm(x2, w, mod3, seq):
    n, d = x2.shape
    tr = min(ROW_TILE, seq)
    return pl.pallas_call(
        _prenorm_kernel,
        grid=(n // tr,),
        in_specs=[
            pl.BlockSpec((tr, d), lambda i: (i, 0)),
            pl.BlockSpec((1, d), lambda i: (0, 0)),
            pl.BlockSpec((None, 6, d), lambda i: ((i * tr) // seq, 0, 0)),
        ],
        out_specs=pl.BlockSpec((tr, d), lambda i: (i, 0)),
        out_shape=jax.ShapeDtypeStruct((n, d), BF16),
        compiler_params=_params("parallel"),
        name="prenorm",
    )(x2, w.reshape(1, d), mod3)


def _midnorm_kernel(x_ref, y_ref, w1_ref, w2_ref, mod_ref, h_ref, u_ref, ut_ref):
    h = x_ref[...] + mod_ref[2:3, :] * _rms(y_ref[...].astype(F32), w1_ref[...])
    h_ref[...] = h
    u = _rms(h, w2_ref[...]) * (1.0 + mod_ref[4:5, :]) + mod_ref[3:4, :]
    u_ref[...] = u.astype(u_ref.dtype)
    ut_ref[...] = u.T.astype(ut_ref.dtype)


def _midnorm(x2, y, w_post, w_pre, mod3, seq):
    n, d = x2.shape
    tr = min(ROW_TILE, seq)
    row = pl.BlockSpec((tr, d), lambda i: (i, 0))
    vec = pl.BlockSpec((1, d), lambda i: (0, 0))
    return pl.pallas_call(
        _midnorm_kernel,
        grid=(n // tr,),
        in_specs=[row, row, vec, vec, pl.BlockSpec((None, 6, d), lambda i: ((i * tr) // seq, 0, 0))],
        out_specs=[row, row, pl.BlockSpec((d, tr), lambda i: (0, i))],
        out_shape=[jax.ShapeDtypeStruct((n, d), F32), jax.ShapeDtypeStruct((n, d), BF16),
                   jax.ShapeDtypeStruct((d, n), BF16)],
        compiler_params=_params("parallel"),
        name="midnorm",
    )(x2, y, w_post.reshape(1, d), w_pre.reshape(1, d), mod3)


def _finalnorm_kernel(h_ref, y_ref, w_ref, mod_ref, o_ref):
    o_ref[...] = h_ref[...] + mod_ref[5:6, :] * _rms(y_ref[...].astype(F32), w_ref[...])


def _finalnorm(h1, y, w, mod3, seq):
    n, d = h1.shape
    tr = min(ROW_TILE, seq)
    row = pl.BlockSpec((tr, d), lambda i: (i, 0))
    return pl.pallas_call(
        _finalnorm_kernel,
        grid=(n // tr,),
        in_specs=[row, row, pl.BlockSpec((1, d), lambda i: (0, 0)),
                  pl.BlockSpec((None, 6, d), lambda i: ((i * tr) // seq, 0, 0))],
        out_specs=row,
        out_shape=jax.ShapeDtypeStruct((n, d), F32),
        compiler_params=_params("parallel"),
        name="finalnorm",
    )(h1, y, w.reshape(1, d), mod3)


def _gelu(x):
    return 0.5 * x * (1.0 + lax.erf(x * (2.0 ** -0.5)))


def _mmw_kernel(*refs, has_gate, has_add, row_chunk):
    a_ref, w_ref = refs[0], refs[1]
    nxt = 2
    o_ref, wb_ref = refs[-2], refs[-1]

    @pl.when(pl.program_id(1) == 0)
    def _():
        for r0 in range(0, wb_ref.shape[0], row_chunk):
            wb_ref[r0:r0 + row_chunk, :] = w_ref[r0:r0 + row_chunk, :].astype(BF16)

    acc = jnp.dot(a_ref[...], wb_ref[...], preferred_element_type=F32)
    if has_gate:
        acc = acc * jax.nn.sigmoid(refs[nxt][...].astype(F32))
        nxt += 1
    if has_add:
        acc = acc + refs[nxt][...].astype(F32)
    o_ref[...] = acc.astype(o_ref.dtype)


def _matmul_w(a, w, out_dtype, *, gate=None, gate_col=0, add=None):
    m, k = a.shape
    n = w.shape[1]
    bm = min(MM_BM, m)
    n_side = (gate is not None) + (add is not None)
    out_bytes = jnp.dtype(out_dtype).itemsize

    def vmem_bytes(bn, w_bufs):
        return (w_bufs * k * bn * 4 + k * bn * 2 + 2 * bm * k * 2 + 2 * bm * bn * (out_bytes + 2 * n_side)
                + bm * bn * 4)

    if n % MM_BN == 0 and vmem_bytes(MM_BN, 1) <= MMW_VMEM_BUDGET_BYTES:
        bn, w_mode = MM_BN, pl.Buffered(1)
    else:
        bn, w_mode = min(MMW_BN, n), None
    assert n % bn == 0 and m % bm == 0
    in_specs = [pl.BlockSpec((bm, k), lambda j, i: (i, 0)),
                pl.BlockSpec((k, bn), lambda j, i: (0, j), pipeline_mode=w_mode)]
    args = [a, w]
    if gate is not None:
        goff = gate_col // bn
        assert goff * bn == gate_col
        in_specs.append(pl.BlockSpec((bm, bn), lambda j, i: (i, goff + j)))
        args.append(gate)
    if add is not None:
        in_specs.append(pl.BlockSpec((bm, bn), lambda j, i: (i, j)))
        args.append(add)
    return pl.pallas_call(
        functools.partial(_mmw_kernel, has_gate=gate is not None, has_add=add is not None, row_chunk=min(512, k)),
        grid=(n // bn, m // bm),
        in_specs=in_specs,
        out_specs=pl.BlockSpec((bm, bn), lambda j, i: (i, j)),
        out_shape=jax.ShapeDtypeStruct((m, n), out_dtype),
        scratch_shapes=[pltpu.VMEM((k, bn), BF16)],
        compiler_params=_params("parallel", "arbitrary"),
        name="matmul_w",
    )(*args)


def _inproj_kernel(a_ref, wt_ref, wtn_ref, o_ref, wb_ref, *, shifts):
    j = pl.program_id(0)
    i = pl.program_id(1)
    _, k, bn = wb_ref.shape
    kc = k // pl.num_programs(1)

    def fill(shift):
        cs = pl.ds(pl.multiple_of(i * kc, kc), kc)
        if shift == 0:
            blk = wt_ref[:, cs]
        else:
            tall = jnp.concatenate([wt_ref[:, cs], wtn_ref[:, cs]], axis=0)
            blk = tall[shift:shift + bn, :]
        wb_ref[j % 2, cs, :] = blk.T.astype(BF16)

    for j0, j1, shift in shifts:
        pl.when((j >= j0) & (j < j1))(functools.partial(fill, shift))

    @pl.when(j == 0)
    def _():
        o_ref[...] = jnp.zeros(o_ref.shape, o_ref.dtype)

    @pl.when(j > 0)
    def _():
        acc = jnp.dot(a_ref[...], wb_ref[(j + 1) % 2], preferred_element_type=F32)
        o_ref[...] = acc.astype(o_ref.dtype)


def _inproj(a, wt, n_out, shifts):
    m, k = a.shape
    bm = min(MM_BM, m)
    bn = INPROJ_BN
    nj = n_out // bn
    assert n_out % bn == 0 and m % bm == 0 and all(s % SUBLANES == 0 and s <= LANES for _, _, s in shifts)
    assert k % (m // bm) == 0 and (k // (m // bm)) % LANES == 0
    per = bn // LANES
    return pl.pallas_call(
        functools.partial(_inproj_kernel, shifts=shifts),
        grid=(nj + 1, m // bm),
        in_specs=[pl.BlockSpec((bm, k), lambda j, i: (i, 0)),
                  pl.BlockSpec((bn, k), lambda j, i: (jnp.minimum(j, nj - 1), 0)),
                  pl.BlockSpec((LANES, k), lambda j, i: ((jnp.minimum(j, nj - 1) + 1) * per, 0))],
        out_specs=pl.BlockSpec((bm, bn), lambda j, i: (i, jnp.where(j == 0, nj, j - 1))),
        out_shape=jax.ShapeDtypeStruct((m, n_out + bn), BF16),
        scratch_shapes=[pltpu.VMEM((2, k, bn), BF16)],
        compiler_params=_params("arbitrary", "arbitrary"),
        name="inproj",
    )(a, wt, wt)


def _gate_proj_kernel(a_ref, wa_ref, wb_ref, o_ref, *, n_a, off_b, n_b):
    k = wa_ref.shape[1]
    rows = jnp.concatenate([wa_ref[0:n_a, :], wb_ref[off_b:off_b + n_b, :],
                            jnp.zeros((LANES - n_a - n_b, k), F32)], axis=0).astype(BF16)
    o_ref[...] = lax.dot_general(a_ref[...], rows, _NT, preferred_element_type=F32)


def _gate_proj(a, wt, row_a, n_a, row_b, n_b):
    m, k = a.shape
    bm = min(MM_BM, m)
    off_b = row_b % LANES
    assert row_a % LANES == 0 and n_a % SUBLANES == 0 and off_b % SUBLANES == 0 and n_b % SUBLANES == 0
    assert off_b + n_b <= LANES and n_a + n_b <= LANES
    return pl.pallas_call(
        functools.partial(_gate_proj_kernel, n_a=n_a, off_b=off_b, n_b=n_b),
        grid=(m // bm,),
        in_specs=[pl.BlockSpec((bm, k), lambda i: (i, 0)),
                  pl.BlockSpec((LANES, k), lambda i: (row_a // LANES, 0)),
                  pl.BlockSpec((LANES, k), lambda i: (row_b // LANES, 0))],
        out_specs=pl.BlockSpec((bm, LANES), lambda i: (i, 0)),
        out_shape=jax.ShapeDtypeStruct((m, LANES), F32),
        compiler_params=_params("parallel"),
        name="gate_proj",
    )(a, wt, wt)


def _mma_kernel(a_ref, b_ref, o_ref, ab_ref, *, gelu, row_chunk):
    @pl.when(pl.program_id(1) == 0)
    def _():
        for r0 in range(0, ab_ref.shape[0], row_chunk):
            ab_ref[r0:r0 + row_chunk, :] = a_ref[r0:r0 + row_chunk, :].astype(BF16)

    acc = jnp.dot(ab_ref[...], b_ref[...], preferred_element_type=F32)
    if gelu:
        acc = _gelu(acc)
    o_ref[...] = acc.astype(o_ref.dtype)


def _matmul_a(a, b, out_dtype, gelu=False):
    m, k = a.shape
    n = b.shape[1]
    bm = min(MMA_BM, m)
    bn = min(MM_BN, n)
    return pl.pallas_call(
        functools.partial(_mma_kernel, gelu=gelu, row_chunk=min(128, bm)),
        grid=(m // bm, n // bn),
        in_specs=[pl.BlockSpec((bm, k), lambda i, j: (i, 0), pipeline_mode=pl.Buffered(1)),
                  pl.BlockSpec((k, bn), lambda i, j: (0, j))],
        out_specs=pl.BlockSpec((bm, bn), lambda i, j: (i, j)),
        out_shape=jax.ShapeDtypeStruct((m, n), out_dtype),
        scratch_shapes=[pltpu.VMEM((bm, k), BF16)],
        compiler_params=_params("parallel", "arbitrary"),
        name="matmul_a",
    )(a, b)


def _pos_columns(pos, width):
    lane = lax.broadcasted_iota(jnp.int32, (pos.shape[0], width), 1)
    coarse = (pos >> 6) << 6
    vals = jnp.where(lane < 3, coarse, jnp.where(lane < 6, pos & 63, 0))
    return vals.astype(F32).astype(BF16)


def _nsa_kernel(slopes_ref, q_ref, kcm_ref, vcm_ref, ks_ref, vs_ref, kw_ref, vw_ref, gz_ref, wck_ref, wcv_ref,
                pos_ref, o_ref, kc_ref, vc_ref, pext_ref, xs_ref, acc_ref, qext_ref, *, seq, tq, tk,
                heads_per_group):
    R = heads_per_group
    dk = NSA_HEAD_DIM
    g = pl.program_id(1)
    qi = pl.program_id(2)
    t0 = qi * tq
    n_cmp = (seq - CMP_BLOCK) // CMP_STRIDE + 1
    n_slc = seq // SLC_BLOCK
    n_sel = min(SLC_TOPN, n_slc)
    half = CMP_STRIDE * dk
    scale = dk ** -0.5
    c_exp = scale * math.log2(math.e)
    rows = R * tq

    @pl.when(qi == 0)
    def _():
        pext_ref[...] = _pos_columns(lax.broadcasted_iota(jnp.int32, (seq, 1), 0), LANES)
        cpos = lax.broadcasted_iota(jnp.int32, (LANES, 1), 0) * CMP_STRIDE + (CMP_BLOCK - 1)
        for src_ref, w_ref, dst, pi in ((kcm_ref, wck_ref, kc_ref, 0), (vcm_ref, wcv_ref, vc_ref, 1)):
            xs_ref[...] = src_ref[...].astype(F32)
            x = jnp.concatenate([xs_ref[pl.ds(l, LANES, stride=CMP_STRIDE), :].astype(BF16)
                                 for l in range(CMP_STRIDE)], axis=1)
            ya = jnp.dot(x, w_ref[0:half, :], preferred_element_type=F32)
            yb = jnp.dot(x, w_ref[half:2 * half, :], preferred_element_type=F32)
            pos = jnp.broadcast_to(pos_ref[pi], (SUBLANES, 2 * half)).astype(BF16)
            pb = jnp.dot(pos, w_ref[...], preferred_element_type=F32)[0:1, :]
            comp = (ya + pltpu.roll(yb, LANES - 1, 0) + pb).astype(BF16)
            dst[...] = jnp.concatenate([comp, _pos_columns(cpos, LANES)], axis=1) if pi == 0 else comp

        head_of_row = lax.broadcasted_iota(jnp.int32, (rows, LANES), 0) >> _log2(tq)
        lane6 = lax.broadcasted_iota(jnp.int32, (rows, LANES), 1)
        term = jnp.where(lane6 < 3, lane6, lane6 - 3)
        q_ext = jnp.zeros((rows, LANES), F32)
        for r in range(R):
            for k in range(3):
                q_ext = jnp.where((head_of_row == r) & (term == k) & (lane6 < 6), slopes_ref[g, r, k], q_ext)
        qext_ref[...] = q_ext.astype(BF16)

    row_id = lax.broadcasted_iota(jnp.int32, (rows, 1), 0)
    t_col = t0 + (row_id & (tq - 1))
    q = q_ref[...]
    qs = jnp.concatenate([q[:, r * dk:(r + 1) * dk] for r in range(R)], axis=0)
    qa = jnp.concatenate([qs, qext_ref[...]], axis=1)

    n_idx = lax.broadcasted_iota(jnp.int32, (rows, LANES), 1)
    cvalid = (t_col >= n_idx * CMP_STRIDE + CMP_BLOCK - 1) & (n_idx < n_cmp)
    sc = lax.dot_general(qa, kc_ref[...], _NT, preferred_element_type=F32)
    sc = jnp.where(cvalid, sc, NEG_INF)
    mc = jnp.max(sc, axis=-1, keepdims=True)
    p_cmp = jnp.where(cvalid, jnp.exp2((sc - mc) * c_exp), 0.0)
    p_cmp = p_cmp * (1.0 / jnp.maximum(jnp.sum(p_cmp, axis=-1, keepdims=True), TINY))
    o_cmp = jnp.dot(p_cmp.astype(BF16), vc_ref[...], preferred_element_type=F32)

    psum = p_cmp[0:tq]
    for r in range(1, R):
        psum = psum + p_cmp[r * tq:(r + 1) * tq]
    jj = lax.broadcasted_iota(jnp.int32, (LANES, LANES), 0)
    nn = lax.broadcasted_iota(jnp.int32, (LANES, LANES), 1)
    ovl = _ind((nn * CMP_STRIDE < jj * SLC_BLOCK + SLC_BLOCK) & (nn * CMP_STRIDE + CMP_BLOCK > jj * SLC_BLOCK)
               & (jj < n_slc) & (nn < n_cmp), BF16)
    imp_t = jnp.zeros((LANES, tq), F32)
    for part in _split3(psum):
        imp_t = imp_t + lax.dot_general(ovl, part, _NT, preferred_element_type=F32)
    imp_t = imp_t[0:n_slc]
    j_i = lax.broadcasted_iota(jnp.int32, (n_slc, tq), 0)
    t_i = t0 + lax.broadcasted_iota(jnp.int32, (n_slc, tq), 1)
    cur = t_i >> _log2(SLC_BLOCK)
    causal = j_i * SLC_BLOCK <= t_i
    forced = (j_i == 0) | (j_i == cur) | (j_i == cur - 1)
    score = jnp.where(causal, jnp.where(forced, BIG, imp_t), -BIG)
    rank = jnp.zeros((n_slc, tq), jnp.int32)
    for jp in range(n_slc):
        row = score[jp:jp + 1, :]
        beats = (row > score) | ((row == score) & (j_i > jp))
        rank = rank + _ind(beats, jnp.int32)
    sel_t = jnp.where(rank < n_sel, 0.0, NEG_INF)
    sel_t = jnp.concatenate([sel_t, jnp.zeros((LANES - n_slc, tq), F32)], axis=0)
    sel_neg = sel_t.T.astype(BF16)
    tq_col = t0 + lax.broadcasted_iota(jnp.int32, (tq, 1), 0)

    def attend(s, v):
        p = jnp.exp2((s - jnp.max(s, axis=-1, keepdims=True)) * c_exp).astype(BF16)
        ones = _ind(lax.broadcasted_iota(jnp.int32, (v.shape[0], LANES), 1) == 0, BF16)
        pv = jnp.dot(p, jnp.concatenate([v, ones], axis=1), preferred_element_type=F32)
        return pv[:, 0:dk] * (1.0 / jnp.maximum(pv[:, dk:dk + 1], TINY))

    def slc_variant(nk):
        ka = jnp.concatenate([ks_ref[0:nk, :], pext_ref[0:nk, :]], axis=1)
        ej = lax.broadcasted_iota(jnp.int32, (LANES, nk), 0)
        ec = lax.broadcasted_iota(jnp.int32, (LANES, nk), 1)
        expand = _ind(ej == (ec >> _log2(SLC_BLOCK)), BF16)
        madd = jnp.dot(sel_neg, expand, preferred_element_type=F32)
        madd = madd + jnp.where(lax.broadcasted_iota(jnp.int32, (tq, nk), 1) <= tq_col, 0.0, NEG_INF)
        s = lax.dot_general(qa, ka, _NT, preferred_element_type=F32) + jnp.concatenate([madd] * R, axis=0)
        acc_ref[...] = attend(s, vs_ref[0:nk, :])

    variant = (t0 + tq - 1) // tk
    for vi in range(seq // tk):
        pl.when(variant == vi)(functools.partial(slc_variant, (vi + 1) * tk))
    o_slc = acc_ref[...]

    span = WINDOW + tq
    ws = pl.multiple_of(jnp.maximum(t0 - WINDOW, 0), tq)
    kwa = jnp.concatenate([kw_ref[pl.ds(ws, span), :], pext_ref[pl.ds(ws, span), :]], axis=1)
    dw_i = tq_col - (ws + lax.broadcasted_iota(jnp.int32, (tq, span), 1))
    wadd = jnp.where((dw_i >= 0) & (dw_i < WINDOW), 0.0, NEG_INF)
    sw = lax.dot_general(qa, kwa, _NT, preferred_element_type=F32) + jnp.concatenate([wadd] * R, axis=0)
    o_win = attend(sw, vw_ref[pl.ds(ws, span), :])

    gz = gz_ref[...]
    gsel = jnp.zeros((tq, 3 * R), F32)
    for gg in range(NSA_KV_GROUPS):
        gsel = jnp.where(g == gg, gz[:, 3 * R * gg:3 * R * (gg + 1)], gsel)
    gts = jax.nn.sigmoid(gsel)
    for r in range(R):
        sl = slice(r * tq, (r + 1) * tq)
        o = (gts[:, 3 * r:3 * r + 1] * o_cmp[sl] + gts[:, 3 * r + 1:3 * r + 2] * o_slc[sl]
             + gts[:, 3 * r + 2:3 * r + 3] * o_win[sl])
        o_ref[:, r * dk:(r + 1) * dk] = o.astype(o_ref.dtype)


def _nsa(slopes, z, gz, wck, wcv, pos, *, batch, seq, kv_col):
    G = NSA_KV_GROUPS
    R = NSA_HEADS // G
    dk = NSA_HEAD_DIM
    tq = NSA_TQ
    tk = NSA_TK
    nq = seq // tq
    assert seq // CMP_STRIDE == LANES and CMP_BLOCK == 2 * CMP_STRIDE and seq >= WINDOW + tq and seq % tk == 0
    kv0 = kv_col // dk

    def kv_spec(i):
        return pl.BlockSpec((seq, dk), lambda b, g, qi: (b, kv0 + i * G + g))

    w_spec = pl.BlockSpec((CMP_BLOCK * dk, dk), lambda b, g, qi: (0, 0))
    rows = R * tq
    assert dk == LANES and 3 * NSA_HEADS <= LANES
    return pl.pallas_call(
        functools.partial(_nsa_kernel, seq=seq, tq=tq, tk=tk, heads_per_group=R),
        grid=(batch, G, nq),
        in_specs=[
            pl.BlockSpec(memory_space=pltpu.SMEM),
            pl.BlockSpec((tq, R * dk), lambda b, g, qi: (b * nq + qi, g)),
            kv_spec(0), kv_spec(1), kv_spec(2), kv_spec(3), kv_spec(4), kv_spec(5),
            pl.BlockSpec((tq, LANES), lambda b, g, qi: (b * nq + qi, 0)),
            w_spec, w_spec,
            pl.BlockSpec((2, 1, CMP_BLOCK * dk), lambda b, g, qi: (0, 0, 0)),
        ],
        out_specs=pl.BlockSpec((tq, R * dk), lambda b, g, qi: (b * nq + qi, g)),
        out_shape=jax.ShapeDtypeStruct((batch * seq, NSA_HEADS * dk), BF16),
        scratch_shapes=[
            pltpu.VMEM((LANES, 2 * dk), BF16), pltpu.VMEM((LANES, dk), BF16),
            pltpu.VMEM((seq, LANES), BF16), pltpu.VMEM((seq, dk), F32), pltpu.VMEM((rows, dk), F32),
            pltpu.VMEM((rows, LANES), BF16),
        ],
        compiler_params=_params("parallel", "parallel", "arbitrary"),
        name="nsa",
    )(slopes, z, z, z, z, z, z, z, gz, wck, wcv, pos)


def _log_sigmoid(x):
    return jnp.minimum(x, 0.0) - jnp.log1p(jnp.exp(-jnp.abs(x)))


def _mlstm_kernel(bias_ref, q_ref, k_ref, v_ref, og_ref, gz_ref, nw_ref, o_ref, c_ref, *, seq, L, gate_lane, hps):
    dqk = ML_QK_DIM
    dv = ML_V_DIM
    nc = seq // L
    lane = lax.broadcasted_iota(jnp.int32, (L, LANES), 1)

    def gate_forms(blk, which, bias):
        col = jnp.sum(jnp.where(lane == which, blk, 0.0), axis=1, keepdims=True) + bias
        return col, jnp.broadcast_to(col, (L, LANES)).T[0:1, :]

    ext = dv + LANES
    ii = lax.broadcasted_iota(jnp.int32, (L, L), 0)
    kk = lax.broadcasted_iota(jnp.int32, (L, L), 1)
    causal = kk <= ii
    tri = _ind(causal, BF16)
    tri_t = _ind(ii <= kk, BF16)
    ones_slab = _ind(lax.broadcasted_iota(jnp.int32, (L, LANES), 1) == 0, BF16)
    c_ref[...] = jnp.zeros((hps, dqk, ext), F32)

    def cumsum_mats(lf_row, lf_col):
        lr = jnp.broadcast_to(lf_row, (L, L))
        lc = jnp.broadcast_to(lf_col, (L, L))
        b_row = jnp.zeros((L, L), F32)
        b_col = jnp.zeros((L, L), F32)
        for part in _split3(lr)[:2]:
            b_row = b_row + jnp.dot(part, tri_t, preferred_element_type=F32)
        for part in _split3(lc)[:2]:
            b_col = b_col + jnp.dot(tri, part, preferred_element_type=F32)
        return b_row, b_col

    def head_chunk(hh, r0, gblk, m):
        h_id = pl.program_id(1) * hps + hh
        i_lane = gate_lane + h_id
        f_lane = i_lane + ML_HEADS
        qc = q_ref[pl.ds(r0, L), hh * dqk:(hh + 1) * dqk]
        kc = k_ref[pl.ds(r0, L), hh * dqk:(hh + 1) * dqk] * (dqk ** -0.5)
        v_ext = jnp.concatenate([v_ref[pl.ds(r0, L), hh * dv:(hh + 1) * dv], ones_slab], axis=1)
        i_col, i_row = gate_forms(gblk, i_lane, bias_ref[0, h_id])
        f_col, f_row = gate_forms(gblk, f_lane, bias_ref[1, h_id])
        lf_col = _log_sigmoid(f_col)
        lf_row = _log_sigmoid(f_row)
        b_row, b_col = cumsum_mats(lf_row, lf_col)
        log_d = jnp.where(causal, b_col - b_row + i_row, NEG_INF)
        b_c = b_col[:, 0:1]
        m_inter = b_c + m
        m_t = jnp.maximum(m_inter, jnp.max(log_d, axis=-1, keepdims=True))
        d = jnp.exp(log_d - m_t)
        s = lax.dot_general(qc, kc, _NT, preferred_element_type=F32) * d
        w_inter = jnp.exp(m_inter - m_t)
        inter = jnp.dot(qc, c_ref[hh].astype(BF16), preferred_element_type=F32)
        intra = jnp.dot(s.astype(BF16), v_ext, preferred_element_type=F32)
        numden = w_inter * inter + intra
        den = numden[:, dv:dv + 1]
        h = numden[:, 0:dv] / jnp.maximum(jnp.abs(den), jnp.exp(-m_t))
        g_tot = b_row[0:1, L - 1:L]
        a_row = g_tot - b_row[0:1, :] + i_row
        a_col = g_tot - b_c + i_col
        m_new = jnp.maximum(g_tot + m, jnp.max(a_row, axis=-1, keepdims=True))
        decay = jnp.exp(g_tot + m - m_new)
        w_tok = jnp.exp(a_col - m_new)
        kw_t = (kc.astype(F32) * w_tok).T.astype(BF16)
        c_ref[hh] = decay * c_ref[hh] + jnp.dot(kw_t, v_ext, preferred_element_type=F32)
        hn = _rms(h, nw_ref[hh]) * jax.nn.sigmoid(og_ref[pl.ds(r0, L), hh * dv:(hh + 1) * dv].astype(F32))
        o_ref[pl.ds(r0, L), hh * dv:(hh + 1) * dv] = hn.astype(o_ref.dtype)
        return m_new

    def body(c, ms):
        r0 = pl.multiple_of(c * L, L)
        gblk = gz_ref[pl.ds(r0, L), :]
        return tuple(head_chunk(hh, r0, gblk, ms[hh]) for hh in range(hps))

    lax.fori_loop(0, nc, body, tuple(jnp.zeros((1, 1), F32) for _ in range(hps)))


def _mlstm(gate_bias, z, gz, norm_w, *, batch, seq, q_col, k_col, v_col, o_col, gate_lane):
    H = ML_HEADS
    dqk = ML_QK_DIM
    dv = ML_V_DIM
    L = min(ML_CHUNK_LEN, seq)
    hps = ML_HEADS_PER_STEP
    qw, vw = hps * dqk, hps * dv
    assert L % LANES == 0 and seq % L == 0 and gate_lane + 2 * H <= LANES and H % hps == 0
    assert q_col % qw == 0 and k_col % qw == 0 and v_col % vw == 0 and o_col % vw == 0
    return pl.pallas_call(
        functools.partial(_mlstm_kernel, seq=seq, L=L, gate_lane=gate_lane, hps=hps),
        grid=(batch, H // hps),
        in_specs=[
            pl.BlockSpec(memory_space=pltpu.SMEM),
            pl.BlockSpec((seq, qw), lambda b, h: (b, q_col // qw + h)),
            pl.BlockSpec((seq, qw), lambda b, h: (b, k_col // qw + h)),
            pl.BlockSpec((seq, vw), lambda b, h: (b, v_col // vw + h)),
            pl.BlockSpec((seq, vw), lambda b, h: (b, o_col // vw + h)),
            pl.BlockSpec((seq, LANES), lambda b, h: (b, 0)),
            pl.BlockSpec((hps, 1, dv), lambda b, h: (h, 0, 0)),
        ],
        out_specs=pl.BlockSpec((seq, vw), lambda b, h: (b, h)),
        out_shape=jax.ShapeDtypeStruct((batch * seq, H * dv), BF16),
        scratch_shapes=[pltpu.VMEM((hps, dqk, dv + LANES), F32)],
        compiler_params=_params("parallel", "parallel"),
        name="mlstm",
    )(gate_bias, z, z, z, z, gz, norm_w.reshape(H, 1, dv))


def _peer_pairs():
    return [(a, (PEER_TOPK + 1) // (a + 1)) for a in range(PEER_TOPK + 1)]


def _topk_multiset(s, k, with_rank=False):
    tt = s.shape[1]
    riota = lax.broadcasted_iota(jnp.int32, (k, tt), 0).astype(F32)
    v = jnp.full((k, tt), -jnp.inf, F32)
    taken = jnp.zeros((1, tt), F32)
    rank = jnp.full(s.shape, float(s.shape[0]), F32)
    rem = s
    for _ in range(k):
        m = jnp.max(rem, axis=0, keepdims=True)
        eq = rem == m
        cnt = jnp.sum(_ind(eq, F32), axis=0, keepdims=True)
        v = jnp.where((riota >= taken) & (riota < taken + cnt), m, v)
        if with_rank:
            rank = jnp.where(eq, taken, rank)
        taken = taken + cnt
        rem = jnp.where(eq, -jnp.inf, rem)
    return (v, rank) if with_rank else v


def _peer_topk_kernel(q_ref, khi_ref, klo_ref, cnt_ref, rank_ref, e1_ref, e2_ref, cand_ref):
    H = PEER_HEADS
    half = PEER_KEY_DIM // 2
    K = PEER_TOPK
    pairs = _peer_pairs()
    n_cand = sum(nb for _, nb in pairs)
    tt = q_ref.shape[0]
    for h in range(H):
        s_parts = []
        for side in range(2):
            qf = q_ref[:, (2 * h + side) * half:(2 * h + side + 1) * half]
            q_hi = qf.astype(BF16)
            q_lo = (qf - q_hi.astype(F32)).astype(BF16)
            k_hi = khi_ref[side, h]
            k_lo = klo_ref[side, h]
            s = (lax.dot_general(k_hi, q_hi, _NT, preferred_element_type=F32)
                 + lax.dot_general(k_hi, q_lo, _NT, preferred_element_type=F32)
                 + lax.dot_general(k_lo, q_hi, _NT, preferred_element_type=F32))
            s_parts.append(s)
        s1, s2 = s_parts
        v1 = _topk_multiset(s1, K + 1)
        v2, rank2 = _topk_multiset(s2, K + 1, with_rank=True)
        off = 0
        for a, nb in pairs:
            cand_ref[off:off + nb, :] = v1[a:a + 1, :] + v2[0:nb, :]
            off += nb
        cand_ref[n_cand:, :] = jnp.full((cand_ref.shape[0] - n_cand, tt), -jnp.inf, F32)
        rem = cand_ref[...]
        taken = jnp.zeros((1, tt), F32)
        tau16 = jnp.zeros((1, tt), F32)
        tau17 = jnp.zeros((1, tt), F32)
        zsum = jnp.zeros((1, tt), F32)
        cmax = v1[0:1, :] + v2[0:1, :]
        for _ in range(K + 1):
            m = jnp.max(rem, axis=0, keepdims=True)
            eq = rem == m
            cnt = jnp.sum(_ind(eq, F32), axis=0, keepdims=True)
            take = jnp.minimum(cnt, jnp.maximum(K - taken, 0.0))
            zsum = zsum + take * jnp.exp(m - cmax)
            after = taken + cnt
            tau16 = jnp.where((taken < K) & (after >= K), m, tau16)
            tau17 = jnp.where((taken <= K) & (after > K), m, tau17)
            taken = after
            rem = jnp.where(eq, -jnp.inf, rem)
        theta = 0.5 * (tau16 + tau17)
        r1 = theta - s1
        cnt1 = jnp.zeros(r1.shape, F32)
        for b in range(K + 1):
            cnt1 = cnt1 + _ind(v2[b:b + 1, :] >= r1, F32)
        cnt_ref[h] = cnt1
        rank_ref[h] = rank2.astype(rank_ref.dtype)
        e1_ref[h] = jnp.exp(s1 - v1[0:1, :])
        e2_ref[h] = (jnp.exp(s2 - v2[0:1, :]) / zsum).astype(e2_ref.dtype)


def _peer_topk(pq, k_hi, k_lo):
    n = pq.shape[0]
    H = PEER_HEADS
    nk = PEER_N_KEYS
    tt = min(PEER_TOPK_TT, n)
    n_cand = sum(nb for _, nb in _peer_pairs())
    cand_rows = -(-n_cand // SUBLANES) * SUBLANES
    key_spec = pl.BlockSpec((2, H, nk, PEER_KEY_DIM // 2), lambda i: (0, 0, 0, 0))
    hkt = pl.BlockSpec((H, nk, tt), lambda i: (0, 0, i))
    wide = jax.ShapeDtypeStruct((H, nk, n), F32)
    narrow = jax.ShapeDtypeStruct((H, nk, n), BF16)
    return pl.pallas_call(
        _peer_topk_kernel,
        grid=(n // tt,),
        in_specs=[pl.BlockSpec((tt, pq.shape[1]), lambda i: (i, 0)), key_spec, key_spec],
        out_specs=[hkt, hkt, hkt, hkt],
        out_shape=[wide, narrow, wide, narrow],
        scratch_shapes=[pltpu.VMEM((cand_rows, tt), F32)],
        compiler_params=_params("parallel"),
        name="peer_topk",
    )(pq, k_hi, k_lo)


def _bcast_packed(row):
    half = jnp.broadcast_to(row, (SUBLANES, LANES))
    return jnp.concatenate([half, half], axis=0).astype(BF16)


def _peer_mix_kernel(act_ref, vt_ref, cnt_ref, e1_ref, rank_ref, e2_ref, o_ref, acc_ref, a_ref, *, chunk):
    H = PEER_HEADS
    nk = PEER_N_KEYS
    te, tt = a_ref.shape
    e = pl.program_id(1)
    pack = 2 * SUBLANES

    @pl.when(e == 0)
    def _():
        acc_ref[...] = jnp.zeros(acc_ref.shape, F32)

    for i in range(te // nk):
        cnt_rows = [cnt_ref[h, i:i + 1, :] for h in range(H)]
        e1_rows = [e1_ref[h, i:i + 1, :] for h in range(H)]
        for lt in range(tt // LANES):
            ls = slice(lt * LANES, (lt + 1) * LANES)
            cntb = [_bcast_packed(cnt_rows[h][:, ls]) for h in range(H)]
            e1b = [_bcast_packed(e1_rows[h][:, ls]) for h in range(H)]
            for sg in range(nk // pack):
                rs = slice(sg * pack, (sg + 1) * pack)
                w = jnp.zeros((pack, LANES), BF16)
                for h in range(H):
                    hit = rank_ref[h, rs, ls] < cntb[h]
                    w = w + e1b[h] * jnp.where(hit, e2_ref[h, rs, ls], jnp.zeros((), BF16))
                rr = slice(i * nk + sg * pack, i * nk + (sg + 1) * pack)
                a_ref[rr, ls] = act_ref[rr, ls] * w
        done = (i + 1) * nk
        if done % chunk == 0:
            ks = slice(done - chunk, done)
            acc_ref[...] += jnp.dot(vt_ref[:, ks], a_ref[ks, :], preferred_element_type=F32)

    @pl.when(e == pl.num_programs(1) - 1)
    def _():
        o_ref[...] = acc_ref[...].T.astype(o_ref.dtype)


def _peer_mix(act_t, pvt, cnt1, rank2, e1, e2):
    n_exp, n = act_t.shape
    d = pvt.shape[0]
    H = PEER_HEADS
    nk = PEER_N_KEYS
    tt = min(PEER_TT, n)
    te = PEER_TE
    assert te == SUBLANES * nk and n_exp % te == 0

    def i1_spec():
        return pl.BlockSpec((H, SUBLANES, tt), lambda i, e: (0, e, i))

    def i2_spec():
        return pl.BlockSpec((H, nk, tt), lambda i, e: (0, 0, i))

    return pl.pallas_call(
        functools.partial(_peer_mix_kernel, chunk=PEER_DOT_CHUNK),
        grid=(n // tt, n_exp // te),
        in_specs=[
            pl.BlockSpec((te, tt), lambda i, e: (e, i)),
            pl.BlockSpec((d, te), lambda i, e: (0, e)),
            i1_spec(), i1_spec(), i2_spec(), i2_spec(),
        ],
        out_specs=pl.BlockSpec((tt, d), lambda i, e: (i, 0)),
        out_shape=jax.ShapeDtypeStruct((n, d), BF16),
        scratch_shapes=[pltpu.VMEM((d, tt), F32), pltpu.VMEM((te, tt), BF16)],
        compiler_params=_params("parallel", "arbitrary"),
        name="peer_mix",
    )(act_t, pvt, cnt1, e1, rank2, e2)


def _cast_transpose_kernel(x_ref, o_ref):
    o_ref[...] = x_ref[...].T.astype(o_ref.dtype)


def _cast_transpose(x, dtype):
    r, c = x.shape
    tr = min(CAST_T_ROWS, r)
    return pl.pallas_call(
        _cast_transpose_kernel,
        grid=(r // tr,),
        in_specs=[pl.BlockSpec((tr, c), lambda i: (i, 0))],
        out_specs=pl.BlockSpec((c, tr), lambda i: (0, i)),
        out_shape=jax.ShapeDtypeStruct((c, r), dtype),
        compiler_params=_params("parallel"),
        name="cast_transpose",
    )(x)


def _layer(h, mod3, norm_pre_mix, norm_post_mix, norm_pre_ffn, norm_post_ffn, w_in, cmp_pos, w_cmp_k, w_cmp_v,
           gate_bias, ml_norm_w, w_up_nsa, w_up_mlstm, w_out, peer_w_q, sub_keys, peer_u, peer_v, *, batch, seq):
    d = h.shape[1]
    G = NSA_KV_GROUPS
    R = NSA_HEADS // G
    dk = NSA_HEAD_DIM
    H = ML_HEADS
    nsa_w = NSA_HEADS * dk
    kv_w = 6 * G * dk
    mq_w = H * ML_QK_DIM
    mv_w = H * ML_V_DIM
    c_kv = nsa_w
    c_gn = c_kv + kv_w
    c_mq = c_gn + 3 * NSA_HEADS
    c_mk = c_mq + mq_w
    c_mv = c_mk + mq_w
    c_mo = c_mv + mv_w
    c_if = c_mo + mv_w
    c_mg = c_if + 2 * H
    n_gate = 3 * NSA_HEADS + 2 * H
    z_kv = nsa_w
    z_mq = z_kv + kv_w
    z_mk = z_mq + mq_w
    z_mv = z_mk + mq_w
    z_mo = z_mv + mv_w
    z_mg = z_mo + mv_w
    z_w = z_mg + 2 * d
    assert z_mq % INPROJ_BN == 0 and z_mg % INPROJ_BN == 0 and z_w % INPROJ_BN == 0
    z_shifts = [(0, z_mq // INPROJ_BN, 0), (z_mq // INPROJ_BN, z_mg // INPROJ_BN, 3 * NSA_HEADS),
                (z_mg // INPROJ_BN, z_w // INPROJ_BN, n_gate)]

    u = _prenorm(h, norm_pre_mix, mod3, seq)
    w_in_t = w_in.T
    z = _inproj(u, w_in_t, z_w, z_shifts)
    gz = _gate_proj(u, w_in_t, c_gn, 3 * NSA_HEADS, c_if, 2 * H)

    slopes = jnp.exp2(-8.0 * jnp.arange(1, NSA_HEADS + 1, dtype=F32) / NSA_HEADS) / (dk ** -0.5)
    slopes = jnp.stack([t.astype(F32) for t in _split3(slopes)], axis=-1).reshape(G, R, 3)
    o_nsa = _nsa(slopes, z, gz, w_cmp_k.astype(BF16), w_cmp_v.astype(BF16),
                 cmp_pos.reshape(2, 1, CMP_BLOCK * dk), batch=batch, seq=seq, kv_col=z_kv)

    h_ml = _mlstm(gate_bias, z, gz, ml_norm_w, batch=batch, seq=seq, q_col=z_mq, k_col=z_mk, v_col=z_mv,
                  o_col=z_mo, gate_lane=3 * NSA_HEADS)

    t1 = _matmul_w(o_nsa, w_up_nsa, BF16, gate=z, gate_col=z_mg)
    mix = _matmul_w(h_ml, w_up_mlstm, BF16, gate=z, gate_col=z_mg + d, add=t1)
    y = _matmul_w(mix, w_out, BF16)
    h1, u2, u2_t = _midnorm(h, y, norm_post_mix, norm_pre_ffn, mod3, seq)

    pq = _matmul_w(u2, peer_w_q, F32)
    k_hi = sub_keys.astype(BF16)
    k_lo = (sub_keys - k_hi.astype(F32)).astype(BF16)
    cnt1, rank2, e1, e2 = _peer_topk(pq, k_hi, k_lo)
    act_t = _matmul_a(peer_u, u2_t, BF16, gelu=True)
    y2 = _peer_mix(act_t, _cast_transpose(peer_v, BF16), cnt1, rank2, e1, e2)
    return _finalnorm(h1, y2, norm_post_ffn, mod3, seq)


def kernel(x, c, w_ada, b_ada, norm_pre_mix, norm_post_mix, norm_pre_ffn, norm_post_ffn, w_in, nsa_cmp_pos, nsa_w_cmp_k, nsa_w_cmp_v, mlstm_gate_bias, mlstm_norm_w, w_up_nsa, w_up_mlstm, w_out, peer_w_q, peer_sub_keys, peer_u, peer_v):
    batch, seq, d = x.shape
    depth = w_ada.shape[0]
    h = x.reshape(batch * seq, d)
    c_pad = jnp.concatenate([c, jnp.zeros((SUBLANES - batch % SUBLANES, d), c.dtype)], axis=0)
    for l in range(depth):
        mod = _adaln(c_pad, w_ada[l], b_ada[l])[:batch]
        mod3 = mod.reshape(batch, 6, d)
        h = _layer(h, mod3, norm_pre_mix[l], norm_post_mix[l], norm_pre_ffn[l], norm_post_ffn[l], w_in[l],
                   nsa_cmp_pos[l], nsa_w_cmp_k[l], nsa_w_cmp_v[l], mlstm_gate_bias[l], mlstm_norm_w[l],
                   w_up_nsa[l], w_up_mlstm[l], w_out[l], peer_w_q[l], peer_sub_keys[l], peer_u[l], peer_v[l],
                   batch=batch, seq=seq)
    return h.reshape(batch, seq, d).astype(x.dtype)
```

```python
import functools
import math

import jax
import jax.numpy as jnp
from jax import lax
from jax.experimental import pallas as pl
from jax.experimental.pallas import tpu as pltpu

D_MODEL = 4096
BATCH = 4
SEQ = 2048
NSA_HEADS = 16
NSA_KV_GROUPS = 4
NSA_HEAD_DIM = 128
CMP_BLOCK = 32
CMP_STRIDE = 16
SLC_BLOCK = 64
SLC_TOPN = 16
WINDOW = 512
ML_HEADS = 8
ML_QK_DIM = 256
ML_V_DIM = 512
PEER_HEADS = 8
PEER_KEY_DIM = 256
PEER_N_KEYS = 128
PEER_TOPK = 16

NEG_INF = -1e30
BIG = 1e9
TINY = 1e-30
EPS = 1e-6

F32 = jnp.float32
BF16 = jnp.bfloat16

V7X_VMEM_LIMIT_BYTES = 56 * 1024 * 1024
LANES = 128
SUBLANES = 8

ROW_TILE = 256
MM_BM = 1024
MM_BN = 1024
MMW_BN = 512
INPROJ_BN = 1024
MMW_VMEM_BUDGET_BYTES = 50 * 1024 * 1024
MMA_BM = 1024
CAST_T_ROWS = 512
ADA_BN = 512
NSA_TQ = 256
NSA_TK = 512
ML_CHUNK_LEN = 256
ML_HEADS_PER_STEP = 2
PEER_TOPK_TT = 256
PEER_TT = 512
PEER_TE = 1024
PEER_DOT_CHUNK = 256

_NT = (((1,), (1,)), ((), ()))


def _params(*sem):
    return pltpu.CompilerParams(dimension_semantics=sem, vmem_limit_bytes=V7X_VMEM_LIMIT_BYTES)


def _ind(cond, dtype):
    wide = jnp.int32 if jnp.issubdtype(dtype, jnp.integer) else F32
    return jnp.where(cond, jnp.ones((), wide), jnp.zeros((), wide)).astype(dtype)


def _log2(n):
    k = int(math.log2(n))
    assert 1 << k == n
    return k


def _split3(x):
    hi = x.astype(BF16)
    r1 = x - hi.astype(F32)
    mid = r1.astype(BF16)
    lo = (r1 - mid.astype(F32)).astype(BF16)
    return hi, mid, lo


def _adaln_kernel(c_ref, w_ref, b_ref, o_ref):
    c = c_ref[...]
    cond = (c * jax.nn.sigmoid(c)).astype(BF16)
    o_ref[...] = jnp.dot(cond, w_ref[...].astype(BF16), preferred_element_type=F32) + b_ref[...]


def _adaln(c_pad, w_ada, b_ada):
    rows, d = c_pad.shape
    n = w_ada.shape[1]
    bn = min(ADA_BN, n)
    return pl.pallas_call(
        _adaln_kernel,
        grid=(n // bn,),
        in_specs=[
            pl.BlockSpec((rows, d), lambda j: (0, 0)),
            pl.BlockSpec((d, bn), lambda j: (0, j)),
            pl.BlockSpec((1, bn), lambda j: (0, j)),
        ],
        out_specs=pl.BlockSpec((rows, bn), lambda j: (0, j)),
        out_shape=jax.ShapeDtypeStruct((rows, n), F32),
        compiler_params=_params("parallel"),
        name="adaln",
    )(c_pad, w_ada, b_ada.reshape(1, n))


def _rms(x, w):
    return x * lax.rsqrt(jnp.mean(x * x, axis=-1, keepdims=True) + EPS) * w


def _prenorm_kernel(x_ref, w_ref, mod_ref, u_ref):
    y = _rms(x_ref[...], w_ref[...])
    u_ref[...] = (y * (1.0 + mod_ref[1:2, :]) + mod_ref[0:1, :]).astype(u_ref.dtype)


def _prenorm(x2, w, mod3, seq):
    n, d = x2.shape
    tr = min(ROW_TILE, seq)
    return pl.pallas_call(
        _prenorm_kernel,
        grid=(n // tr,),
        in_specs=[
            pl.BlockSpec((tr, d), lambda i: (i, 0)),
            pl.BlockSpec((1, d), lambda i: (0, 0)),
            pl.BlockSpec((None, 6, d), lambda i: ((i * tr) // seq, 0, 0)),
        ],
        out_specs=pl.BlockSpec((tr, d), lambda i: (i, 0)),
        out_shape=jax.ShapeDtypeStruct((n, d), BF16),
        compiler_params=_params("parallel"),
        name="prenorm",
    )(x2, w.reshape(1, d), mod3)


def _midnorm_kernel(x_ref, y_ref, w1_ref, w2_ref, mod_ref, h_ref, u_ref, ut_ref):
    h = x_ref[...] + mod_ref[2:3, :] * _rms(y_ref[...].astype(F32), w1_ref[...])
    h_ref[...] = h
    u = _rms(h, w2_ref[...]) * (1.0 + mod_ref[4:5, :]) + mod_ref[3:4, :]
    u_ref[...] = u.astype(u_ref.dtype)
    ut_ref[...] = u.T.astype(ut_ref.dtype)


def _midnorm(x2, y, w_post, w_pre, mod3, seq):
    n, d = x2.shape
    tr = min(ROW_TILE, seq)
    row = pl.BlockSpec((tr, d), lambda i: (i, 0))
    vec = pl.BlockSpec((1, d), lambda i: (0, 0))
    return pl.pallas_call(
        _midnorm_kernel,
        grid=(n // tr,),
        in_specs=[row, row, vec, vec, pl.BlockSpec((None, 6, d), lambda i: ((i * tr) // seq, 0, 0))],
        out_specs=[row, row, pl.BlockSpec((d, tr), lambda i: (0, i))],
        out_shape=[jax.ShapeDtypeStruct((n, d), F32), jax.ShapeDtypeStruct((n, d), BF16),
                   jax.ShapeDtypeStruct((d, n), BF16)],
        compiler_params=_params("parallel"),
        name="midnorm",
    )(x2, y, w_post.reshape(1, d), w_pre.reshape(1, d), mod3)


def _finalnorm_kernel(h_ref, y_ref, w_ref, mod_ref, o_ref):
    o_ref[...] = h_ref[...] + mod_ref[5:6, :] * _rms(y_ref[...].astype(F32), w_ref[...])


def _finalnorm(h1, y, w, mod3, seq):
    n, d = h1.shape
    tr = min(ROW_TILE, seq)
    row = pl.BlockSpec((tr, d), lambda i: (i, 0))
    return pl.pallas_call(
        _finalnorm_kernel,
        grid=(n // tr,),
        in_specs=[row, row, pl.BlockSpec((1, d), lambda i: (0, 0)),
                  pl.BlockSpec((None, 6, d), lambda i: ((i * tr) // seq, 0, 0))],
        out_specs=row,
        out_shape=jax.ShapeDtypeStruct((n, d), F32),
        compiler_params=_params("parallel"),
        name="finalnorm",
    )(h1, y, w.reshape(1, d), mod3)


def _gelu(x):
    return 0.5 * x * (1.0 + lax.erf(x * (2.0 ** -0.5)))


def _mmw_kernel(*refs, has_gate, has_add, row_chunk):
    a_ref, w_ref = refs[0], refs[1]
    nxt = 2
    o_ref, wb_ref = refs[-2], refs[-1]

    @pl.when(pl.program_id(1) == 0)
    def _():
        for r0 in range(0, wb_ref.shape[0], row_chunk):
            wb_ref[r0:r0 + row_chunk, :] = w_ref[r0:r0 + row_chunk, :].astype(BF16)

    acc = jnp.dot(a_ref[...], wb_ref[...], preferred_element_type=F32)
    if has_gate:
        acc = acc * jax.nn.sigmoid(refs[nxt][...].astype(F32))
        nxt += 1
    if has_add:
        acc = acc + refs[nxt][...].astype(F32)
    o_ref[...] = acc.astype(o_ref.dtype)


def _matmul_w(a, w, out_dtype, *, gate=None, gate_col=0, add=None):
    m, k = a.shape
    n = w.shape[1]
    bm = min(MM_BM, m)
    n_side = (gate is not None) + (add is not None)
    out_bytes = jnp.dtype(out_dtype).itemsize

    def vmem_bytes(bn, w_bufs):
        return (w_bufs * k * bn * 4 + k * bn * 2 + 2 * bm * k * 2 + 2 * bm * bn * (out_bytes + 2 * n_side)
                + bm * bn * 4)

    if n % MM_BN == 0 and vmem_bytes(MM_BN, 1) <= MMW_VMEM_BUDGET_BYTES:
        bn, w_mode = MM_BN, pl.Buffered(1)
    else:
        bn, w_mode = min(MMW_BN, n), None
    assert n % bn == 0 and m % bm == 0
    in_specs = [pl.BlockSpec((bm, k), lambda j, i: (i, 0)),
                pl.BlockSpec((k, bn), lambda j, i: (0, j), pipeline_mode=w_mode)]
    args = [a, w]
    if gate is not None:
        goff = gate_col // bn
        assert goff * bn == gate_col
        in_specs.append(pl.BlockSpec((bm, bn), lambda j, i: (i, goff + j)))
        args.append(gate)
    if add is not None:
        in_specs.append(pl.BlockSpec((bm, bn), lambda j, i: (i, j)))
        args.append(add)
    return pl.pallas_call(
        functools.partial(_mmw_kernel, has_gate=gate is not None, has_add=add is not None, row_chunk=min(512, k)),
        grid=(n // bn, m // bm),
        in_specs=in_specs,
        out_specs=pl.BlockSpec((bm, bn), lambda j, i: (i, j)),
        out_shape=jax.ShapeDtypeStruct((m, n), out_dtype),
        scratch_shapes=[pltpu.VMEM((k, bn), BF16)],
        compiler_params=_params("parallel", "arbitrary"),
        name="matmul_w",
    )(*args)


def _inproj_kernel(a_ref, wt_ref, wtn_ref, o_ref, wb_ref, *, shifts, col_chunk):
    j = pl.program_id(0)
    k, bn = wb_ref.shape

    def fill(shift):
        for c0 in range(0, k, col_chunk):
            cs = slice(c0, c0 + col_chunk)
            if shift == 0:
                blk = wt_ref[:, cs]
            else:
                tall = jnp.concatenate([wt_ref[:, cs], wtn_ref[:, cs]], axis=0)
                blk = tall[shift:shift + bn, :]
            wb_ref[cs, :] = blk.T.astype(BF16)

    @pl.when(pl.program_id(1) == 0)
    def _():
        for j0, j1, shift in shifts:
            pl.when((j >= j0) & (j < j1))(functools.partial(fill, shift))

    acc = jnp.dot(a_ref[...], wb_ref[...], preferred_element_type=F32)
    o_ref[...] = acc.astype(o_ref.dtype)


def _inproj(a, wt, n_out, shifts):
    m, k = a.shape
    bm = min(MM_BM, m)
    bn = INPROJ_BN
    assert n_out % bn == 0 and m % bm == 0 and all(s % SUBLANES == 0 and s <= LANES for _, _, s in shifts)
    per = bn // LANES
    once = pl.Buffered(1)
    return pl.pallas_call(
        functools.partial(_inproj_kernel, shifts=shifts, col_chunk=min(512, k)),
        grid=(n_out // bn, m // bm),
        in_specs=[pl.BlockSpec((bm, k), lambda j, i: (i, 0)),
                  pl.BlockSpec((bn, k), lambda j, i: (j, 0), pipeline_mode=once),
                  pl.BlockSpec((LANES, k), lambda j, i: ((j + 1) * per, 0), pipeline_mode=once)],
        out_specs=pl.BlockSpec((bm, bn), lambda j, i: (i, j)),
        out_shape=jax.ShapeDtypeStruct((m, n_out), BF16),
        scratch_shapes=[pltpu.VMEM((k, bn), BF16)],
        compiler_params=_params("parallel", "arbitrary"),
        name="inproj",
    )(a, wt, wt)


def _gate_proj_kernel(a_ref, wa_ref, wb_ref, o_ref, *, n_a, off_b, n_b):
    k = wa_ref.shape[1]
    rows = jnp.concatenate([wa_ref[0:n_a, :], wb_ref[off_b:off_b + n_b, :],
                            jnp.zeros((LANES - n_a - n_b, k), F32)], axis=0).astype(BF16)
    o_ref[...] = lax.dot_general(a_ref[...], rows, _NT, preferred_element_type=F32)


def _gate_proj(a, wt, row_a, n_a, row_b, n_b):
    m, k = a.shape
    bm = min(MM_BM, m)
    off_b = row_b % LANES
    assert row_a % LANES == 0 and n_a % SUBLANES == 0 and off_b % SUBLANES == 0 and n_b % SUBLANES == 0
    assert off_b + n_b <= LANES and n_a + n_b <= LANES
    return pl.pallas_call(
        functools.partial(_gate_proj_kernel, n_a=n_a, off_b=off_b, n_b=n_b),
        grid=(m // bm,),
        in_specs=[pl.BlockSpec((bm, k), lambda i: (i, 0)),
                  pl.BlockSpec((LANES, k), lambda i: (row_a // LANES, 0)),
                  pl.BlockSpec((LANES, k), lambda i: (row_b // LANES, 0))],
        out_specs=pl.BlockSpec((bm, LANES), lambda i: (i, 0)),
        out_shape=jax.ShapeDtypeStruct((m, LANES), F32),
        compiler_params=_params("parallel"),
        name="gate_proj",
    )(a, wt, wt)


def _mma_kernel(a_ref, b_ref, o_ref, ab_ref, *, gelu, row_chunk):
    @pl.when(pl.program_id(1) == 0)
    def _():
        for r0 in range(0, ab_ref.shape[0], row_chunk):
            ab_ref[r0:r0 + row_chunk, :] = a_ref[r0:r0 + row_chunk, :].astype(BF16)

    acc = jnp.dot(ab_ref[...], b_ref[...], preferred_element_type=F32)
    if gelu:
        acc = _gelu(acc)
    o_ref[...] = acc.astype(o_ref.dtype)


def _matmul_a(a, b, out_dtype, gelu=False):
    m, k = a.shape
    n = b.shape[1]
    bm = min(MMA_BM, m)
    bn = min(MM_BN, n)
    return pl.pallas_call(
        functools.partial(_mma_kernel, gelu=gelu, row_chunk=min(128, bm)),
        grid=(m // bm, n // bn),
        in_specs=[pl.BlockSpec((bm, k), lambda i, j: (i, 0), pipeline_mode=pl.Buffered(1)),
                  pl.BlockSpec((k, bn), lambda i, j: (0, j))],
        out_specs=pl.BlockSpec((bm, bn), lambda i, j: (i, j)),
        out_shape=jax.ShapeDtypeStruct((m, n), out_dtype),
        scratch_shapes=[pltpu.VMEM((bm, k), BF16)],
        compiler_params=_params("parallel", "arbitrary"),
        name="matmul_a",
    )(a, b)


def _pos_columns(pos, width):
    lane = lax.broadcasted_iota(jnp.int32, (pos.shape[0], width), 1)
    coarse = (pos >> 6) << 6
    vals = jnp.where(lane < 3, coarse, jnp.where(lane < 6, pos & 63, 0))
    return vals.astype(F32).astype(BF16)


def _nsa_kernel(slopes_ref, q_ref, kcm_ref, vcm_ref, ks_ref, vs_ref, kw_ref, vw_ref, gz_ref, wck_ref, wcv_ref,
                pos_ref, o_ref, kc_ref, vc_ref, pext_ref, xs_ref, acc_ref, qext_ref, *, seq, tq, tk,
                heads_per_group):
    R = heads_per_group
    dk = NSA_HEAD_DIM
    g = pl.program_id(1)
    qi = pl.program_id(2)
    t0 = qi * tq
    n_cmp = (seq - CMP_BLOCK) // CMP_STRIDE + 1
    n_slc = seq // SLC_BLOCK
    n_sel = min(SLC_TOPN, n_slc)
    half = CMP_STRIDE * dk
    scale = dk ** -0.5
    c_exp = scale * math.log2(math.e)
    rows = R * tq

    @pl.when(qi == 0)
    def _():
        pext_ref[...] = _pos_columns(lax.broadcasted_iota(jnp.int32, (seq, 1), 0), LANES)
        cpos = lax.broadcasted_iota(jnp.int32, (LANES, 1), 0) * CMP_STRIDE + (CMP_BLOCK - 1)
        for src_ref, w_ref, dst, pi in ((kcm_ref, wck_ref, kc_ref, 0), (vcm_ref, wcv_ref, vc_ref, 1)):
            xs_ref[...] = src_ref[...].astype(F32)
            x = jnp.concatenate([xs_ref[pl.ds(l, LANES, stride=CMP_STRIDE), :].astype(BF16)
                                 for l in range(CMP_STRIDE)], axis=1)
            ya = jnp.dot(x, w_ref[0:half, :], preferred_element_type=F32)
            yb = jnp.dot(x, w_ref[half:2 * half, :], preferred_element_type=F32)
            pos = jnp.broadcast_to(pos_ref[pi], (SUBLANES, 2 * half)).astype(BF16)
            pb = jnp.dot(pos, w_ref[...], preferred_element_type=F32)[0:1, :]
            comp = (ya + pltpu.roll(yb, LANES - 1, 0) + pb).astype(BF16)
            dst[...] = jnp.concatenate([comp, _pos_columns(cpos, LANES)], axis=1) if pi == 0 else comp

        head_of_row = lax.broadcasted_iota(jnp.int32, (rows, LANES), 0) >> _log2(tq)
        lane6 = lax.broadcasted_iota(jnp.int32, (rows, LANES), 1)
        term = jnp.where(lane6 < 3, lane6, lane6 - 3)
        q_ext = jnp.zeros((rows, LANES), F32)
        for r in range(R):
            for k in range(3):
                q_ext = jnp.where((head_of_row == r) & (term == k) & (lane6 < 6), slopes_ref[g, r, k], q_ext)
        qext_ref[...] = q_ext.astype(BF16)

    row_id = lax.broadcasted_iota(jnp.int32, (rows, 1), 0)
    t_col = t0 + (row_id & (tq - 1))
    q = q_ref[...]
    qs = jnp.concatenate([q[:, r * dk:(r + 1) * dk] for r in range(R)], axis=0)
    qa = jnp.concatenate([qs, qext_ref[...]], axis=1)

    n_idx = lax.broadcasted_iota(jnp.int32, (rows, LANES), 1)
    cvalid = (t_col >= n_idx * CMP_STRIDE + CMP_BLOCK - 1) & (n_idx < n_cmp)
    sc = lax.dot_general(qa, kc_ref[...], _NT, preferred_element_type=F32)
    sc = jnp.where(cvalid, sc, NEG_INF)
    mc = jnp.max(sc, axis=-1, keepdims=True)
    p_cmp = jnp.where(cvalid, jnp.exp2((sc - mc) * c_exp), 0.0)
    p_cmp = p_cmp * (1.0 / jnp.maximum(jnp.sum(p_cmp, axis=-1, keepdims=True), TINY))
    o_cmp = jnp.dot(p_cmp.astype(BF16), vc_ref[...], preferred_element_type=F32)

    psum = p_cmp[0:tq]
    for r in range(1, R):
        psum = psum + p_cmp[r * tq:(r + 1) * tq]
    jj = lax.broadcasted_iota(jnp.int32, (LANES, LANES), 0)
    nn = lax.broadcasted_iota(jnp.int32, (LANES, LANES), 1)
    ovl = _ind((nn * CMP_STRIDE < jj * SLC_BLOCK + SLC_BLOCK) & (nn * CMP_STRIDE + CMP_BLOCK > jj * SLC_BLOCK)
               & (jj < n_slc) & (nn < n_cmp), BF16)
    imp_t = jnp.zeros((LANES, tq), F32)
    for part in _split3(psum):
        imp_t = imp_t + lax.dot_general(ovl, part, _NT, preferred_element_type=F32)
    imp_t = imp_t[0:n_slc]
    j_i = lax.broadcasted_iota(jnp.int32, (n_slc, tq), 0)
    t_i = t0 + lax.broadcasted_iota(jnp.int32, (n_slc, tq), 1)
    cur = t_i >> _log2(SLC_BLOCK)
    causal = j_i * SLC_BLOCK <= t_i
    forced = (j_i == 0) | (j_i == cur) | (j_i == cur - 1)
    score = jnp.where(causal, jnp.where(forced, BIG, imp_t), -BIG)
    rank = jnp.zeros((n_slc, tq), jnp.int32)
    for jp in range(n_slc):
        row = score[jp:jp + 1, :]
        beats = (row > score) | ((row == score) & (j_i > jp))
        rank = rank + _ind(beats, jnp.int32)
    sel_t = jnp.where(rank < n_sel, 0.0, NEG_INF)
    sel_t = jnp.concatenate([sel_t, jnp.zeros((LANES - n_slc, tq), F32)], axis=0)
    sel_neg = sel_t.T.astype(BF16)
    tq_col = t0 + lax.broadcasted_iota(jnp.int32, (tq, 1), 0)

    def attend(s, v):
        p = jnp.exp2((s - jnp.max(s, axis=-1, keepdims=True)) * c_exp).astype(BF16)
        ones = _ind(lax.broadcasted_iota(jnp.int32, (v.shape[0], LANES), 1) == 0, BF16)
        pv = jnp.dot(p, jnp.concatenate([v, ones], axis=1), preferred_element_type=F32)
        return pv[:, 0:dk] * (1.0 / jnp.maximum(pv[:, dk:dk + 1], TINY))

    def slc_variant(nk):
        ka = jnp.concatenate([ks_ref[0:nk, :], pext_ref[0:nk, :]], axis=1)
        ej = lax.broadcasted_iota(jnp.int32, (LANES, nk), 0)
        ec = lax.broadcasted_iota(jnp.int32, (LANES, nk), 1)
        expand = _ind(ej == (ec >> _log2(SLC_BLOCK)), BF16)
        madd = jnp.dot(sel_neg, expand, preferred_element_type=F32)
        madd = madd + jnp.where(lax.broadcasted_iota(jnp.int32, (tq, nk), 1) <= tq_col, 0.0, NEG_INF)
        s = lax.dot_general(qa, ka, _NT, preferred_element_type=F32) + jnp.concatenate([madd] * R, axis=0)
        acc_ref[...] = attend(s, vs_ref[0:nk, :])

    variant = (t0 + tq - 1) // tk
    for vi in range(seq // tk):
        pl.when(variant == vi)(functools.partial(slc_variant, (vi + 1) * tk))
    o_slc = acc_ref[...]

    span = WINDOW + tq
    ws = pl.multiple_of(jnp.maximum(t0 - WINDOW, 0), tq)
    kwa = jnp.concatenate([kw_ref[pl.ds(ws, span), :], pext_ref[pl.ds(ws, span), :]], axis=1)
    dw_i = tq_col - (ws + lax.broadcasted_iota(jnp.int32, (tq, span), 1))
    wadd = jnp.where((dw_i >= 0) & (dw_i < WINDOW), 0.0, NEG_INF)
    sw = lax.dot_general(qa, kwa, _NT, preferred_element_type=F32) + jnp.concatenate([wadd] * R, axis=0)
    o_win = attend(sw, vw_ref[pl.ds(ws, span), :])

    gz = gz_ref[...]
    gsel = jnp.zeros((tq, 3 * R), F32)
    for gg in range(NSA_KV_GROUPS):
        gsel = jnp.where(g == gg, gz[:, 3 * R * gg:3 * R * (gg + 1)], gsel)
    gts = jax.nn.sigmoid(gsel)
    for r in range(R):
        sl = slice(r * tq, (r + 1) * tq)
        o = (gts[:, 3 * r:3 * r + 1] * o_cmp[sl] + gts[:, 3 * r + 1:3 * r + 2] * o_slc[sl]
             + gts[:, 3 * r + 2:3 * r + 3] * o_win[sl])
        o_ref[:, r * dk:(r + 1) * dk] = o.astype(o_ref.dtype)


def _nsa(slopes, z, gz, wck, wcv, pos, *, batch, seq, kv_col):
    G = NSA_KV_GROUPS
    R = NSA_HEADS // G
    dk = NSA_HEAD_DIM
    tq = NSA_TQ
    tk = NSA_TK
    nq = seq // tq
    assert seq // CMP_STRIDE == LANES and CMP_BLOCK == 2 * CMP_STRIDE and seq >= WINDOW + tq and seq % tk == 0
    kv0 = kv_col // dk

    def kv_spec(i):
        return pl.BlockSpec((seq, dk), lambda b, g, qi: (b, kv0 + i * G + g))

    w_spec = pl.BlockSpec((CMP_BLOCK * dk, dk), lambda b, g, qi: (0, 0))
    rows = R * tq
    assert dk == LANES and 3 * NSA_HEADS <= LANES
    return pl.pallas_call(
        functools.partial(_nsa_kernel, seq=seq, tq=tq, tk=tk, heads_per_group=R),
        grid=(batch, G, nq),
        in_specs=[
            pl.BlockSpec(memory_space=pltpu.SMEM),
            pl.BlockSpec((tq, R * dk), lambda b, g, qi: (b * nq + qi, g)),
            kv_spec(0), kv_spec(1), kv_spec(2), kv_spec(3), kv_spec(4), kv_spec(5),
            pl.BlockSpec((tq, LANES), lambda b, g, qi: (b * nq + qi, 0)),
            w_spec, w_spec,
            pl.BlockSpec((2, 1, CMP_BLOCK * dk), lambda b, g, qi: (0, 0, 0)),
        ],
        out_specs=pl.BlockSpec((tq, R * dk), lambda b, g, qi: (b * nq + qi, g)),
        out_shape=jax.ShapeDtypeStruct((batch * seq, NSA_HEADS * dk), BF16),
        scratch_shapes=[
            pltpu.VMEM((LANES, 2 * dk), BF16), pltpu.VMEM((LANES, dk), BF16),
            pltpu.VMEM((seq, LANES), BF16), pltpu.VMEM((seq, dk), F32), pltpu.VMEM((rows, dk), F32),
            pltpu.VMEM((rows, LANES), BF16),
        ],
        compiler_params=_params("parallel", "parallel", "arbitrary"),
        name="nsa",
    )(slopes, z, z, z, z, z, z, z, gz, wck, wcv, pos)


def _log_sigmoid(x):
    return jnp.minimum(x, 0.0) - jnp.log1p(jnp.exp(-jnp.abs(x)))


def _mlstm_kernel(bias_ref, q_ref, k_ref, v_ref, og_ref, gz_ref, nw_ref, o_ref, c_ref, *, seq, L, gate_lane, hps):
    dqk = ML_QK_DIM
    dv = ML_V_DIM
    nc = seq // L
    lane = lax.broadcasted_iota(jnp.int32, (L, LANES), 1)

    def gate_forms(blk, which, bias):
        col = jnp.sum(jnp.where(lane == which, blk, 0.0), axis=1, keepdims=True) + bias
        return col, jnp.broadcast_to(col, (L, LANES)).T[0:1, :]

    ext = dv + LANES
    ii = lax.broadcasted_iota(jnp.int32, (L, L), 0)
    kk = lax.broadcasted_iota(jnp.int32, (L, L), 1)
    causal = kk <= ii
    tri = _ind(causal, BF16)
    tri_t = _ind(ii <= kk, BF16)
    ones_slab = _ind(lax.broadcasted_iota(jnp.int32, (L, LANES), 1) == 0, BF16)
    c_ref[...] = jnp.zeros((hps, dqk, ext), F32)

    def cumsum_mats(lf_row, lf_col):
        lr = jnp.broadcast_to(lf_row, (L, L))
        lc = jnp.broadcast_to(lf_col, (L, L))
        b_row = jnp.zeros((L, L), F32)
        b_col = jnp.zeros((L, L), F32)
        for part in _split3(lr)[:2]:
            b_row = b_row + jnp.dot(part, tri_t, preferred_element_type=F32)
        for part in _split3(lc)[:2]:
            b_col = b_col + jnp.dot(tri, part, preferred_element_type=F32)
        return b_row, b_col

    def head_chunk(hh, r0, gblk, m):
        h_id = pl.program_id(1) * hps + hh
        i_lane = gate_lane + h_id
        f_lane = i_lane + ML_HEADS
        qc = q_ref[pl.ds(r0, L), hh * dqk:(hh + 1) * dqk]
        kc = k_ref[pl.ds(r0, L), hh * dqk:(hh + 1) * dqk] * (dqk ** -0.5)
        v_ext = jnp.concatenate([v_ref[pl.ds(r0, L), hh * dv:(hh + 1) * dv], ones_slab], axis=1)
        i_col, i_row = gate_forms(gblk, i_lane, bias_ref[0, h_id])
        f_col, f_row = gate_forms(gblk, f_lane, bias_ref[1, h_id])
        lf_col = _log_sigmoid(f_col)
        lf_row = _log_sigmoid(f_row)
        b_row, b_col = cumsum_mats(lf_row, lf_col)
        log_d = jnp.where(causal, b_col - b_row + i_row, NEG_INF)
        b_c = b_col[:, 0:1]
        m_inter = b_c + m
        m_t = jnp.maximum(m_inter, jnp.max(log_d, axis=-1, keepdims=True))
        d = jnp.exp(log_d - m_t)
        s = lax.dot_general(qc, kc, _NT, preferred_element_type=F32) * d
        w_inter = jnp.exp(m_inter - m_t)
        inter = jnp.dot(qc, c_ref[hh].astype(BF16), preferred_element_type=F32)
        intra = jnp.dot(s.astype(BF16), v_ext, preferred_element_type=F32)
        numden = w_inter * inter + intra
        den = numden[:, dv:dv + 1]
        h = numden[:, 0:dv] / jnp.maximum(jnp.abs(den), jnp.exp(-m_t))
        g_tot = b_row[0:1, L - 1:L]
        a_row = g_tot - b_row[0:1, :] + i_row
        a_col = g_tot - b_c + i_col
        m_new = jnp.maximum(g_tot + m, jnp.max(a_row, axis=-1, keepdims=True))
        decay = jnp.exp(g_tot + m - m_new)
        w_tok = jnp.exp(a_col - m_new)
        kw_t = (kc.astype(F32) * w_tok).T.astype(BF16)
        c_ref[hh] = decay * c_ref[hh] + jnp.dot(kw_t, v_ext, preferred_element_type=F32)
        hn = _rms(h, nw_ref[hh]) * jax.nn.sigmoid(og_ref[pl.ds(r0, L), hh * dv:(hh + 1) * dv].astype(F32))
        o_ref[pl.ds(r0, L), hh * dv:(hh + 1) * dv] = hn.astype(o_ref.dtype)
        return m_new

    def body(c, ms):
        r0 = pl.multiple_of(c * L, L)
        gblk = gz_ref[pl.ds(r0, L), :]
        return tuple(head_chunk(hh, r0, gblk, ms[hh]) for hh in range(hps))

    lax.fori_loop(0, nc, body, tuple(jnp.zeros((1, 1), F32) for _ in range(hps)))


def _mlstm(gate_bias, z, gz, norm_w, *, batch, seq, q_col, k_col, v_col, o_col, gate_lane):
    H = ML_HEADS
    dqk = ML_QK_DIM
    dv = ML_V_DIM
    L = min(ML_CHUNK_LEN, seq)
    hps = ML_HEADS_PER_STEP
    qw, vw = hps * dqk, hps * dv
    assert L % LANES == 0 and seq % L == 0 and gate_lane + 2 * H <= LANES and H % hps == 0
    assert q_col % qw == 0 and k_col % qw == 0 and v_col % vw == 0 and o_col % vw == 0
    return pl.pallas_call(
        functools.partial(_mlstm_kernel, seq=seq, L=L, gate_lane=gate_lane, hps=hps),
        grid=(batch, H // hps),
        in_specs=[
            pl.BlockSpec(memory_space=pltpu.SMEM),
            pl.BlockSpec((seq, qw), lambda b, h: (b, q_col // qw + h)),
            pl.BlockSpec((seq, qw), lambda b, h: (b, k_col // qw + h)),
            pl.BlockSpec((seq, vw), lambda b, h: (b, v_col // vw + h)),
            pl.BlockSpec((seq, vw), lambda b, h: (b, o_col // vw + h)),
            pl.BlockSpec((seq, LANES), lambda b, h: (b, 0)),
            pl.BlockSpec((hps, 1, dv), lambda b, h: (h, 0, 0)),
        ],
        out_specs=pl.BlockSpec((seq, vw), lambda b, h: (b, h)),
        out_shape=jax.ShapeDtypeStruct((batch * seq, H * dv), BF16),
        scratch_shapes=[pltpu.VMEM((hps, dqk, dv + LANES), F32)],
        compiler_params=_params("parallel", "parallel"),
        name="mlstm",
    )(gate_bias, z, z, z, z, gz, norm_w.reshape(H, 1, dv))


def _peer_pairs():
    return [(a, (PEER_TOPK + 1) // (a + 1)) for a in range(PEER_TOPK + 1)]


def _topk_multiset(s, k, with_rank=False):
    tt = s.shape[1]
    riota = lax.broadcasted_iota(jnp.int32, (k, tt), 0).astype(F32)
    v = jnp.full((k, tt), -jnp.inf, F32)
    taken = jnp.zeros((1, tt), F32)
    rank = jnp.full(s.shape, float(s.shape[0]), F32)
    rem = s
    for _ in range(k):
        m = jnp.max(rem, axis=0, keepdims=True)
        eq = rem == m
        cnt = jnp.sum(_ind(eq, F32), axis=0, keepdims=True)
        v = jnp.where((riota >= taken) & (riota < taken + cnt), m, v)
        if with_rank:
            rank = jnp.where(eq, taken, rank)
        taken = taken + cnt
        rem = jnp.where(eq, -jnp.inf, rem)
    return (v, rank) if with_rank else v


def _topk_lists(s, k, v_ref, rk_ref=None):
    rem = s
    rank = jnp.full(s.shape, float(s.shape[0]), F32)
    for kk in range(k):
        m = jnp.max(rem, axis=0, keepdims=True)
        eq = rem == m
        v_ref[kk:kk + 1, :] = m
        if rk_ref is not None:
            rank = jnp.where(eq, float(kk), rank)
        rem = jnp.where(eq, -jnp.inf, rem)
    if rk_ref is not None:
        rk_ref[...] = rank
    removed = jnp.sum(_ind(rem == -jnp.inf, F32), axis=0, keepdims=True)

    @pl.when(jnp.max(removed) > k)
    def _():
        v, rank_m = _topk_multiset(s, k, with_rank=True)
        v_ref[0:k, :] = v
        if rk_ref is not None:
            rk_ref[...] = rank_m


def _peer_topk_kernel(q_ref, khi_ref, klo_ref, cnt_ref, rank_ref, e1_ref, e2_ref, cand_ref, v1_ref, v2_ref, rk_ref):
    H = PEER_HEADS
    half = PEER_KEY_DIM // 2
    K = PEER_TOPK
    pairs = _peer_pairs()
    n_cand = sum(nb for _, nb in pairs)
    tt = q_ref.shape[0]
    for h in range(H):
        s_parts = []
        for side in range(2):
            qf = q_ref[:, (2 * h + side) * half:(2 * h + side + 1) * half]
            q_hi = qf.astype(BF16)
            q_lo = (qf - q_hi.astype(F32)).astype(BF16)
            k_hi = khi_ref[side, h]
            k_lo = klo_ref[side, h]
            s = (lax.dot_general(k_hi, q_hi, _NT, preferred_element_type=F32)
                 + lax.dot_general(k_hi, q_lo, _NT, preferred_element_type=F32)
                 + lax.dot_general(k_lo, q_hi, _NT, preferred_element_type=F32))
            s_parts.append(s)
        s1, s2 = s_parts
        _topk_lists(s1, K + 1, v1_ref)
        _topk_lists(s2, K + 1, v2_ref, rk_ref)
        v1 = v1_ref[0:K + 1, :]
        v2 = v2_ref[0:K + 1, :]
        rank2 = rk_ref[...]
        off = 0
        for a, nb in pairs:
            cand_ref[off:off + nb, :] = v1[a:a + 1, :] + v2[0:nb, :]
            off += nb
        cand_ref[n_cand:, :] = jnp.full((cand_ref.shape[0] - n_cand, tt), -jnp.inf, F32)
        rem = cand_ref[...]
        taken = jnp.zeros((1, tt), F32)
        tau16 = jnp.zeros((1, tt), F32)
        tau17 = jnp.zeros((1, tt), F32)
        zsum = jnp.zeros((1, tt), F32)
        cmax = v1[0:1, :] + v2[0:1, :]
        for _ in range(K + 1):
            m = jnp.max(rem, axis=0, keepdims=True)
            eq = rem == m
            cnt = jnp.sum(_ind(eq, F32), axis=0, keepdims=True)
            take = jnp.minimum(cnt, jnp.maximum(K - taken, 0.0))
            zsum = zsum + take * jnp.exp(m - cmax)
            after = taken + cnt
            tau16 = jnp.where((taken < K) & (after >= K), m, tau16)
            tau17 = jnp.where((taken <= K) & (after > K), m, tau17)
            taken = after
            rem = jnp.where(eq, -jnp.inf, rem)
        theta = 0.5 * (tau16 + tau17)
        r1 = theta - s1
        cnt1 = jnp.zeros(r1.shape, F32)
        for b in range(K + 1):
            cnt1 = cnt1 + _ind(v2[b:b + 1, :] >= r1, F32)
        cnt_ref[h] = cnt1
        rank_ref[h] = rank2.astype(rank_ref.dtype)
        e1_ref[h] = jnp.exp(s1 - v1[0:1, :])
        e2_ref[h] = (jnp.exp(s2 - v2[0:1, :]) / zsum).astype(e2_ref.dtype)


def _peer_topk(pq, k_hi, k_lo):
    n = pq.shape[0]
    H = PEER_HEADS
    nk = PEER_N_KEYS
    tt = min(PEER_TOPK_TT, n)
    n_cand = sum(nb for _, nb in _peer_pairs())
    cand_rows = -(-n_cand // SUBLANES) * SUBLANES
    list_rows = -(-(PEER_TOPK + 1) // SUBLANES) * SUBLANES
    key_spec = pl.BlockSpec((2, H, nk, PEER_KEY_DIM // 2), lambda i: (0, 0, 0, 0))
    hkt = pl.BlockSpec((H, nk, tt), lambda i: (0, 0, i))
    wide = jax.ShapeDtypeStruct((H, nk, n), F32)
    narrow = jax.ShapeDtypeStruct((H, nk, n), BF16)
    return pl.pallas_call(
        _peer_topk_kernel,
        grid=(n // tt,),
        in_specs=[pl.BlockSpec((tt, pq.shape[1]), lambda i: (i, 0)), key_spec, key_spec],
        out_specs=[hkt, hkt, hkt, hkt],
        out_shape=[wide, narrow, wide, narrow],
        scratch_shapes=[pltpu.VMEM((cand_rows, tt), F32), pltpu.VMEM((list_rows, tt), F32),
                        pltpu.VMEM((list_rows, tt), F32), pltpu.VMEM((nk, tt), F32)],
        compiler_params=_params("parallel"),
        name="peer_topk",
    )(pq, k_hi, k_lo)


def _bcast_packed(row):
    half = jnp.broadcast_to(row, (SUBLANES, LANES))
    return jnp.concatenate([half, half], axis=0).astype(BF16)


def _peer_mix_kernel(act_ref, vt_ref, cnt_ref, e1_ref, rank_ref, e2_ref, o_ref, acc_ref, a_ref, *, chunk):
    H = PEER_HEADS
    nk = PEER_N_KEYS
    te, tt = a_ref.shape
    e = pl.program_id(1)
    pack = 2 * SUBLANES

    @pl.when(e == 0)
    def _():
        acc_ref[...] = jnp.zeros(acc_ref.shape, F32)

    for i in range(te // nk):
        cnt_rows = [cnt_ref[h, i:i + 1, :] for h in range(H)]
        e1_rows = [e1_ref[h, i:i + 1, :] for h in range(H)]
        for lt in range(tt // LANES):
            ls = slice(lt * LANES, (lt + 1) * LANES)
            cntb = [_bcast_packed(cnt_rows[h][:, ls]) for h in range(H)]
            e1b = [_bcast_packed(e1_rows[h][:, ls]) for h in range(H)]
            for sg in range(nk // pack):
                rs = slice(sg * pack, (sg + 1) * pack)
                w = jnp.zeros((pack, LANES), BF16)
                for h in range(H):
                    hit = rank_ref[h, rs, ls] < cntb[h]
                    w = w + e1b[h] * jnp.where(hit, e2_ref[h, rs, ls], jnp.zeros((), BF16))
                rr = slice(i * nk + sg * pack, i * nk + (sg + 1) * pack)
                a_ref[rr, ls] = act_ref[rr, ls] * w
        done = (i + 1) * nk
        if done % chunk == 0:
            ks = slice(done - chunk, done)
            acc_ref[...] += jnp.dot(vt_ref[:, ks], a_ref[ks, :], preferred_element_type=F32)

    @pl.when(e == pl.num_programs(1) - 1)
    def _():
        o_ref[...] = acc_ref[...].T.astype(o_ref.dtype)


def _peer_mix(act_t, pvt, cnt1, rank2, e1, e2):
    n_exp, n = act_t.shape
    d = pvt.shape[0]
    H = PEER_HEADS
    nk = PEER_N_KEYS
    tt = min(PEER_TT, n)
    te = PEER_TE
    assert te == SUBLANES * nk and n_exp % te == 0

    def i1_spec():
        return pl.BlockSpec((H, SUBLANES, tt), lambda i, e: (0, e, i))

    def i2_spec():
        return pl.BlockSpec((H, nk, tt), lambda i, e: (0, 0, i))

    return pl.pallas_call(
        functools.partial(_peer_mix_kernel, chunk=PEER_DOT_CHUNK),
        grid=(n // tt, n_exp // te),
        in_specs=[
            pl.BlockSpec((te, tt), lambda i, e: (e, i)),
            pl.BlockSpec((d, te), lambda i, e: (0, e)),
            i1_spec(), i1_spec(), i2_spec(), i2_spec(),
        ],
        out_specs=pl.BlockSpec((tt, d), lambda i, e: (i, 0)),
        out_shape=jax.ShapeDtypeStruct((n, d), BF16),
        scratch_shapes=[pltpu.VMEM((d, tt), F32), pltpu.VMEM((te, tt), BF16)],
        compiler_params=_params("parallel", "arbitrary"),
        name="peer_mix",
    )(act_t, pvt, cnt1, e1, rank2, e2)


def _cast_transpose_kernel(x_ref, o_ref):
    o_ref[...] = x_ref[...].T.astype(o_ref.dtype)


def _cast_transpose(x, dtype):
    r, c = x.shape
    tr = min(CAST_T_ROWS, r)
    return pl.pallas_call(
        _cast_transpose_kernel,
        grid=(r // tr,),
        in_specs=[pl.BlockSpec((tr, c), lambda i: (i, 0))],
        out_specs=pl.BlockSpec((c, tr), lambda i: (0, i)),
        out_shape=jax.ShapeDtypeStruct((c, r), dtype),
        compiler_params=_params("parallel"),
        name="cast_transpose",
    )(x)


def _layer(h, mod3, norm_pre_mix, norm_post_mix, norm_pre_ffn, norm_post_ffn, w_in, cmp_pos, w_cmp_k, w_cmp_v,
           gate_bias, ml_norm_w, w_up_nsa, w_up_mlstm, w_out, peer_w_q, sub_keys, peer_u, peer_v, *, batch, seq):
    d = h.shape[1]
    G = NSA_KV_GROUPS
    R = NSA_HEADS // G
    dk = NSA_HEAD_DIM
    H = ML_HEADS
    nsa_w = NSA_HEADS * dk
    kv_w = 6 * G * dk
    mq_w = H * ML_QK_DIM
    mv_w = H * ML_V_DIM
    c_kv = nsa_w
    c_gn = c_kv + kv_w
    c_mq = c_gn + 3 * NSA_HEADS
    c_mk = c_mq + mq_w
    c_mv = c_mk + mq_w
    c_mo = c_mv + mv_w
    c_if = c_mo + mv_w
    c_mg = c_if + 2 * H
    n_gate = 3 * NSA_HEADS + 2 * H
    z_kv = nsa_w
    z_mq = z_kv + kv_w
    z_mk = z_mq + mq_w
    z_mv = z_mk + mq_w
    z_mo = z_mv + mv_w
    z_mg = z_mo + mv_w
    z_w = z_mg + 2 * d
    assert z_mq % INPROJ_BN == 0 and z_mg % INPROJ_BN == 0 and z_w % INPROJ_BN == 0
    z_shifts = [(0, z_mq // INPROJ_BN, 0), (z_mq // INPROJ_BN, z_mg // INPROJ_BN, 3 * NSA_HEADS),
                (z_mg // INPROJ_BN, z_w // INPROJ_BN, n_gate)]

    u = _prenorm(h, norm_pre_mix, mod3, seq)
    w_in_t = w_in.T
    z = _inproj(u, w_in_t, z_w, z_shifts)
    gz = _gate_proj(u, w_in_t, c_gn, 3 * NSA_HEADS, c_if, 2 * H)

    slopes = jnp.exp2(-8.0 * jnp.arange(1, NSA_HEADS + 1, dtype=F32) / NSA_HEADS) / (dk ** -0.5)
    slopes = jnp.stack([t.astype(F32) for t in _split3(slopes)], axis=-1).reshape(G, R, 3)
    o_nsa = _nsa(slopes, z, gz, w_cmp_k.astype(BF16), w_cmp_v.astype(BF16),
                 cmp_pos.reshape(2, 1, CMP_BLOCK * dk), batch=batch, seq=seq, kv_col=z_kv)

    h_ml = _mlstm(gate_bias, z, gz, ml_norm_w, batch=batch, seq=seq, q_col=z_mq, k_col=z_mk, v_col=z_mv,
                  o_col=z_mo, gate_lane=3 * NSA_HEADS)

    t1 = _matmul_w(o_nsa, w_up_nsa, BF16, gate=z, gate_col=z_mg)
    mix = _matmul_w(h_ml, w_up_mlstm, BF16, gate=z, gate_col=z_mg + d, add=t1)
    y = _matmul_w(mix, w_out, BF16)
    h1, u2, u2_t = _midnorm(h, y, norm_post_mix, norm_pre_ffn, mod3, seq)

    pq = _matmul_w(u2, peer_w_q, F32)
    k_hi = sub_keys.astype(BF16)
    k_lo = (sub_keys - k_hi.astype(F32)).astype(BF16)
    cnt1, rank2, e1, e2 = _peer_topk(pq, k_hi, k_lo)
    act_t = _matmul_a(peer_u, u2_t, BF16, gelu=True)
    y2 = _peer_mix(act_t, _cast_transpose(peer_v, BF16), cnt1, rank2, e1, e2)
    return _finalnorm(h1, y2, norm_post_ffn, mod3, seq)


def kernel(x, c, w_ada, b_ada, norm_pre_mix, norm_post_mix, norm_pre_ffn, norm_post_ffn, w_in, nsa_cmp_pos, nsa_w_cmp_k, nsa_w_cmp_v, mlstm_gate_bias, mlstm_norm_w, w_up_nsa, w_up_mlstm, w_out, peer_w_q, peer_sub_keys, peer_u, peer_v):
    batch, seq, d = x.shape
    depth = w_ada.shape[0]
    h = x.reshape(batch * seq, d)
    c_pad = jnp.concatenate([c, jnp.zeros((SUBLANES - batch % SUBLANES, d), c.dtype)], axis=0)
    for l in range(depth):
        mod = _adaln(c_pad, w_ada[l], b_ada[l])[:batch]
        mod3 = mod.reshape(batch, 6, d)
        h = _layer(h, mod3, norm_pre_mix[l], norm_post_mix[l], norm_pre_ffn[l], norm_post_ffn[l], w_in[l],
                   nsa_cmp_pos[l], nsa_w_cmp_k[l], nsa_w_cmp_v[l], mlstm_gate_bias[l], mlstm_norm_w[l],
                   w_up_nsa[l], w_up_mlstm[l], w_out[l], peer_w_q[l], peer_sub_keys[l], peer_u[l], peer_v[l],
                   batch=batch, seq=seq)
    return h.reshape(batch, seq, d).astype(x.dtype)
```

```python
import functools
import math

import jax
import jax.numpy as jnp
from jax import lax
from jax.experimental import pallas as pl
from jax.experimental.pallas import tpu as pltpu

D_MODEL = 4096
BATCH = 4
SEQ = 2048
NSA_HEADS = 16
NSA_KV_GROUPS = 4
NSA_HEAD_DIM = 128
CMP_BLOCK = 32
CMP_STRIDE = 16
SLC_BLOCK = 64
SLC_TOPN = 16
WINDOW = 512
ML_HEADS = 8
ML_QK_DIM = 256
ML_V_DIM = 512
PEER_HEADS = 8
PEER_KEY_DIM = 256
PEER_N_KEYS = 128
PEER_TOPK = 16

NEG_INF = -1e30
BIG = 1e9
TINY = 1e-30
EPS = 1e-6

F32 = jnp.float32
BF16 = jnp.bfloat16

V7X_VMEM_LIMIT_BYTES = 56 * 1024 * 1024
LANES = 128
SUBLANES = 8

ROW_TILE = 256
MM_BM = 1024
MM_BN = 1024
MMW_BN = 512
INPROJ_BN = 1024
MMW_VMEM_BUDGET_BYTES = 50 * 1024 * 1024
MMA_BM = 1024
CAST_T_ROWS = 512
ADA_BN = 512
NSA_TQ = 256
NSA_TK = 256
ML_CHUNK_LEN = 256
ML_HEADS_PER_STEP = 2
PEER_TOPK_TT = 256
PEER_TT = 512
PEER_TE = 1024
PEER_DOT_CHUNK = 256

_NT = (((1,), (1,)), ((), ()))


def _params(*sem):
    return pltpu.CompilerParams(dimension_semantics=sem, vmem_limit_bytes=V7X_VMEM_LIMIT_BYTES)


def _ind(cond, dtype):
    wide = jnp.int32 if jnp.issubdtype(dtype, jnp.integer) else F32
    return jnp.where(cond, jnp.ones((), wide), jnp.zeros((), wide)).astype(dtype)


def _log2(n):
    k = int(math.log2(n))
    assert 1 << k == n
    return k


def _split3(x):
    hi = x.astype(BF16)
    r1 = x - hi.astype(F32)
    mid = r1.astype(BF16)
    lo = (r1 - mid.astype(F32)).astype(BF16)
    return hi, mid, lo


def _adaln_kernel(c_ref, w_ref, b_ref, o_ref):
    c = c_ref[...]
    cond = (c * jax.nn.sigmoid(c)).astype(BF16)
    o_ref[...] = jnp.dot(cond, w_ref[...].astype(BF16), preferred_element_type=F32) + b_ref[...]


def _adaln(c_pad, w_ada, b_ada):
    rows, d = c_pad.shape
    n = w_ada.shape[1]
    bn = min(ADA_BN, n)
    return pl.pallas_call(
        _adaln_kernel,
        grid=(n // bn,),
        in_specs=[
            pl.BlockSpec((rows, d), lambda j: (0, 0)),
            pl.BlockSpec((d, bn), lambda j: (0, j)),
            pl.BlockSpec((1, bn), lambda j: (0, j)),
        ],
        out_specs=pl.BlockSpec((rows, bn), lambda j: (0, j)),
        out_shape=jax.ShapeDtypeStruct((rows, n), F32),
        compiler_params=_params("parallel"),
        name="adaln",
    )(c_pad, w_ada, b_ada.reshape(1, n))


def _rms(x, w):
    return x * lax.rsqrt(jnp.mean(x * x, axis=-1, keepdims=True) + EPS) * w


def _prenorm_kernel(x_ref, w_ref, mod_ref, u_ref):
    y = _rms(x_ref[...], w_ref[...])
    u_ref[...] = (y * (1.0 + mod_ref[1:2, :]) + mod_ref[0:1, :]).astype(u_ref.dtype)


def _prenorm(x2, w, mod3, seq):
    n, d = x2.shape
    tr = min(ROW_TILE, seq)
    return pl.pallas_call(
        _prenorm_kernel,
        grid=(n // tr,),
        in_specs=[
            pl.BlockSpec((tr, d), lambda i: (i, 0)),
            pl.BlockSpec((1, d), lambda i: (0, 0)),
            pl.BlockSpec((None, 6, d), lambda i: ((i * tr) // seq, 0, 0)),
        ],
        out_specs=pl.BlockSpec((tr, d), lambda i: (i, 0)),
        out_shape=jax.ShapeDtypeStruct((n, d), BF16),
        compiler_params=_params("parallel"),
        name="prenorm",
    )(x2, w.reshape(1, d), mod3)


def _midnorm_kernel(x_ref, y_ref, w1_ref, w2_ref, mod_ref, h_ref, u_ref, ut_ref):
    h = x_ref[...] + mod_ref[2:3, :] * _rms(y_ref[...].astype(F32), w1_ref[...])
    h_ref[...] = h
    u = _rms(h, w2_ref[...]) * (1.0 + mod_ref[4:5, :]) + mod_ref[3:4, :]
    u_ref[...] = u.astype(u_ref.dtype)
    ut_ref[...] = u.T.astype(ut_ref.dtype)


def _midnorm(x2, y, w_post, w_pre, mod3, seq):
    n, d = x2.shape
    tr = min(ROW_TILE, seq)
    row = pl.BlockSpec((tr, d), lambda i: (i, 0))
    vec = pl.BlockSpec((1, d), lambda i: (0, 0))
    return pl.pallas_call(
        _midnorm_kernel,
        grid=(n // tr,),
        in_specs=[row, row, vec, vec, pl.BlockSpec((None, 6, d), lambda i: ((i * tr) // seq, 0, 0))],
        out_specs=[row, row, pl.BlockSpec((d, tr), lambda i: (0, i))],
        out_shape=[jax.ShapeDtypeStruct((n, d), F32), jax.ShapeDtypeStruct((n, d), BF16),
                   jax.ShapeDtypeStruct((d, n), BF16)],
        compiler_params=_params("parallel"),
        name="midnorm",
    )(x2, y, w_post.reshape(1, d), w_pre.reshape(1, d), mod3)


def _finalnorm_kernel(h_ref, y_ref, w_ref, mod_ref, o_ref):
    o_ref[...] = h_ref[...] + mod_ref[5:6, :] * _rms(y_ref[...].astype(F32), w_ref[...])


def _finalnorm(h1, y, w, mod3, seq):
    n, d = h1.shape
    tr = min(ROW_TILE, seq)
    row = pl.BlockSpec((tr, d), lambda i: (i, 0))
    return pl.pallas_call(
        _finalnorm_kernel,
        grid=(n // tr,),
        in_specs=[row, row, pl.BlockSpec((1, d), lambda i: (0, 0)),
                  pl.BlockSpec((None, 6, d), lambda i: ((i * tr) // seq, 0, 0))],
        out_specs=row,
        out_shape=jax.ShapeDtypeStruct((n, d), F32),
        compiler_params=_params("parallel"),
        name="finalnorm",
    )(h1, y, w.reshape(1, d), mod3)


def _gelu(x):
    return 0.5 * x * (1.0 + lax.erf(x * (2.0 ** -0.5)))


def _mmw_kernel(*refs, has_gate, has_add, row_chunk):
    a_ref, w_ref = refs[0], refs[1]
    nxt = 2
    o_ref, wb_ref = refs[-2], refs[-1]

    @pl.when(pl.program_id(1) == 0)
    def _():
        for r0 in range(0, wb_ref.shape[0], row_chunk):
            wb_ref[r0:r0 + row_chunk, :] = w_ref[r0:r0 + row_chunk, :].astype(BF16)

    acc = jnp.dot(a_ref[...], wb_ref[...], preferred_element_type=F32)
    if has_gate:
        acc = acc * jax.nn.sigmoid(refs[nxt][...].astype(F32))
        nxt += 1
    if has_add:
        acc = acc + refs[nxt][...].astype(F32)
    o_ref[...] = acc.astype(o_ref.dtype)


def _matmul_w(a, w, out_dtype, *, gate=None, gate_col=0, add=None):
    m, k = a.shape
    n = w.shape[1]
    bm = min(MM_BM, m)
    n_side = (gate is not None) + (add is not None)
    out_bytes = jnp.dtype(out_dtype).itemsize

    def vmem_bytes(bn, w_bufs):
        return (w_bufs * k * bn * 4 + k * bn * 2 + 2 * bm * k * 2 + 2 * bm * bn * (out_bytes + 2 * n_side)
                + bm * bn * 4)

    if n % MM_BN == 0 and vmem_bytes(MM_BN, 1) <= MMW_VMEM_BUDGET_BYTES:
        bn, w_mode = MM_BN, pl.Buffered(1)
    else:
        bn, w_mode = min(MMW_BN, n), None
    assert n % bn == 0 and m % bm == 0
    in_specs = [pl.BlockSpec((bm, k), lambda j, i: (i, 0)),
                pl.BlockSpec((k, bn), lambda j, i: (0, j), pipeline_mode=w_mode)]
    args = [a, w]
    if gate is not None:
        goff = gate_col // bn
        assert goff * bn == gate_col
        in_specs.append(pl.BlockSpec((bm, bn), lambda j, i: (i, goff + j)))
        args.append(gate)
    if add is not None:
        in_specs.append(pl.BlockSpec((bm, bn), lambda j, i: (i, j)))
        args.append(add)
    return pl.pallas_call(
        functools.partial(_mmw_kernel, has_gate=gate is not None, has_add=add is not None, row_chunk=min(512, k)),
        grid=(n // bn, m // bm),
        in_specs=in_specs,
        out_specs=pl.BlockSpec((bm, bn), lambda j, i: (i, j)),
        out_shape=jax.ShapeDtypeStruct((m, n), out_dtype),
        scratch_shapes=[pltpu.VMEM((k, bn), BF16)],
        compiler_params=_params("parallel", "arbitrary"),
        name="matmul_w",
    )(*args)


def _inproj_kernel(a_ref, wt_ref, wtn_ref, o_ref, wb_ref, *, shifts, col_chunk):
    j = pl.program_id(0)
    k, bn = wb_ref.shape

    def fill(shift):
        for c0 in range(0, k, col_chunk):
            cs = slice(c0, c0 + col_chunk)
            if shift == 0:
                blk = wt_ref[:, cs]
            else:
                tall = jnp.concatenate([wt_ref[:, cs], wtn_ref[:, cs]], axis=0)
                blk = tall[shift:shift + bn, :]
            wb_ref[cs, :] = blk.T.astype(BF16)

    @pl.when(pl.program_id(1) == 0)
    def _():
        for j0, j1, shift in shifts:
            pl.when((j >= j0) & (j < j1))(functools.partial(fill, shift))

    acc = jnp.dot(a_ref[...], wb_ref[...], preferred_element_type=F32)
    o_ref[...] = acc.astype(o_ref.dtype)


def _inproj(a, wt, n_out, shifts):
    m, k = a.shape
    bm = min(MM_BM, m)
    bn = INPROJ_BN
    assert n_out % bn == 0 and m % bm == 0 and all(s % SUBLANES == 0 and s <= LANES for _, _, s in shifts)
    per = bn // LANES
    once = pl.Buffered(1)
    return pl.pallas_call(
        functools.partial(_inproj_kernel, shifts=shifts, col_chunk=min(512, k)),
        grid=(n_out // bn, m // bm),
        in_specs=[pl.BlockSpec((bm, k), lambda j, i: (i, 0)),
                  pl.BlockSpec((bn, k), lambda j, i: (j, 0), pipeline_mode=once),
                  pl.BlockSpec((LANES, k), lambda j, i: ((j + 1) * per, 0), pipeline_mode=once)],
        out_specs=pl.BlockSpec((bm, bn), lambda j, i: (i, j)),
        out_shape=jax.ShapeDtypeStruct((m, n_out), BF16),
        scratch_shapes=[pltpu.VMEM((k, bn), BF16)],
        compiler_params=_params("parallel", "arbitrary"),
        name="inproj",
    )(a, wt, wt)


def _gate_proj_kernel(a_ref, wa_ref, wb_ref, o_ref, *, n_a, off_b, n_b):
    k = wa_ref.shape[1]
    rows = jnp.concatenate([wa_ref[0:n_a, :], wb_ref[off_b:off_b + n_b, :],
                            jnp.zeros((LANES - n_a - n_b, k), F32)], axis=0).astype(BF16)
    o_ref[...] = lax.dot_general(a_ref[...], rows, _NT, preferred_element_type=F32)


def _gate_proj(a, wt, row_a, n_a, row_b, n_b):
    m, k = a.shape
    bm = min(MM_BM, m)
    off_b = row_b % LANES
    assert row_a % LANES == 0 and n_a % SUBLANES == 0 and off_b % SUBLANES == 0 and n_b % SUBLANES == 0
    assert off_b + n_b <= LANES and n_a + n_b <= LANES
    return pl.pallas_call(
        functools.partial(_gate_proj_kernel, n_a=n_a, off_b=off_b, n_b=n_b),
        grid=(m // bm,),
        in_specs=[pl.BlockSpec((bm, k), lambda i: (i, 0)),
                  pl.BlockSpec((LANES, k), lambda i: (row_a // LANES, 0)),
                  pl.BlockSpec((LANES, k), lambda i: (row_b // LANES, 0))],
        out_specs=pl.BlockSpec((bm, LANES), lambda i: (i, 0)),
        out_shape=jax.ShapeDtypeStruct((m, LANES), F32),
        compiler_params=_params("parallel"),
        name="gate_proj",
    )(a, wt, wt)


def _mma_kernel(a_ref, b_ref, o_ref, ab_ref, *, gelu, row_chunk):
    @pl.when(pl.program_id(1) == 0)
    def _():
        for r0 in range(0, ab_ref.shape[0], row_chunk):
            ab_ref[r0:r0 + row_chunk, :] = a_ref[r0:r0 + row_chunk, :].astype(BF16)

    acc = jnp.dot(ab_ref[...], b_ref[...], preferred_element_type=F32)
    if gelu:
        acc = _gelu(acc)
    o_ref[...] = acc.astype(o_ref.dtype)


def _matmul_a(a, b, out_dtype, gelu=False):
    m, k = a.shape
    n = b.shape[1]
    bm = min(MMA_BM, m)
    bn = min(MM_BN, n)
    return pl.pallas_call(
        functools.partial(_mma_kernel, gelu=gelu, row_chunk=min(128, bm)),
        grid=(m // bm, n // bn),
        in_specs=[pl.BlockSpec((bm, k), lambda i, j: (i, 0), pipeline_mode=pl.Buffered(1)),
                  pl.BlockSpec((k, bn), lambda i, j: (0, j))],
        out_specs=pl.BlockSpec((bm, bn), lambda i, j: (i, j)),
        out_shape=jax.ShapeDtypeStruct((m, n), out_dtype),
        scratch_shapes=[pltpu.VMEM((bm, k), BF16)],
        compiler_params=_params("parallel", "arbitrary"),
        name="matmul_a",
    )(a, b)


def _pos_columns(pos, width):
    lane = lax.broadcasted_iota(jnp.int32, (pos.shape[0], width), 1)
    coarse = (pos >> 6) << 6
    vals = jnp.where(lane < 3, coarse, jnp.where(lane < 6, pos & 63, 0))
    return vals.astype(F32).astype(BF16)


def _nsa_kernel(slopes_ref, q_ref, kcm_ref, vcm_ref, ks_ref, vs_ref, kw_ref, vw_ref, gz_ref, wck_ref, wcv_ref,
                pos_ref, o_ref, kc_ref, vc_ref, pext_ref, xs_ref, acc_ref, qext_ref, *, seq, tq, tk,
                heads_per_group):
    R = heads_per_group
    dk = NSA_HEAD_DIM
    g = pl.program_id(1)
    qi = pl.program_id(2)
    t0 = qi * tq
    n_cmp = (seq - CMP_BLOCK) // CMP_STRIDE + 1
    n_slc = seq // SLC_BLOCK
    n_sel = min(SLC_TOPN, n_slc)
    half = CMP_STRIDE * dk
    scale = dk ** -0.5
    c_exp = scale * math.log2(math.e)
    rows = R * tq

    @pl.when(qi == 0)
    def _():
        pext_ref[...] = _pos_columns(lax.broadcasted_iota(jnp.int32, (seq, 1), 0), LANES)
        cpos = lax.broadcasted_iota(jnp.int32, (LANES, 1), 0) * CMP_STRIDE + (CMP_BLOCK - 1)
        for src_ref, w_ref, dst, pi in ((kcm_ref, wck_ref, kc_ref, 0), (vcm_ref, wcv_ref, vc_ref, 1)):
            xs_ref[...] = src_ref[...].astype(F32)
            x = jnp.concatenate([xs_ref[pl.ds(l, LANES, stride=CMP_STRIDE), :].astype(BF16)
                                 for l in range(CMP_STRIDE)], axis=1)
            ya = jnp.dot(x, w_ref[0:half, :], preferred_element_type=F32)
            yb = jnp.dot(x, w_ref[half:2 * half, :], preferred_element_type=F32)
            pos = jnp.broadcast_to(pos_ref[pi], (SUBLANES, 2 * half)).astype(BF16)
            pb = jnp.dot(pos, w_ref[...], preferred_element_type=F32)[0:1, :]
            comp = (ya + pltpu.roll(yb, LANES - 1, 0) + pb).astype(BF16)
            dst[...] = jnp.concatenate([comp, _pos_columns(cpos, LANES)], axis=1) if pi == 0 else comp

        head_of_row = lax.broadcasted_iota(jnp.int32, (rows, LANES), 0) >> _log2(tq)
        lane6 = lax.broadcasted_iota(jnp.int32, (rows, LANES), 1)
        term = jnp.where(lane6 < 3, lane6, lane6 - 3)
        q_ext = jnp.zeros((rows, LANES), F32)
        for r in range(R):
            for k in range(3):
                q_ext = jnp.where((head_of_row == r) & (term == k) & (lane6 < 6), slopes_ref[g, r, k], q_ext)
        qext_ref[...] = q_ext.astype(BF16)

    row_id = lax.broadcasted_iota(jnp.int32, (rows, 1), 0)
    t_col = t0 + (row_id & (tq - 1))
    q = q_ref[...]
    qs = jnp.concatenate([q[:, r * dk:(r + 1) * dk] for r in range(R)], axis=0)
    qa = jnp.concatenate([qs, qext_ref[...]], axis=1)

    n_idx = lax.broadcasted_iota(jnp.int32, (rows, LANES), 1)
    cvalid = (t_col >= n_idx * CMP_STRIDE + CMP_BLOCK - 1) & (n_idx < n_cmp)
    sc = lax.dot_general(qa, kc_ref[...], _NT, preferred_element_type=F32)
    sc = jnp.where(cvalid, sc, NEG_INF)
    mc = jnp.max(sc, axis=-1, keepdims=True)
    p_cmp = jnp.where(cvalid, jnp.exp2((sc - mc) * c_exp), 0.0)
    p_cmp = p_cmp * (1.0 / jnp.maximum(jnp.sum(p_cmp, axis=-1, keepdims=True), TINY))
    o_cmp = jnp.dot(p_cmp.astype(BF16), vc_ref[...], preferred_element_type=F32)

    psum = p_cmp[0:tq]
    for r in range(1, R):
        psum = psum + p_cmp[r * tq:(r + 1) * tq]
    jj = lax.broadcasted_iota(jnp.int32, (LANES, LANES), 0)
    nn = lax.broadcasted_iota(jnp.int32, (LANES, LANES), 1)
    ovl = _ind((nn * CMP_STRIDE < jj * SLC_BLOCK + SLC_BLOCK) & (nn * CMP_STRIDE + CMP_BLOCK > jj * SLC_BLOCK)
               & (jj < n_slc) & (nn < n_cmp), BF16)
    imp_t = jnp.zeros((LANES, tq), F32)
    for part in _split3(psum):
        imp_t = imp_t + lax.dot_general(ovl, part, _NT, preferred_element_type=F32)
    imp_t = imp_t[0:n_slc]
    j_i = lax.broadcasted_iota(jnp.int32, (n_slc, tq), 0)
    t_i = t0 + lax.broadcasted_iota(jnp.int32, (n_slc, tq), 1)
    cur = t_i >> _log2(SLC_BLOCK)
    causal = j_i * SLC_BLOCK <= t_i
    forced = (j_i == 0) | (j_i == cur) | (j_i == cur - 1)
    score = jnp.where(causal, jnp.where(forced, BIG, imp_t), -BIG)
    rank = jnp.zeros((n_slc, tq), jnp.int32)
    for jp in range(n_slc):
        row = score[jp:jp + 1, :]
        beats = (row > score) | ((row == score) & (j_i > jp))
        rank = rank + _ind(beats, jnp.int32)
    sel_t = jnp.where(rank < n_sel, 0.0, NEG_INF)
    sel_t = jnp.concatenate([sel_t, jnp.zeros((LANES - n_slc, tq), F32)], axis=0)
    sel_neg = sel_t.T.astype(BF16)
    tq_col = t0 + lax.broadcasted_iota(jnp.int32, (tq, 1), 0)

    def attend(s, v):
        p = jnp.exp2((s - jnp.max(s, axis=-1, keepdims=True)) * c_exp).astype(BF16)
        ones = _ind(lax.broadcasted_iota(jnp.int32, (v.shape[0], LANES), 1) == 0, BF16)
        pv = jnp.dot(p, jnp.concatenate([v, ones], axis=1), preferred_element_type=F32)
        return pv[:, 0:dk] * (1.0 / jnp.maximum(pv[:, dk:dk + 1], TINY))

    def slc_variant(nk):
        ka = jnp.concatenate([ks_ref[0:nk, :], pext_ref[0:nk, :]], axis=1)
        ej = lax.broadcasted_iota(jnp.int32, (LANES, nk), 0)
        ec = lax.broadcasted_iota(jnp.int32, (LANES, nk), 1)
        expand = _ind(ej == (ec >> _log2(SLC_BLOCK)), BF16)
        madd = jnp.dot(sel_neg, expand, preferred_element_type=F32)
        madd = madd + jnp.where(lax.broadcasted_iota(jnp.int32, (tq, nk), 1) <= tq_col, 0.0, NEG_INF)
        s = lax.dot_general(qa, ka, _NT, preferred_element_type=F32) + jnp.concatenate([madd] * R, axis=0)
        acc_ref[...] = attend(s, vs_ref[0:nk, :])

    variant = (t0 + tq - 1) // tk
    for vi in range(seq // tk):
        pl.when(variant == vi)(functools.partial(slc_variant, (vi + 1) * tk))
    o_slc = acc_ref[...]

    span = WINDOW + tq
    ws = pl.multiple_of(jnp.maximum(t0 - WINDOW, 0), tq)
    kwa = jnp.concatenate([kw_ref[pl.ds(ws, span), :], pext_ref[pl.ds(ws, span), :]], axis=1)
    dw_i = tq_col - (ws + lax.broadcasted_iota(jnp.int32, (tq, span), 1))
    wadd = jnp.where((dw_i >= 0) & (dw_i < WINDOW), 0.0, NEG_INF)
    sw = lax.dot_general(qa, kwa, _NT, preferred_element_type=F32) + jnp.concatenate([wadd] * R, axis=0)
    o_win = attend(sw, vw_ref[pl.ds(ws, span), :])

    gz = gz_ref[...]
    gsel = jnp.zeros((tq, 3 * R), F32)
    for gg in range(NSA_KV_GROUPS):
        gsel = jnp.where(g == gg, gz[:, 3 * R * gg:3 * R * (gg + 1)], gsel)
    gts = jax.nn.sigmoid(gsel)
    for r in range(R):
        sl = slice(r * tq, (r + 1) * tq)
        o = (gts[:, 3 * r:3 * r + 1] * o_cmp[sl] + gts[:, 3 * r + 1:3 * r + 2] * o_slc[sl]
             + gts[:, 3 * r + 2:3 * r + 3] * o_win[sl])
        o_ref[:, r * dk:(r + 1) * dk] = o.astype(o_ref.dtype)


def _nsa(slopes, z, gz, wck, wcv, pos, *, batch, seq, kv_col):
    G = NSA_KV_GROUPS
    R = NSA_HEADS // G
    dk = NSA_HEAD_DIM
    tq = NSA_TQ
    tk = NSA_TK
    nq = seq // tq
    assert seq // CMP_STRIDE == LANES and CMP_BLOCK == 2 * CMP_STRIDE and seq >= WINDOW + tq and seq % tk == 0
    kv0 = kv_col // dk

    def kv_spec(i):
        return pl.BlockSpec((seq, dk), lambda b, g, qi: (b, kv0 + i * G + g))

    w_spec = pl.BlockSpec((CMP_BLOCK * dk, dk), lambda b, g, qi: (0, 0))
    rows = R * tq
    assert dk == LANES and 3 * NSA_HEADS <= LANES
    return pl.pallas_call(
        functools.partial(_nsa_kernel, seq=seq, tq=tq, tk=tk, heads_per_group=R),
        grid=(batch, G, nq),
        in_specs=[
            pl.BlockSpec(memory_space=pltpu.SMEM),
            pl.BlockSpec((tq, R * dk), lambda b, g, qi: (b * nq + qi, g)),
            kv_spec(0), kv_spec(1), kv_spec(2), kv_spec(3), kv_spec(4), kv_spec(5),
            pl.BlockSpec((tq, LANES), lambda b, g, qi: (b * nq + qi, 0)),
            w_spec, w_spec,
            pl.BlockSpec((2, 1, CMP_BLOCK * dk), lambda b, g, qi: (0, 0, 0)),
        ],
        out_specs=pl.BlockSpec((tq, R * dk), lambda b, g, qi: (b * nq + qi, g)),
        out_shape=jax.ShapeDtypeStruct((batch * seq, NSA_HEADS * dk), BF16),
        scratch_shapes=[
            pltpu.VMEM((LANES, 2 * dk), BF16), pltpu.VMEM((LANES, dk), BF16),
            pltpu.VMEM((seq, LANES), BF16), pltpu.VMEM((seq, dk), F32), pltpu.VMEM((rows, dk), F32),
            pltpu.VMEM((rows, LANES), BF16),
        ],
        compiler_params=_params("parallel", "parallel", "arbitrary"),
        name="nsa",
    )(slopes, z, z, z, z, z, z, z, gz, wck, wcv, pos)


def _log_sigmoid(x):
    return jnp.minimum(x, 0.0) - jnp.log1p(jnp.exp(-jnp.abs(x)))


def _mlstm_kernel(bias_ref, q_ref, k_ref, v_ref, og_ref, gz_ref, nw_ref, o_ref, c_ref, *, seq, L, gate_lane, hps):
    dqk = ML_QK_DIM
    dv = ML_V_DIM
    nc = seq // L
    lane = lax.broadcasted_iota(jnp.int32, (L, LANES), 1)

    def gate_forms(blk, which, bias):
        col = jnp.sum(jnp.where(lane == which, blk, 0.0), axis=1, keepdims=True) + bias
        return col, jnp.broadcast_to(col, (L, LANES)).T[0:1, :]

    ext = dv + LANES
    ii = lax.broadcasted_iota(jnp.int32, (L, L), 0)
    kk = lax.broadcasted_iota(jnp.int32, (L, L), 1)
    causal = kk <= ii
    tri = _ind(causal, BF16)
    tri_t = _ind(ii <= kk, BF16)
    ones_slab = _ind(lax.broadcasted_iota(jnp.int32, (L, LANES), 1) == 0, BF16)
    c_ref[...] = jnp.zeros((hps, dqk, ext), F32)

    def cumsum_mats(lf_row, lf_col):
        lr = jnp.broadcast_to(lf_row, (L, L))
        lc = jnp.broadcast_to(lf_col, (L, L))
        b_row = jnp.zeros((L, L), F32)
        b_col = jnp.zeros((L, L), F32)
        for part in _split3(lr)[:2]:
            b_row = b_row + jnp.dot(part, tri_t, preferred_element_type=F32)
        for part in _split3(lc)[:2]:
            b_col = b_col + jnp.dot(tri, part, preferred_element_type=F32)
        return b_row, b_col

    def head_chunk(hh, r0, gblk, m):
        h_id = pl.program_id(1) * hps + hh
        i_lane = gate_lane + h_id
        f_lane = i_lane + ML_HEADS
        qc = q_ref[pl.ds(r0, L), hh * dqk:(hh + 1) * dqk]
        kc = k_ref[pl.ds(r0, L), hh * dqk:(hh + 1) * dqk] * (dqk ** -0.5)
        v_ext = jnp.concatenate([v_ref[pl.ds(r0, L), hh * dv:(hh + 1) * dv], ones_slab], axis=1)
        i_col, i_row = gate_forms(gblk, i_lane, bias_ref[0, h_id])
        f_col, f_row = gate_forms(gblk, f_lane, bias_ref[1, h_id])
        lf_col = _log_sigmoid(f_col)
        lf_row = _log_sigmoid(f_row)
        b_row, b_col = cumsum_mats(lf_row, lf_col)
        log_d = jnp.where(causal, b_col - b_row + i_row, NEG_INF)
        b_c = b_col[:, 0:1]
        m_inter = b_c + m
        m_t = jnp.maximum(m_inter, jnp.max(log_d, axis=-1, keepdims=True))
        d = jnp.exp(log_d - m_t)
        s = lax.dot_general(qc, kc, _NT, preferred_element_type=F32) * d
        w_inter = jnp.exp(m_inter - m_t)
        inter = jnp.dot(qc, c_ref[hh].astype(BF16), preferred_element_type=F32)
        intra = jnp.dot(s.astype(BF16), v_ext, preferred_element_type=F32)
        numden = w_inter * inter + intra
        den = numden[:, dv:dv + 1]
        h = numden[:, 0:dv] / jnp.maximum(jnp.abs(den), jnp.exp(-m_t))
        g_tot = b_row[0:1, L - 1:L]
        a_row = g_tot - b_row[0:1, :] + i_row
        a_col = g_tot - b_c + i_col
        m_new = jnp.maximum(g_tot + m, jnp.max(a_row, axis=-1, keepdims=True))
        decay = jnp.exp(g_tot + m - m_new)
        w_tok = jnp.exp(a_col - m_new)
        kw_t = (kc.astype(F32) * w_tok).T.astype(BF16)
        c_ref[hh] = decay * c_ref[hh] + jnp.dot(kw_t, v_ext, preferred_element_type=F32)
        hn = _rms(h, nw_ref[hh]) * jax.nn.sigmoid(og_ref[pl.ds(r0, L), hh * dv:(hh + 1) * dv].astype(F32))
        o_ref[pl.ds(r0, L), hh * dv:(hh + 1) * dv] = hn.astype(o_ref.dtype)
        return m_new

    def body(c, ms):
        r0 = pl.multiple_of(c * L, L)
        gblk = gz_ref[pl.ds(r0, L), :]
        return tuple(head_chunk(hh, r0, gblk, ms[hh]) for hh in range(hps))

    lax.fori_loop(0, nc, body, tuple(jnp.zeros((1, 1), F32) for _ in range(hps)))


def _mlstm(gate_bias, z, gz, norm_w, *, batch, seq, q_col, k_col, v_col, o_col, gate_lane):
    H = ML_HEADS
    dqk = ML_QK_DIM
    dv = ML_V_DIM
    L = min(ML_CHUNK_LEN, seq)
    hps = ML_HEADS_PER_STEP
    qw, vw = hps * dqk, hps * dv
    assert L % LANES == 0 and seq % L == 0 and gate_lane + 2 * H <= LANES and H % hps == 0
    assert q_col % qw == 0 and k_col % qw == 0 and v_col % vw == 0 and o_col % vw == 0
    return pl.pallas_call(
        functools.partial(_mlstm_kernel, seq=seq, L=L, gate_lane=gate_lane, hps=hps),
        grid=(batch, H // hps),
        in_specs=[
            pl.BlockSpec(memory_space=pltpu.SMEM),
            pl.BlockSpec((seq, qw), lambda b, h: (b, q_col // qw + h)),
            pl.BlockSpec((seq, qw), lambda b, h: (b, k_col // qw + h)),
            pl.BlockSpec((seq, vw), lambda b, h: (b, v_col // vw + h)),
            pl.BlockSpec((seq, vw), lambda b, h: (b, o_col // vw + h)),
            pl.BlockSpec((seq, LANES), lambda b, h: (b, 0)),
            pl.BlockSpec((hps, 1, dv), lambda b, h: (h, 0, 0)),
        ],
        out_specs=pl.BlockSpec((seq, vw), lambda b, h: (b, h)),
        out_shape=jax.ShapeDtypeStruct((batch * seq, H * dv), BF16),
        scratch_shapes=[pltpu.VMEM((hps, dqk, dv + LANES), F32)],
        compiler_params=_params("parallel", "parallel"),
        name="mlstm",
    )(gate_bias, z, z, z, z, gz, norm_w.reshape(H, 1, dv))


def _peer_pairs():
    return [(a, (PEER_TOPK + 1) // (a + 1)) for a in range(PEER_TOPK + 1)]


def _topk_multiset(s, k, with_rank=False):
    tt = s.shape[1]
    riota = lax.broadcasted_iota(jnp.int32, (k, tt), 0).astype(F32)
    v = jnp.full((k, tt), -jnp.inf, F32)
    taken = jnp.zeros((1, tt), F32)
    rank = jnp.full(s.shape, float(s.shape[0]), F32)
    rem = s
    for _ in range(k):
        m = jnp.max(rem, axis=0, keepdims=True)
        eq = rem == m
        cnt = jnp.sum(_ind(eq, F32), axis=0, keepdims=True)
        v = jnp.where((riota >= taken) & (riota < taken + cnt), m, v)
        if with_rank:
            rank = jnp.where(eq, taken, rank)
        taken = taken + cnt
        rem = jnp.where(eq, -jnp.inf, rem)
    return (v, rank) if with_rank else v


def _peer_topk_kernel(q_ref, khi_ref, klo_ref, cnt_ref, rank_ref, e1_ref, e2_ref, cand_ref):
    H = PEER_HEADS
    half = PEER_KEY_DIM // 2
    K = PEER_TOPK
    pairs = _peer_pairs()
    n_cand = sum(nb for _, nb in pairs)
    tt = q_ref.shape[0]
    for h in range(H):
        s_parts = []
        for side in range(2):
            qf = q_ref[:, (2 * h + side) * half:(2 * h + side + 1) * half]
            q_hi = qf.astype(BF16)
            q_lo = (qf - q_hi.astype(F32)).astype(BF16)
            k_hi = khi_ref[side, h]
            k_lo = klo_ref[side, h]
            s = (lax.dot_general(k_hi, q_hi, _NT, preferred_element_type=F32)
                 + lax.dot_general(k_hi, q_lo, _NT, preferred_element_type=F32)
                 + lax.dot_general(k_lo, q_hi, _NT, preferred_element_type=F32))
            s_parts.append(s)
        s1, s2 = s_parts
        v1 = _topk_multiset(s1, K + 1)
        v2, rank2 = _topk_multiset(s2, K + 1, with_rank=True)
        off = 0
        for a, nb in pairs:
            cand_ref[off:off + nb, :] = v1[a:a + 1, :] + v2[0:nb, :]
            off += nb
        cand_ref[n_cand:, :] = jnp.full((cand_ref.shape[0] - n_cand, tt), -jnp.inf, F32)
        rem = cand_ref[...]
        taken = jnp.zeros((1, tt), F32)
        tau16 = jnp.zeros((1, tt), F32)
        tau17 = jnp.zeros((1, tt), F32)
        zsum = jnp.zeros((1, tt), F32)
        cmax = v1[0:1, :] + v2[0:1, :]
        for _ in range(K + 1):
            m = jnp.max(rem, axis=0, keepdims=True)
            eq = rem == m
            cnt = jnp.sum(_ind(eq, F32), axis=0, keepdims=True)
            take = jnp.minimum(cnt, jnp.maximum(K - taken, 0.0))
            zsum = zsum + take * jnp.exp(m - cmax)
            after = taken + cnt
            tau16 = jnp.where((taken < K) & (after >= K), m, tau16)
            tau17 = jnp.where((taken <= K) & (after > K), m, tau17)
            taken = after
            rem = jnp.where(eq, -jnp.inf, rem)
        theta = 0.5 * (tau16 + tau17)
        r1 = theta - s1
        cnt1 = jnp.zeros(r1.shape, F32)
        for b in range(K + 1):
            cnt1 = cnt1 + _ind(v2[b:b + 1, :] >= r1, F32)
        cnt_ref[h] = cnt1
        rank_ref[h] = rank2.astype(rank_ref.dtype)
        e1_ref[h] = jnp.exp(s1 - v1[0:1, :])
        e2_ref[h] = (jnp.exp(s2 - v2[0:1, :]) / zsum).astype(e2_ref.dtype)


def _peer_topk(pq, k_hi, k_lo):
    n = pq.shape[0]
    H = PEER_HEADS
    nk = PEER_N_KEYS
    tt = min(PEER_TOPK_TT, n)
    n_cand = sum(nb for _, nb in _peer_pairs())
    cand_rows = -(-n_cand // SUBLANES) * SUBLANES
    key_spec = pl.BlockSpec((2, H, nk, PEER_KEY_DIM // 2), lambda i: (0, 0, 0, 0))
    hkt = pl.BlockSpec((H, nk, tt), lambda i: (0, 0, i))
    wide = jax.ShapeDtypeStruct((H, nk, n), F32)
    narrow = jax.ShapeDtypeStruct((H, nk, n), BF16)
    return pl.pallas_call(
        _peer_topk_kernel,
        grid=(n // tt,),
        in_specs=[pl.BlockSpec((tt, pq.shape[1]), lambda i: (i, 0)), key_spec, key_spec],
        out_specs=[hkt, hkt, hkt, hkt],
        out_shape=[wide, narrow, wide, narrow],
        scratch_shapes=[pltpu.VMEM((cand_rows, tt), F32)],
        compiler_params=_params("parallel"),
        name="peer_topk",
    )(pq, k_hi, k_lo)


def _bcast_packed(row):
    half = jnp.broadcast_to(row, (SUBLANES, LANES))
    return jnp.concatenate([half, half], axis=0).astype(BF16)


def _peer_mix_kernel(act_ref, vt_ref, cnt_ref, e1_ref, rank_ref, e2_ref, o_ref, acc_ref, a_ref, *, chunk):
    H = PEER_HEADS
    nk = PEER_N_KEYS
    te, tt = a_ref.shape
    e = pl.program_id(1)
    pack = 2 * SUBLANES

    @pl.when(e == 0)
    def _():
        acc_ref[...] = jnp.zeros(acc_ref.shape, F32)

    for i in range(te // nk):
        cnt_rows = [cnt_ref[h, i:i + 1, :] for h in range(H)]
        e1_rows = [e1_ref[h, i:i + 1, :] for h in range(H)]
        for lt in range(tt // LANES):
            ls = slice(lt * LANES, (lt + 1) * LANES)
            cntb = [_bcast_packed(cnt_rows[h][:, ls]) for h in range(H)]
            e1b = [_bcast_packed(e1_rows[h][:, ls]) for h in range(H)]
            for sg in range(nk // pack):
                rs = slice(sg * pack, (sg + 1) * pack)
                w = jnp.zeros((pack, LANES), BF16)
                for h in range(H):
                    hit = rank_ref[h, rs, ls] < cntb[h]
                    w = w + e1b[h] * jnp.where(hit, e2_ref[h, rs, ls], jnp.zeros((), BF16))
                rr = slice(i * nk + sg * pack, i * nk + (sg + 1) * pack)
                a_ref[rr, ls] = act_ref[rr, ls] * w
        done = (i + 1) * nk
        if done % chunk == 0:
            ks = slice(done - chunk, done)
            acc_ref[...] += jnp.dot(vt_ref[:, ks], a_ref[ks, :], preferred_element_type=F32)

    @pl.when(e == pl.num_programs(1) - 1)
    def _():
        o_ref[...] = acc_ref[...].T.astype(o_ref.dtype)


def _peer_mix(act_t, pvt, cnt1, rank2, e1, e2):
    n_exp, n = act_t.shape
    d = pvt.shape[0]
    H = PEER_HEADS
    nk = PEER_N_KEYS
    tt = min(PEER_TT, n)
    te = PEER_TE
    assert te == SUBLANES * nk and n_exp % te == 0

    def i1_spec():
        return pl.BlockSpec((H, SUBLANES, tt), lambda i, e: (0, e, i))

    def i2_spec():
        return pl.BlockSpec((H, nk, tt), lambda i, e: (0, 0, i))

    return pl.pallas_call(
        functools.partial(_peer_mix_kernel, chunk=PEER_DOT_CHUNK),
        grid=(n // tt, n_exp // te),
        in_specs=[
            pl.BlockSpec((te, tt), lambda i, e: (e, i)),
            pl.BlockSpec((d, te), lambda i, e: (0, e)),
            i1_spec(), i1_spec(), i2_spec(), i2_spec(),
        ],
        out_specs=pl.BlockSpec((tt, d), lambda i, e: (i, 0)),
        out_shape=jax.ShapeDtypeStruct((n, d), BF16),
        scratch_shapes=[pltpu.VMEM((d, tt), F32), pltpu.VMEM((te, tt), BF16)],
        compiler_params=_params("parallel", "arbitrary"),
        name="peer_mix",
    )(act_t, pvt, cnt1, e1, rank2, e2)


def _cast_transpose_kernel(x_ref, o_ref):
    o_ref[...] = x_ref[...].T.astype(o_ref.dtype)


def _cast_transpose(x, dtype):
    r, c = x.shape
    tr = min(CAST_T_ROWS, r)
    return pl.pallas_call(
        _cast_transpose_kernel,
        grid=(r // tr,),
        in_specs=[pl.BlockSpec((tr, c), lambda i: (i, 0))],
        out_specs=pl.BlockSpec((c, tr), lambda i: (0, i)),
        out_shape=jax.ShapeDtypeStruct((c, r), dtype),
        compiler_params=_params("parallel"),
        name="cast_transpose",
    )(x)


def _layer(h, mod3, norm_pre_mix, norm_post_mix, norm_pre_ffn, norm_post_ffn, w_in, cmp_pos, w_cmp_k, w_cmp_v,
           gate_bias, ml_norm_w, w_up_nsa, w_up_mlstm, w_out, peer_w_q, sub_keys, peer_u, peer_v, *, batch, seq):
    d = h.shape[1]
    G = NSA_KV_GROUPS
    R = NSA_HEADS // G
    dk = NSA_HEAD_DIM
    H = ML_HEADS
    nsa_w = NSA_HEADS * dk
    kv_w = 6 * G * dk
    mq_w = H * ML_QK_DIM
    mv_w = H * ML_V_DIM
    c_kv = nsa_w
    c_gn = c_kv + kv_w
    c_mq = c_gn + 3 * NSA_HEADS
    c_mk = c_mq + mq_w
    c_mv = c_mk + mq_w
    c_mo = c_mv + mv_w
    c_if = c_mo + mv_w
    c_mg = c_if + 2 * H
    n_gate = 3 * NSA_HEADS + 2 * H
    z_kv = nsa_w
    z_mq = z_kv + kv_w
    z_mk = z_mq + mq_w
    z_mv = z_mk + mq_w
    z_mo = z_mv + mv_w
    z_mg = z_mo + mv_w
    z_w = z_mg + 2 * d
    assert z_mq % INPROJ_BN == 0 and z_mg % INPROJ_BN == 0 and z_w % INPROJ_BN == 0
    z_shifts = [(0, z_mq // INPROJ_BN, 0), (z_mq // INPROJ_BN, z_mg // INPROJ_BN, 3 * NSA_HEADS),
                (z_mg // INPROJ_BN, z_w // INPROJ_BN, n_gate)]

    u = _prenorm(h, norm_pre_mix, mod3, seq)
    w_in_t = w_in.T
    z = _inproj(u, w_in_t, z_w, z_shifts)
    gz = _gate_proj(u, w_in_t, c_gn, 3 * NSA_HEADS, c_if, 2 * H)

    slopes = jnp.exp2(-8.0 * jnp.arange(1, NSA_HEADS + 1, dtype=F32) / NSA_HEADS) / (dk ** -0.5)
    slopes = jnp.stack([t.astype(F32) for t in _split3(slopes)], axis=-1).reshape(G, R, 3)
    o_nsa = _nsa(slopes, z, gz, w_cmp_k.astype(BF16), w_cmp_v.astype(BF16),
                 cmp_pos.reshape(2, 1, CMP_BLOCK * dk), batch=batch, seq=seq, kv_col=z_kv)

    h_ml = _mlstm(gate_bias, z, gz, ml_norm_w, batch=batch, seq=seq, q_col=z_mq, k_col=z_mk, v_col=z_mv,
                  o_col=z_mo, gate_lane=3 * NSA_HEADS)

    t1 = _matmul_w(o_nsa, w_up_nsa, BF16, gate=z, gate_col=z_mg)
    mix = _matmul_w(h_ml, w_up_mlstm, BF16, gate=z, gate_col=z_mg + d, add=t1)
    y = _matmul_w(mix, w_out, BF16)
    h1, u2, u2_t = _midnorm(h, y, norm_post_mix, norm_pre_ffn, mod3, seq)

    pq = _matmul_w(u2, peer_w_q, F32)
    k_hi = sub_keys.astype(BF16)
    k_lo = (sub_keys - k_hi.astype(F32)).astype(BF16)
    cnt1, rank2, e1, e2 = _peer_topk(pq, k_hi, k_lo)
    act_t = _matmul_a(peer_u, u2_t, BF16, gelu=True)
    y2 = _peer_mix(act_t, _cast_transpose(peer_v, BF16), cnt1, rank2, e1, e2)
    return _finalnorm(h1, y2, norm_post_ffn, mod3, seq)


def kernel(x, c, w_ada, b_ada, norm_pre_mix, norm_post_mix, norm_pre_ffn, norm_post_ffn, w_in, nsa_cmp_pos, nsa_w_cmp_k, nsa_w_cmp_v, mlstm_gate_bias, mlstm_norm_w, w_up_nsa, w_up_mlstm, w_out, peer_w_q, peer_sub_keys, peer_u, peer_v):
    batch, seq, d = x.shape
    depth = w_ada.shape[0]
    h = x.reshape(batch * seq, d)
    c_pad = jnp.concatenate([c, jnp.zeros((SUBLANES - batch % SUBLANES, d), c.dtype)], axis=0)
    for l in range(depth):
        mod = _adaln(c_pad, w_ada[l], b_ada[l])[:batch]
        mod3 = mod.reshape(batch, 6, d)
        h = _layer(h, mod3, norm_pre_mix[l], norm_post_mix[l], norm_pre_ffn[l], norm_post_ffn[l], w_in[l],
                   nsa_cmp_pos[l], nsa_w_cmp_k[l], nsa_w_cmp_v[l], mlstm_gate_bias[l], mlstm_norm_w[l],
                   w_up_nsa[l], w_up_mlstm[l], w_out[l], peer_w_q[l], peer_sub_keys[l], peer_u[l], peer_v[l],
                   batch=batch, seq=seq)
    return h.reshape(batch, seq, d).astype(x.dtype)
```
